```python
import math
import jax, jax.numpy as jnp
from jax import lax
import numpy as np


D_MODEL = 2048
BATCH = 8
SEQ = 2048
DEPTH = 2

GRID_W = 64
CTX_LEN = 256
EPS = 1e-6
ROPE_THETA = 10000.0
Q_BLOCK = 128
D_FF = 5632
N_MOD = 9
ADA_STD = 0.5

MLA_HEADS = 8
MLA_Q_RANK = 512
MLA_KV_RANK = 256
MLA_NOPE = 128
MLA_ROPE = 64
MLA_V = 128
MLA_SCALE = (MLA_NOPE + MLA_ROPE) ** -0.5

DIFF_HEADS = 4
DIFF_QK = 64
DIFF_V = 2 * DIFF_QK
DIFF_SCALE = DIFF_QK ** -0.5
DIFF_LAMBDA_STD = 0.1

FOURIER_GROUPS = 4
FOURIER_CH = 128

MLA_WIDTH = MLA_HEADS * MLA_V
DIFF_WIDTH = DIFF_HEADS * DIFF_V
FOURIER_WIDTH = FOURIER_GROUPS * FOURIER_CH
MIX_WIDTH = MLA_WIDTH + DIFF_WIDTH + FOURIER_WIDTH

OFF_KV_LAT = 0
OFF_K_ROPE = OFF_KV_LAT + MLA_KV_RANK
OFF_DK = OFF_K_ROPE + MLA_ROPE
OFF_DV = OFF_DK + 2 * DIFF_HEADS * DIFF_QK
KV_COLS = OFF_DV + DIFF_HEADS * DIFF_V
OFF_Q_LAT = 0
OFF_DQ = OFF_Q_LAT + MLA_Q_RANK
OFF_FOUR = OFF_DQ + 2 * DIFF_HEADS * DIFF_QK
Q_COLS = OFF_FOUR + FOURIER_WIDTH
IN_COLS = KV_COLS + Q_COLS

kernel_name = 'hybrid_mla_diff_fourier_macaron_dit'


def rmsnorm(x, g):
    xf = x.astype(jnp.float32)
    y = xf * lax.rsqrt(jnp.mean(xf * xf, axis=-1, keepdims=True) + EPS)
    return y.astype(x.dtype) * g


def modulate(x, g, shift, scale):
    return rmsnorm(x, g) * (1 + scale) + shift


def swiglu(h, wg, wu, wd):
    return (jax.nn.silu(h @ wg) * (h @ wu)) @ wd


def ffn_sublayer(z, shift, scale, gate, g, wg, wu, wd):
    return z + 0.5 * gate * swiglu(modulate(z, g, shift, scale), wg, wu, wd)


def rope_1d(x, pos):
    d = x.shape[-1]
    half = d // 2
    inv = ROPE_THETA ** (-2.0 * jnp.arange(half, dtype=jnp.float32) / d)
    ang = pos.astype(jnp.float32)[:, None] * inv[None, :]
    cos = jnp.cos(ang)[:, None, :].astype(x.dtype)
    sin = jnp.sin(ang)[:, None, :].astype(x.dtype)
    x1, x2 = x[..., :half], x[..., half:]
    return jnp.concatenate([x1 * cos - x2 * sin, x1 * sin + x2 * cos], axis=-1)


def rope_2d(x, pos):
    if pos is None:
        return x
    row, col = pos
    half = x.shape[-1] // 2
    return jnp.concatenate([rope_1d(x[..., :half], row), rope_1d(x[..., half:], col)], axis=-1)


def kv_side(pk, kv_norm, w_ukv, pos):
    B, S = pk.shape[:2]
    c_kv = pk[..., OFF_KV_LAT:OFF_K_ROPE]
    k_rope = pk[..., OFF_K_ROPE:OFF_DK]
    dk = pk[..., OFF_DK:OFF_DV]
    dv = pk[..., OFF_DV:KV_COLS]
    kv = (rmsnorm(c_kv, kv_norm) @ w_ukv).reshape(B, S, MLA_HEADS, MLA_NOPE + MLA_V)
    k_nope, mla_v = kv[..., :MLA_NOPE], kv[..., MLA_NOPE:]
    k_rope = rope_2d(k_rope[:, :, None, :], pos)
    mla_k = jnp.concatenate([k_nope, jnp.broadcast_to(k_rope, (B, S, MLA_HEADS, MLA_ROPE))], axis=-1)
    dk = rope_2d(dk.reshape(B, S, 2 * DIFF_HEADS, DIFF_QK), pos).reshape(B, S, DIFF_HEADS, 2, DIFF_QK)
    dv = dv.reshape(B, S, DIFF_HEADS, DIFF_V)
    return (mla_k, mla_v, dk[..., 0, :], dk[..., 1, :], dv)


def q_side(pq, q_norm, w_uq, pos):
    B, S = pq.shape[:2]
    c_q = pq[..., OFF_Q_LAT:OFF_DQ]
    dq = pq[..., OFF_DQ:OFF_FOUR]
    u = pq[..., OFF_FOUR:Q_COLS]
    q = (rmsnorm(c_q, q_norm) @ w_uq).reshape(B, S, MLA_HEADS, MLA_NOPE + MLA_ROPE)
    mla_q = jnp.concatenate([q[..., :MLA_NOPE], rope_2d(q[..., MLA_NOPE:], pos)], axis=-1)
    dq = rope_2d(dq.reshape(B, S, 2 * DIFF_HEADS, DIFF_QK), pos).reshape(B, S, DIFF_HEADS, 2, DIFF_QK)
    return (mla_q, dq[..., 0, :], dq[..., 1, :], u)


def softmax_f32(s):
    return jax.nn.softmax(s.astype(jnp.float32), axis=-1)


def mla_block(q, k, v):
    s = jnp.einsum('bqhd,bkhd->bhqk', q, k) * MLA_SCALE
    p = softmax_f32(s).astype(v.dtype)
    return jnp.einsum('bhqk,bkhd->bqhd', p, v)


def diff_block(q1, q2, k1, k2, v, lam):
    s1 = jnp.einsum('bqhd,bkhd->bhqk', q1, k1) * DIFF_SCALE
    s2 = jnp.einsum('bqhd,bkhd->bhqk', q2, k2) * DIFF_SCALE
    p = (softmax_f32(s1) - lam * softmax_f32(s2)).astype(v.dtype)
    return jnp.einsum('bhqk,bkhd->bqhd', p, v)


def over_query_blocks(fn, qs):
    B, S = qs[0].shape[:2]
    blk = min(Q_BLOCK, S)
    nb = S // blk
    xs = tuple(jnp.swapaxes(q.reshape(B, nb, blk, *q.shape[2:]), 0, 1) for q in qs)
    out = lax.map(lambda t: fn(*t), xs)
    return jnp.swapaxes(out, 0, 1).reshape(B, S, *out.shape[3:])


def fourier_mix(u):
    B, S = u.shape[:2]
    g = u.reshape(B, S, FOURIER_GROUPS, FOURIER_CH).astype(jnp.float32)
    f = jnp.fft.fft2(g, axes=(1, 3), norm='ortho').real
    return f.reshape(B, S, FOURIER_WIDTH).astype(u.dtype)


def token_mix(qp, kvp, lam, lam_init, subln, w_out):
    mla_q, dq1, dq2, u = qp
    mla_k, mla_v, dk1, dk2, dv = kvp
    B, S = u.shape[:2]
    o_mla = over_query_blocks(lambda q: mla_block(q, mla_k, mla_v), (mla_q,))
    o_diff = over_query_blocks(lambda a, b: diff_block(a, b, dk1, dk2, dv, lam), (dq1, dq2))
    o_diff = rmsnorm(o_diff, subln) * (1.0 - lam_init)
    o = jnp.concatenate([o_mla.reshape(B, S, MLA_WIDTH), o_diff.reshape(B, S, DIFF_WIDTH), fourier_mix(u)], axis=-1)
    return o @ w_out


def setup_inputs(seed: int = 0) -> dict:
    key = jax.random.key(seed)
    ks = jax.random.split(key, 20)

    def nrm(k, shape, std):
        return jax.random.normal(k, shape, jnp.float32) * std

    return {
        'x': nrm(ks[0], (BATCH, SEQ, D_MODEL), 1.0),
        'c': nrm(ks[1], (BATCH, D_MODEL), 1.0),
        'ctx': nrm(ks[2], (BATCH, CTX_LEN, D_MODEL), 1.0),
        'c_ctx': nrm(ks[3], (D_MODEL,), 1.0),
        'ada_w': nrm(ks[4], (DEPTH, D_MODEL, N_MOD * D_MODEL), ADA_STD * D_MODEL ** -0.5),
        'ada_b': nrm(ks[5], (DEPTH, N_MOD * D_MODEL), 0.02),
        'norm_g': 1.0 + nrm(ks[6], (DEPTH, 3, D_MODEL), 0.02),
        'ffn_wg': nrm(ks[7], (DEPTH, 2, D_MODEL, D_FF), D_MODEL ** -0.5),
        'ffn_wu': nrm(ks[8], (DEPTH, 2, D_MODEL, D_FF), D_MODEL ** -0.5),
        'ffn_wd': nrm(ks[9], (DEPTH, 2, D_FF, D_MODEL), D_FF ** -0.5),
        'w_in': nrm(ks[10], (DEPTH, D_MODEL, IN_COLS), D_MODEL ** -0.5),
        'mla_q_norm': 1.0 + nrm(ks[11], (DEPTH, MLA_Q_RANK), 0.02),
        'mla_kv_norm': 1.0 + nrm(ks[12], (DEPTH, MLA_KV_RANK), 0.02),
        'mla_w_uq': nrm(ks[13], (DEPTH, MLA_Q_RANK, MLA_HEADS * (MLA_NOPE + MLA_ROPE)), MLA_Q_RANK ** -0.5),
        'mla_w_ukv': nrm(ks[14], (DEPTH, MLA_KV_RANK, MLA_HEADS * (MLA_NOPE + MLA_V)), MLA_KV_RANK ** -0.5),
        'diff_lambda': nrm(ks[15], (DEPTH, 4, DIFF_QK), DIFF_LAMBDA_STD),
        'diff_subln': 1.0 + nrm(ks[16], (DEPTH, DIFF_V), 0.02),
        'w_out': nrm(ks[17], (DEPTH, MIX_WIDTH, D_MODEL), MIX_WIDTH ** -0.5),
        'final_norm': 1.0 + nrm(ks[18], (D_MODEL,), 0.02),
    }


def reference(x, c, ctx, c_ctx, ada_w, ada_b, norm_g, ffn_wg, ffn_wu, ffn_wd, w_in,
              mla_q_norm, mla_kv_norm, mla_w_uq, mla_w_ukv, diff_lambda, diff_subln,
              w_out, final_norm):
    B, n, D = x.shape
    rows = n // GRID_W
    pos = (jnp.repeat(jnp.arange(rows), GRID_W), jnp.tile(jnp.arange(GRID_W), rows))
    xc = ctx
    s_lat = jax.nn.silu(c)
    s_ctx = jax.nn.silu(c_ctx)[None, :]
    for l in range(DEPTH):
        last = l == DEPTH - 1
        lam_init = 0.8 - 0.6 * math.exp(-0.3 * l)
        lf = diff_lambda[l].astype(jnp.float32)
        lam = jnp.exp(jnp.sum(lf[0] * lf[1])) - jnp.exp(jnp.sum(lf[2] * lf[3])) + lam_init
        m = (s_lat @ ada_w[l] + ada_b[l]).reshape(B, N_MOD, 1, D)
        mc = (s_ctx @ ada_w[l] + ada_b[l]).reshape(1, N_MOD, 1, D)

        x = ffn_sublayer(x, m[:, 0], m[:, 1], m[:, 2], norm_g[l, 0], ffn_wg[l, 0], ffn_wu[l, 0], ffn_wd[l, 0])
        xc = ffn_sublayer(xc, mc[:, 0], mc[:, 1], mc[:, 2], norm_g[l, 0], ffn_wg[l, 0], ffn_wu[l, 0], ffn_wd[l, 0])

        h = modulate(x, norm_g[l, 1], m[:, 3], m[:, 4])
        hc = modulate(xc, norm_g[l, 1], mc[:, 3], mc[:, 4])
        p = h @ w_in[l]
        if last:
            pc_kv = hc @ w_in[l][:, :KV_COLS]
        else:
            pc = hc @ w_in[l]
            pc_kv = pc[..., :KV_COLS]
        ctx_kv = kv_side(pc_kv, mla_kv_norm[l], mla_w_ukv[l], None)
        lat_kv = kv_side(p[..., :KV_COLS], mla_kv_norm[l], mla_w_ukv[l], pos)
        full_kv = tuple(jnp.concatenate([a, b], axis=1) for a, b in zip(ctx_kv, lat_kv))
        lat_q = q_side(p[..., KV_COLS:], mla_q_norm[l], mla_w_uq[l], pos)
        x = x + m[:, 5] * token_mix(lat_q, full_kv, lam, lam_init, diff_subln[l], w_out[l])
        if not last:
            ctx_q = q_side(pc[..., KV_COLS:], mla_q_norm[l], mla_w_uq[l], None)
            xc = xc + mc[:, 5] * token_mix(ctx_q, ctx_kv, lam, lam_init, diff_subln[l], w_out[l])

        x = ffn_sublayer(x, m[:, 6], m[:, 7], m[:, 8], norm_g[l, 2], ffn_wg[l, 1], ffn_wu[l, 1], ffn_wd[l, 1])
        if not last:
            xc = ffn_sublayer(xc, mc[:, 6], mc[:, 7], mc[:, 8], norm_g[l, 2], ffn_wg[l, 1], ffn_wu[l, 1], ffn_wd[l, 1])
    return rmsnorm(x, final_norm)
```

```python
import functools
import math

import numpy as np
import jax
import jax.numpy as jnp
from jax import lax
from jax.experimental import pallas as pl
from jax.experimental.pallas import tpu as pltpu

D_MODEL = 2048
BATCH = 8
SEQ = 2048
DEPTH = 2
GRID_W = 64
CTX_LEN = 256
EPS = 1e-6
ROPE_THETA = 10000.0
D_FF = 5632
N_MOD = 9

MLA_HEADS = 8
MLA_Q_RANK = 512
MLA_KV_RANK = 256
MLA_NOPE = 128
MLA_ROPE = 64
MLA_V = 128
MLA_SCALE = (MLA_NOPE + MLA_ROPE) ** -0.5
MLA_QK_PAD = 256

DIFF_HEADS = 4
DIFF_QK = 64
DIFF_V = 2 * DIFF_QK
DIFF_SCALE = DIFF_QK ** -0.5

FOURIER_GROUPS = 4
FOURIER_CH = 128
MLA_WIDTH = MLA_HEADS * MLA_V
DIFF_WIDTH = DIFF_HEADS * DIFF_V
FOURIER_WIDTH = FOURIER_GROUPS * FOURIER_CH

SEG = SEQ
N_SEG = BATCH + 1
ROWS = N_SEG * SEG
MOD_ROWS = 16

P_CKV = 0
P_KROPE = 256
P_DK = 384
P_DV = 896
P_CQ = 1408
P_DQ = 1920
P_U = 2432
P_COLS = 2944

VMEM_LIMIT = 56 * 1024 * 1024

F32 = jnp.float32
BF16 = jnp.bfloat16


def _silu(v):
    return v / (1.0 + jnp.exp(-v))


def _dot(a, b):
    return jnp.dot(a, b, preferred_element_type=F32)


def _dot_nt(a, b):
    return lax.dot_general(a, b, (((1,), (1,)), ((), ())), preferred_element_type=F32)


def _rms(v):
    return v * lax.rsqrt(jnp.mean(v * v, axis=-1, keepdims=True) + EPS)


def _modulated(z, g, mod_ref):
    return _rms(z) * g * (1.0 + mod_ref[1:2, :]) + mod_ref[0:1, :]


def _params(*sem):
    return pltpu.CompilerParams(dimension_semantics=sem, vmem_limit_bytes=VMEM_LIMIT)


def _ada_kernel(s_ref, w_ref, b_ref, o_ref):
    s = _silu(s_ref[...])
    o_ref[...] = _dot(s.astype(BF16), w_ref[...].astype(BF16)) + b_ref[...]


def _ada_table(s_in, ada_w, ada_b):
    tn = 1024
    n_cols = N_MOD * D_MODEL
    return pl.pallas_call(
        _ada_kernel,
        out_shape=jax.ShapeDtypeStruct((DEPTH, MOD_ROWS, n_cols), F32),
        grid=(DEPTH, n_cols // tn),
        in_specs=[
            pl.BlockSpec((MOD_ROWS, D_MODEL), lambda l, n: (0, 0)),
            pl.BlockSpec((None, D_MODEL, tn), lambda l, n: (l, 0, n)),
            pl.BlockSpec((None, 1, tn), lambda l, n: (l, 0, n)),
        ],
        out_specs=pl.BlockSpec((None, MOD_ROWS, tn), lambda l, n: (l, 0, n)),
        compiler_params=_params("parallel", "parallel"),
        name="ada_table",
    )(s_in, ada_w, ada_b.reshape(DEPTH, 1, n_cols))


def _ffn_kernel(z_ref, mod_ref, g_ref, wg_ref, wu_ref, wd_ref, *rest, n_f, final):
    if final:
        fn_ref, o_ref, h_scr, acc_scr = rest
    else:
        o_ref, h_scr, acc_scr = rest
    f = pl.program_id(1)

    @pl.when(f == 0)
    def _():
        h_scr[...] = _modulated(z_ref[...], g_ref[...], mod_ref).astype(BF16)
        acc_scr[...] = jnp.zeros_like(acc_scr)

    h = h_scr[...]
    a = _silu(_dot(h, wg_ref[...])) * _dot(h, wu_ref[...])
    acc_scr[...] += _dot(a.astype(BF16), wd_ref[...])

    @pl.when(f == n_f - 1)
    def _():
        y = z_ref[...] + 0.5 * mod_ref[2:3, :] * acc_scr[...]
        if final:
            y = _rms(y) * fn_ref[...]
        o_ref[...] = y


def _ffn(z, mod, grp, g, wg, wu, wd, *, row_off=0, final_g=None, tm=512, tf=512):
    n_rows = ROWS - row_off
    off = row_off // tm
    n_f = D_FF // tf
    final = final_g is not None
    in_specs = [
        pl.BlockSpec((tm, D_MODEL), lambda i, f: (i + off, 0)),
        pl.BlockSpec((None, None, 3, D_MODEL), lambda i, f: ((i + off) * tm // SEG, grp, 0, 0)),
        pl.BlockSpec((1, D_MODEL), lambda i, f: (0, 0)),
        pl.BlockSpec((D_MODEL, tf), lambda i, f: (0, f)),
        pl.BlockSpec((D_MODEL, tf), lambda i, f: (0, f)),
        pl.BlockSpec((tf, D_MODEL), lambda i, f: (f, 0)),
    ]
    args = [z, mod, g.reshape(1, D_MODEL), wg, wu, wd]
    if final:
        in_specs.append(pl.BlockSpec((1, D_MODEL), lambda i, f: (0, 0)))
        args.append(final_g.reshape(1, D_MODEL))
    return pl.pallas_call(
        functools.partial(_ffn_kernel, n_f=n_f, final=final),
        out_shape=jax.ShapeDtypeStruct((n_rows, D_MODEL), F32),
        grid=(n_rows // tm, n_f),
        in_specs=in_specs,
        out_specs=pl.BlockSpec((tm, D_MODEL), lambda i, f: (i, 0)),
        scratch_shapes=[pltpu.VMEM((tm, D_MODEL), BF16), pltpu.VMEM((tm, D_MODEL), F32)],
        compiler_params=_params("parallel", "arbitrary"),
        name="ffn_final" if final else "ffn",
    )(*args)


def _rope(v, cos, sin, lo_half):
    swapped = jnp.where(lo_half, pltpu.roll(v, 112, 1), pltpu.roll(v, 16, 1))
    return v * cos + swapped * sin


def _proj_kernel(x_ref, mod_ref, g_ref, win_ref, kvn_ref, wukv_ref, qn_ref, wuq_ref, cos_ref, sin_ref,
                 kmla_ref, vmla_ref, qmla_ref, dk_ref, dv_ref, dq_ref, u_ref):
    hb = _modulated(x_ref[...], g_ref[...], mod_ref).astype(BF16)
    cos = cos_ref[...]
    sin = sin_ref[...]
    lane = lax.broadcasted_iota(jnp.int32, (1, 128), 1)
    lo_half = (lane % 32) < 16
    rope = functools.partial(_rope, cos=cos, sin=sin, lo_half=lo_half)

    ckv = _rms(_dot(hb, win_ref[:, P_CKV:P_KROPE])) * kvn_ref[...]
    kv = _dot(ckv.astype(BF16), wukv_ref[...])
    k_rope = rope(_dot(hb, win_ref[:, P_KROPE:P_DK])).astype(BF16)
    for h in range(MLA_HEADS):
        c0 = h * (MLA_NOPE + MLA_V)
        kmla_ref[:, h * MLA_QK_PAD:h * MLA_QK_PAD + MLA_NOPE] = kv[:, c0:c0 + MLA_NOPE].astype(BF16)
        kmla_ref[:, h * MLA_QK_PAD + MLA_NOPE:(h + 1) * MLA_QK_PAD] = k_rope
        vmla_ref[:, h * MLA_V:(h + 1) * MLA_V] = kv[:, c0 + MLA_NOPE:c0 + MLA_NOPE + MLA_V].astype(BF16)
    dk = _dot(hb, win_ref[:, P_DK:P_DV])
    for j in range(DIFF_HEADS):
        dk_ref[:, j * 128:(j + 1) * 128] = rope(dk[:, j * 128:(j + 1) * 128]).astype(BF16)
    dv_ref[...] = _dot(hb, win_ref[:, P_DV:P_CQ]).astype(BF16)

    cq = _rms(_dot(hb, win_ref[:, P_CQ:P_DQ])) * qn_ref[...]
    q = _dot(cq.astype(BF16), wuq_ref[...])
    for h in range(MLA_HEADS):
        c0 = h * MLA_QK_PAD
        qmla_ref[:, c0:c0 + 128] = (q[:, c0:c0 + 128] * MLA_SCALE).astype(BF16)
        qmla_ref[:, c0 + 128:c0 + 256] = (rope(q[:, c0 + 128:c0 + 256]) * MLA_SCALE).astype(BF16)
    dq = _dot(hb, win_ref[:, P_DQ:P_U])
    for j in range(DIFF_HEADS):
        dq_ref[:, j * 128:(j + 1) * 128] = (rope(dq[:, j * 128:(j + 1) * 128]) * DIFF_SCALE).astype(BF16)
    u_ref[...] = _dot(hb, win_ref[:, P_U:P_COLS]).astype(BF16)


def _const_spec(shape):
    return pl.BlockSpec(shape, lambda i: (0,) * len(shape), pipeline_mode=pl.Buffered(1))


def _proj(x, mod, g, w_in, kv_norm, w_ukv, q_norm, w_uq, cos_t, sin_t, *, tr=256):
    per_seg = SEG // tr
    row = lambda w: pl.BlockSpec((tr, w), lambda i: (i, 0))
    tab = pl.BlockSpec((None, tr, 128), lambda i: (jnp.minimum(i // per_seg, 1), i % per_seg, 0))
    widths = (MLA_HEADS * MLA_QK_PAD, MLA_WIDTH, MLA_HEADS * MLA_QK_PAD, 512, 512, 512, 512)
    return pl.pallas_call(
        _proj_kernel,
        out_shape=[jax.ShapeDtypeStruct((ROWS, w), BF16) for w in widths],
        grid=(ROWS // tr,),
        in_specs=[
            row(D_MODEL),
            pl.BlockSpec((None, None, 3, D_MODEL), lambda i: (i // per_seg, 1, 0, 0)),
            _const_spec((1, D_MODEL)),
            _const_spec((D_MODEL, P_COLS)),
            _const_spec((1, MLA_KV_RANK)),
            _const_spec((MLA_KV_RANK, MLA_HEADS * (MLA_NOPE + MLA_V))),
            _const_spec((1, MLA_Q_RANK)),
            _const_spec((MLA_Q_RANK, MLA_HEADS * MLA_QK_PAD)),
            tab,
            tab,
        ],
        out_specs=[row(w) for w in widths],
        compiler_params=_params("parallel"),
        name="in_proj",
    )(x, mod, g.reshape(1, D_MODEL), w_in, kv_norm.reshape(1, -1), w_ukv, q_norm.reshape(1, -1), w_uq,
      cos_t, sin_t)


TQ = 256


def _softmax_parts(s_c, s_l):
    m = jnp.max(s_c, axis=-1, keepdims=True)
    if s_l is not None:
        m = jnp.maximum(m, jnp.max(s_l, axis=-1, keepdims=True))
    e_c = jnp.exp(s_c - m)
    den = jnp.sum(e_c, axis=-1, keepdims=True)
    e_l = None
    if s_l is not None:
        e_l = jnp.exp(s_l - m)
        den = den + jnp.sum(e_l, axis=-1, keepdims=True)
    return e_c, e_l, den


def _mla_kernel(q_ref, kc_ref, kl_ref, vc_ref, vl_ref, o_ref):
    q = q_ref[...]
    s_c = _dot_nt(q, kc_ref[...])

    @pl.when(pl.program_id(2) == 0)
    def _():
        e_c, _, den = _softmax_parts(s_c, None)
        o_ref[...] = (_dot(e_c.astype(BF16), vc_ref[...]) / den).astype(BF16)

    @pl.when(pl.program_id(2) > 0)
    def _():
        e_c, e_l, den = _softmax_parts(s_c, _dot_nt(q, kl_ref[...]))
        o = _dot(e_c.astype(BF16), vc_ref[...]) + _dot(e_l.astype(BF16), vl_ref[...])
        o_ref[...] = (o / den).astype(BF16)


def _attn_specs(width_qk, width_v):
    per_seg = SEG // TQ
    qrow = lambda b, t: jnp.where(t == 0, b, (b + 1) * per_seg + t - 1)
    q_spec = pl.BlockSpec((TQ, width_qk), lambda b, h, t: (qrow(b, t), h))
    kc_spec = pl.BlockSpec((CTX_LEN, width_qk), lambda b, h, t: (b, h))
    kl_spec = pl.BlockSpec((SEG, width_qk), lambda b, h, t: (b + 1, h))
    vc_spec = pl.BlockSpec((CTX_LEN, width_v), lambda b, h, t: (b, h))
    vl_spec = pl.BlockSpec((SEG, width_v), lambda b, h, t: (b + 1, h))
    o_spec = pl.BlockSpec((TQ, width_v), lambda b, h, t: (qrow(b, t), h))
    return q_spec, kc_spec, kl_spec, vc_spec, vl_spec, o_spec


def _mla_attention(qmla, kmla, vmla):
    q_spec, kc_spec, kl_spec, vc_spec, vl_spec, o_spec = _attn_specs(MLA_QK_PAD, MLA_V)
    return pl.pallas_call(
        _mla_kernel,
        out_shape=jax.ShapeDtypeStruct((ROWS, MLA_WIDTH), BF16),
        grid=(BATCH, MLA_HEADS, 1 + SEG // TQ),
        in_specs=[q_spec, kc_spec, kl_spec, vc_spec, vl_spec],
        out_specs=o_spec,
        compiler_params=_params("parallel", "parallel", "arbitrary"),
        name="mla_attention",
    )(qmla, kmla, kmla, vmla, vmla)


def _diff_kernel(lam_ref, sub_ref, q_ref, kc_ref, kl_ref, vc_ref, vl_ref, o_ref, *, lam_init):
    lf = lam_ref[...]
    lam = (jnp.exp(jnp.sum(lf[0:1] * lf[1:2], axis=-1, keepdims=True))
           - jnp.exp(jnp.sum(lf[2:3] * lf[3:4], axis=-1, keepdims=True)) + lam_init)
    q = q_ref[...]
    first = lax.broadcasted_iota(jnp.int32, (1, 2 * DIFF_QK), 1) < DIFF_QK
    q1 = jnp.where(first, q, jnp.zeros_like(q))
    q2 = jnp.where(first, jnp.zeros_like(q), q)
    kc = kc_ref[...]

    def finish(o):
        o_ref[...] = (_rms(o) * sub_ref[...] * (1.0 - lam_init)).astype(BF16)

    @pl.when(pl.program_id(2) == 0)
    def _():
        e1, _, d1 = _softmax_parts(_dot_nt(q1, kc), None)
        e2, _, d2 = _softmax_parts(_dot_nt(q2, kc), None)
        p = e1 / d1 - lam * (e2 / d2)
        finish(_dot(p.astype(BF16), vc_ref[...]))

    @pl.when(pl.program_id(2) > 0)
    def _():
        kl = kl_ref[...]
        e1c, e1l, d1 = _softmax_parts(_dot_nt(q1, kc), _dot_nt(q1, kl))
        e2c, e2l, d2 = _softmax_parts(_dot_nt(q2, kc), _dot_nt(q2, kl))
        r1 = 1.0 / d1
        r2 = lam / d2
        p_c = e1c * r1 - e2c * r2
        p_l = e1l * r1 - e2l * r2
        finish(_dot(p_c.astype(BF16), vc_ref[...]) + _dot(p_l.astype(BF16), vl_ref[...]))


def _diff_attention(dq, dk, dv, diff_lambda, subln, lam_init):
    q_spec, kc_spec, kl_spec, vc_spec, vl_spec, o_spec = _attn_specs(2 * DIFF_QK, DIFF_V)
    return pl.pallas_call(
        functools.partial(_diff_kernel, lam_init=lam_init),
        out_shape=jax.ShapeDtypeStruct((ROWS, DIFF_WIDTH), BF16),
        grid=(BATCH, DIFF_HEADS, 1 + SEG // TQ),
        in_specs=[
            pl.BlockSpec((4, DIFF_QK), lambda b, h, t: (0, 0)),
            pl.BlockSpec((1, DIFF_V), lambda b, h, t: (0, 0)),
            q_spec, kc_spec, kl_spec, vc_spec, vl_spec,
        ],
        out_specs=o_spec,
        compiler_params=_params("parallel", "parallel", "arbitrary"),
        name="diff_attention",
    )(diff_lambda, subln.reshape(1, DIFF_V), dq, dk, dk, dv, dv)


def _dft_tables(n_pos):
    k = np.arange(n_pos, dtype=np.int64)
    ang = 2.0 * np.pi * ((k[:, None] * k[None, :]) % n_pos) / n_pos
    pos = np.concatenate([np.cos(ang), -np.sin(ang)], axis=1) / math.sqrt(n_pos)
    c = np.arange(FOURIER_CH, dtype=np.int64)
    angc = 2.0 * np.pi * ((c[:, None] * c[None, :]) % FOURIER_CH) / FOURIER_CH
    ch = np.concatenate([np.cos(angc), np.sin(angc)], axis=1) / math.sqrt(FOURIER_CH)
    return jnp.asarray(pos, dtype=BF16), jnp.asarray(ch, dtype=BF16)


def _fourier_kernel(u_ref, ch_ref, pos_ref, *rest, n_pos):
    o_ref, rhs_scr = rest[-2], rest[-1]

    @pl.when(pl.program_id(1) == 0)
    def _():
        for g in range(FOURIER_GROUPS):
            a = _dot(u_ref[:, g * FOURIER_CH:(g + 1) * FOURIER_CH], ch_ref[...])
            rhs_scr[0:n_pos, g * FOURIER_CH:(g + 1) * FOURIER_CH] = a[:, :FOURIER_CH].astype(BF16)
            rhs_scr[n_pos:2 * n_pos, g * FOURIER_CH:(g + 1) * FOURIER_CH] = a[:, FOURIER_CH:].astype(BF16)

    o_ref[...] = _dot(pos_ref[...], rhs_scr[...]).astype(BF16)


def _fourier(u, prev=None, *, n_pos, first_block, tr=256):
    pos_t, ch_t = _dft_tables(n_pos)
    n_j = n_pos // tr
    in_specs = [
        pl.BlockSpec((n_pos, FOURIER_WIDTH), lambda b, j: (b + first_block, 0)),
        pl.BlockSpec((FOURIER_CH, 2 * FOURIER_CH), lambda b, j: (0, 0)),
        pl.BlockSpec((tr, 2 * n_pos), lambda b, j: (j, 0)),
    ]
    args = [u, ch_t, pos_t]
    aliases = {}
    if prev is not None:
        in_specs.append(pl.BlockSpec(memory_space=pl.ANY))
        args.append(prev)
        aliases = {3: 0}
    return pl.pallas_call(
        functools.partial(_fourier_kernel, n_pos=n_pos),
        out_shape=jax.ShapeDtypeStruct((ROWS, FOURIER_WIDTH), BF16),
        grid=(BATCH, n_j),
        in_specs=in_specs,
        out_specs=pl.BlockSpec((tr, FOURIER_WIDTH), lambda b, j: ((b + first_block) * n_j + j, 0)),
        scratch_shapes=[pltpu.VMEM((2 * n_pos, FOURIER_WIDTH), BF16)],
        input_output_aliases=aliases,
        compiler_params=_params("parallel", "arbitrary"),
        name="fourier_mix",
    )(*args)


def _out_kernel(x_ref, mod_ref, om_ref, od_ref, of_ref, w_ref, o_ref):
    y = (_dot(om_ref[...], w_ref[0:MLA_WIDTH, :])
         + _dot(od_ref[...], w_ref[MLA_WIDTH:MLA_WIDTH + DIFF_WIDTH, :])
         + _dot(of_ref[...], w_ref[MLA_WIDTH + DIFF_WIDTH:, :]))
    o_ref[...] = x_ref[...] + mod_ref[2:3, :] * y


def _out_proj(x, mod, o_mla, o_diff, o_four, w_out, *, tr=512):
    per_seg = SEG // tr
    row = lambda w: pl.BlockSpec((tr, w), lambda i: (i, 0))
    return pl.pallas_call(
        _out_kernel,
        out_shape=jax.ShapeDtypeStruct((ROWS, D_MODEL), F32),
        grid=(ROWS // tr,),
        in_specs=[
            row(D_MODEL),
            pl.BlockSpec((None, None, 3, D_MODEL), lambda i: (i // per_seg, 1, 0, 0)),
            row(MLA_WIDTH), row(DIFF_WIDTH), row(FOURIER_WIDTH),
            _const_spec((D_MODEL, D_MODEL)),
        ],
        out_specs=row(D_MODEL),
        compiler_params=_params("parallel"),
        name="out_proj",
    )(x, mod, o_mla, o_diff, o_four, w_out)


def _rope_tables():
    rows = SEQ // GRID_W
    pos_r = jnp.repeat(jnp.arange(rows), GRID_W)
    pos_c = jnp.tile(jnp.arange(GRID_W), rows)
    d = MLA_ROPE // 2
    half = d // 2
    inv = ROPE_THETA ** (-2.0 * jnp.arange(half, dtype=F32) / d)

    def tabs(pos):
        ang = pos.astype(F32)[:, None] * inv[None, :]
        return jnp.cos(ang), jnp.sin(ang)

    cr, sr = tabs(pos_r)
    cc, sc = tabs(pos_c)
    cos64 = jnp.concatenate([cr, cr, cc, cc], axis=-1)
    sin64 = jnp.concatenate([-sr, sr, -sc, sc], axis=-1)
    cos_t = jnp.stack([jnp.ones((SEG, 128), F32), jnp.tile(cos64, (1, 2))])
    sin_t = jnp.stack([jnp.zeros((SEG, 128), F32), jnp.tile(sin64, (1, 2))])
    return cos_t, sin_t


def kernel(x, c, ctx, c_ctx, ada_w, ada_b, norm_g, ffn_wg, ffn_wu, ffn_wd, w_in, mla_q_norm, mla_kv_norm,
           mla_w_uq, mla_w_ukv, diff_lambda, diff_subln, w_out, final_norm):
    z = jnp.concatenate([ctx.reshape(BATCH * CTX_LEN, D_MODEL), x.reshape(BATCH * SEQ, D_MODEL)], axis=0)
    s_in = jnp.concatenate([c_ctx[None, :], c, jnp.zeros((MOD_ROWS - 1 - BATCH, D_MODEL), F32)], axis=0)
    wg = ffn_wg.astype(BF16)
    wu = ffn_wu.astype(BF16)
    wd = ffn_wd.astype(BF16)
    w_in_p = jnp.concatenate(
        [w_in[..., :P_KROPE + MLA_ROPE], jnp.zeros((DEPTH, D_MODEL, P_DK - P_KROPE - MLA_ROPE), F32),
         w_in[..., P_KROPE + MLA_ROPE:]], axis=-1).astype(BF16)
    w_uq_p = jnp.pad(mla_w_uq.reshape(DEPTH, MLA_Q_RANK, MLA_HEADS, MLA_NOPE + MLA_ROPE),
                     ((0, 0), (0, 0), (0, 0), (0, MLA_QK_PAD - MLA_NOPE - MLA_ROPE))
                     ).reshape(DEPTH, MLA_Q_RANK, MLA_HEADS * MLA_QK_PAD).astype(BF16)
    w_ukv = mla_w_ukv.astype(BF16)
    w_o = w_out.astype(BF16)
    cos_t, sin_t = _rope_tables()

    mod_all = _ada_table(s_in, ada_w, ada_b).reshape(DEPTH, MOD_ROWS, 3, 3, D_MODEL)

    for l in range(DEPTH):
        last = l == DEPTH - 1
        lam_init = 0.8 - 0.6 * math.exp(-0.3 * l)
        mod = mod_all[l]
        z = _ffn(z, mod, 0, norm_g[l, 0], wg[l, 0], wu[l, 0], wd[l, 0])
        kmla, vmla, qmla, dk, dv, dq, u = _proj(
            z, mod, norm_g[l, 1], w_in_p[l], mla_kv_norm[l], w_ukv[l], mla_q_norm[l], w_uq_p[l], cos_t, sin_t)
        o_mla = _mla_attention(qmla, kmla, vmla)
        o_diff = _diff_attention(dq, dk, dv, diff_lambda[l], diff_subln[l], lam_init)
        o_four = _fourier(u, n_pos=SEQ, first_block=1)
        o_four = _fourier(u, o_four, n_pos=CTX_LEN, first_block=0)
        z = _out_proj(z, mod, o_mla, o_diff, o_four, w_o[l])
        if last:
            z = _ffn(z, mod, 2, norm_g[l, 2], wg[l, 1], wu[l, 1], wd[l, 1], row_off=SEG, final_g=final_norm)
        else:
            z = _ffn(z, mod, 2, norm_g[l, 2], wg[l, 1], wu[l, 1], wd[l, 1])
    return z.reshape(BATCH, SEQ, D_MODEL)
```

```python
import functools
import math

import numpy as np
import jax
import jax.numpy as jnp
from jax import lax
from jax.experimental import pallas as pl
from jax.experimental.pallas import tpu as pltpu

D_MODEL = 2048
BATCH = 8
SEQ = 2048
DEPTH = 2
GRID_W = 64
CTX_LEN = 256
EPS = 1e-6
ROPE_THETA = 10000.0
D_FF = 5632
N_MOD = 9

MLA_HEADS = 8
MLA_Q_RANK = 512
MLA_KV_RANK = 256
MLA_NOPE = 128
MLA_ROPE = 64
MLA_V = 128
MLA_SCALE = (MLA_NOPE + MLA_ROPE) ** -0.5
MLA_QK_PAD = 256
MLA_V_PAD = 256

DIFF_HEADS = 4
DIFF_QK = 64
DIFF_V = 2 * DIFF_QK
DIFF_SCALE = DIFF_QK ** -0.5

FOURIER_GROUPS = 4
FOURIER_CH = 128
MLA_WIDTH = MLA_HEADS * MLA_V
DIFF_WIDTH = DIFF_HEADS * DIFF_V
FOURIER_WIDTH = FOURIER_GROUPS * FOURIER_CH

SEG = SEQ
N_SEG = BATCH + 1
ROWS = N_SEG * SEG
MOD_ROWS = 16

P_CKV = 0
P_KROPE = 256
P_DK = 384
P_DV = 896
P_CQ = 1408
P_DQ = 1920
P_U = 2432
P_COLS = 2944

VMEM_LIMIT = 56 * 1024 * 1024
LOG2_E = math.log2(math.e)

F32 = jnp.float32
BF16 = jnp.bfloat16


def _silu(v):
    return v / (1.0 + jnp.exp(-v))


def _dot(a, b):
    return jnp.dot(a, b, preferred_element_type=F32)


def _dot_nt(a, b):
    return lax.dot_general(a, b, (((1,), (1,)), ((), ())), preferred_element_type=F32)


def _rms(v):
    return v * lax.rsqrt(jnp.mean(v * v, axis=-1, keepdims=True) + EPS)


def _modulated(z, g, mod_ref):
    return _rms(z) * g * (1.0 + mod_ref[1:2, :]) + mod_ref[0:1, :]


def _params(*sem):
    return pltpu.CompilerParams(dimension_semantics=sem, vmem_limit_bytes=VMEM_LIMIT)


def _resident(shape, index_map):
    return pl.BlockSpec(shape, index_map, pipeline_mode=pl.Buffered(1))


def _ada_kernel(s_ref, w_ref, b_ref, o_ref):
    s = _silu(s_ref[...])
    o_ref[...] = _dot(s.astype(BF16), w_ref[...].astype(BF16)) + b_ref[...]


def _ada_table(s_in, ada_w, ada_b):
    tn = 1024
    n_cols = N_MOD * D_MODEL
    return pl.pallas_call(
        _ada_kernel,
        out_shape=jax.ShapeDtypeStruct((DEPTH, MOD_ROWS, n_cols), F32),
        grid=(DEPTH, n_cols // tn),
        in_specs=[
            pl.BlockSpec((MOD_ROWS, D_MODEL), lambda l, n: (0, 0)),
            pl.BlockSpec((None, D_MODEL, tn), lambda l, n: (l, 0, n)),
            pl.BlockSpec((None, 1, tn), lambda l, n: (l, 0, n)),
        ],
        out_specs=pl.BlockSpec((None, MOD_ROWS, tn), lambda l, n: (l, 0, n)),
        compiler_params=_params("parallel", "parallel"),
        name="ada_table",
    )(s_in, ada_w, ada_b.reshape(DEPTH, 1, n_cols))


def _ffn_kernel(z_ref, mod_ref, g_ref, wg_ref, wu_ref, wd_ref, *rest, n_f, final):
    if final:
        fn_ref, o_ref, h_scr, acc_scr = rest
    else:
        o_ref, h_scr, acc_scr = rest
    f = pl.program_id(1)

    @pl.when(f == 0)
    def _():
        h_scr[...] = _modulated(z_ref[...], g_ref[...], mod_ref).astype(BF16)
        acc_scr[...] = jnp.zeros_like(acc_scr)

    h = h_scr[...]
    a = _silu(_dot(h, wg_ref[...])) * _dot(h, wu_ref[...])
    acc_scr[...] += _dot(a.astype(BF16), wd_ref[...])

    @pl.when(f == n_f - 1)
    def _():
        y = z_ref[...] + 0.5 * mod_ref[2:3, :] * acc_scr[...]
        if final:
            y = _rms(y) * fn_ref[...]
        o_ref[...] = y


def _ffn(z, mod, l, k, g, wg, wu, wd, *, row_off=0, seg_base=0, final_g=None, tm=512, tf=512):
    n_rows = z.shape[0] - row_off
    off = row_off // tm
    n_f = D_FF // tf
    final = final_g is not None
    grp = 2 * k
    in_specs = [
        pl.BlockSpec((tm, D_MODEL), lambda i, f: (i + off, 0)),
        pl.BlockSpec((None, None, None, 3, D_MODEL),
                     lambda i, f: (l, seg_base + (i + off) * tm // SEG, grp, 0, 0)),
        pl.BlockSpec((1, D_MODEL), lambda i, f: (0, 0)),
        pl.BlockSpec((None, None, D_MODEL, tf), lambda i, f: (l, k, 0, f)),
        pl.BlockSpec((None, None, D_MODEL, tf), lambda i, f: (l, k, 0, f)),
        pl.BlockSpec((None, None, tf, D_MODEL), lambda i, f: (l, k, f, 0)),
    ]
    args = [z, mod, g.reshape(1, D_MODEL), wg, wu, wd]
    if final:
        in_specs.append(pl.BlockSpec((1, D_MODEL), lambda i, f: (0, 0)))
        args.append(final_g.reshape(1, D_MODEL))
    return pl.pallas_call(
        functools.partial(_ffn_kernel, n_f=n_f, final=final),
        out_shape=jax.ShapeDtypeStruct((n_rows, D_MODEL), F32),
        grid=(n_rows // tm, n_f),
        in_specs=in_specs,
        out_specs=pl.BlockSpec((tm, D_MODEL), lambda i, f: (i, 0)),
        scratch_shapes=[pltpu.VMEM((tm, D_MODEL), BF16), pltpu.VMEM((tm, D_MODEL), F32)],
        compiler_params=_params("parallel", "arbitrary"),
        name="ffn_final" if final else "ffn",
    )(*args)


def _rope(v, cos, sin, lo_half):
    swapped = jnp.where(lo_half, pltpu.roll(v, 112, 1), pltpu.roll(v, 16, 1))
    return v * cos + swapped * sin


def _proj_kernel(x_ref, mod_ref, g_ref, win_ref, kvn_ref, wukv_ref, qn_ref, wuq_ref, cos_ref, sin_ref,
                 kmla_ref, vmla_ref, qmla_ref, dk_ref, dv_ref, dq_ref, u_ref):
    hb = _modulated(x_ref[...], g_ref[...], mod_ref).astype(BF16)
    cos = cos_ref[...]
    sin = sin_ref[...]
    lane = lax.broadcasted_iota(jnp.int32, (1, 128), 1)
    lo_half = (lane % 32) < 16
    rope = functools.partial(_rope, cos=cos, sin=sin, lo_half=lo_half)
    q_scale = MLA_SCALE * LOG2_E
    dq_scale = DIFF_SCALE * LOG2_E

    ckv = _rms(_dot(hb, win_ref[:, P_CKV:P_KROPE])) * kvn_ref[...]
    kv = _dot(ckv.astype(BF16), wukv_ref[...])
    k_rope = rope(_dot(hb, win_ref[:, P_KROPE:P_DK])).astype(BF16)
    for h in range(MLA_HEADS):
        c0 = h * (MLA_NOPE + MLA_V)
        kmla_ref[:, h * MLA_QK_PAD:h * MLA_QK_PAD + MLA_NOPE] = kv[:, c0:c0 + MLA_NOPE].astype(BF16)
        kmla_ref[:, h * MLA_QK_PAD + MLA_NOPE:(h + 1) * MLA_QK_PAD] = k_rope
        vmla_ref[:, h * MLA_V_PAD:h * MLA_V_PAD + MLA_V] = kv[:, c0 + MLA_NOPE:c0 + MLA_NOPE + MLA_V].astype(BF16)
        vmla_ref[:, h * MLA_V_PAD + MLA_V:(h + 1) * MLA_V_PAD] = jnp.ones((kv.shape[0], MLA_V_PAD - MLA_V), BF16)
    dk = _dot(hb, win_ref[:, P_DK:P_DV])
    for j in range(DIFF_HEADS):
        dk_ref[:, j * 128:(j + 1) * 128] = rope(dk[:, j * 128:(j + 1) * 128]).astype(BF16)
    dv_ref[...] = _dot(hb, win_ref[:, P_DV:P_CQ]).astype(BF16)

    cq = _rms(_dot(hb, win_ref[:, P_CQ:P_DQ])) * qn_ref[...]
    q = _dot(cq.astype(BF16), wuq_ref[...])
    for h in range(MLA_HEADS):
        c0 = h * MLA_QK_PAD
        qmla_ref[:, c0:c0 + 128] = (q[:, c0:c0 + 128] * q_scale).astype(BF16)
        qmla_ref[:, c0 + 128:c0 + 256] = (rope(q[:, c0 + 128:c0 + 256]) * q_scale).astype(BF16)
    dq = _dot(hb, win_ref[:, P_DQ:P_U])
    for j in range(DIFF_HEADS):
        dq_ref[:, j * 128:(j + 1) * 128] = (rope(dq[:, j * 128:(j + 1) * 128]) * dq_scale).astype(BF16)
    u_ref[...] = _dot(hb, win_ref[:, P_U:P_COLS]).astype(BF16)


def _proj(x, mod, l, g, w_in, kv_norm, w_ukv, q_norm, w_uq, cos_t, sin_t, *, tr=512):
    per_seg = SEG // tr
    row = lambda w: pl.BlockSpec((tr, w), lambda i: (i, 0))
    tab = pl.BlockSpec((None, tr, 128), lambda i: (jnp.minimum(i // per_seg, 1), i % per_seg, 0))
    vec = lambda w: _resident((None, 1, w), lambda i: (l, 0, 0))
    mat = lambda r, c: _resident((None, r, c), lambda i: (l, 0, 0))
    widths = (MLA_HEADS * MLA_QK_PAD, MLA_HEADS * MLA_V_PAD, MLA_HEADS * MLA_QK_PAD, 512, 512, 512, 512)
    return pl.pallas_call(
        _proj_kernel,
        out_shape=[jax.ShapeDtypeStruct((ROWS, w), BF16) for w in widths],
        grid=(ROWS // tr,),
        in_specs=[
            row(D_MODEL),
            pl.BlockSpec((None, None, None, 3, D_MODEL), lambda i: (l, i // per_seg, 1, 0, 0)),
            _resident((1, D_MODEL), lambda i: (0, 0)),
            mat(D_MODEL, P_COLS),
            vec(MLA_KV_RANK),
            mat(MLA_KV_RANK, MLA_HEADS * (MLA_NOPE + MLA_V)),
            vec(MLA_Q_RANK),
            mat(MLA_Q_RANK, MLA_HEADS * MLA_QK_PAD),
            tab,
            tab,
        ],
        out_specs=[row(w) for w in widths],
        compiler_params=_params("parallel"),
        name="in_proj",
    )(x, mod, g.reshape(1, D_MODEL), w_in, kv_norm.reshape(DEPTH, 1, -1), w_ukv,
      q_norm.reshape(DEPTH, 1, -1), w_uq, cos_t, sin_t)


SUB = 256


def _softmax_numerators(s_c, s_l):
    m = jnp.max(s_c, axis=-1, keepdims=True)
    if s_l is not None:
        m = jnp.maximum(m, jnp.max(s_l, axis=-1, keepdims=True))
    return jnp.exp2(s_c - m), (jnp.exp2(s_l - m) if s_l is not None else None)


def _softmax_parts(s_c, s_l):
    e_c, e_l = _softmax_numerators(s_c, s_l)
    den = jnp.sum(e_c, axis=-1, keepdims=True)
    if e_l is not None:
        den = den + jnp.sum(e_l, axis=-1, keepdims=True)
    return e_c, e_l, den


def _attn_refs(refs, lat):
    if lat:
        return refs
    q_ref, kc_ref, vc_ref, _, o_ref = refs
    return q_ref, kc_ref, vc_ref, None, None, o_ref


def _one_ahead(items, scores, finish):
    nxt = scores(*items[0])
    for i, item in enumerate(items):
        cur, nxt = nxt, (scores(*items[i + 1]) if i + 1 < len(items) else None)
        finish(*item, cur)


def _mla_kernel(*refs, heads, n_sub, lat):
    q_ref, kc_ref, vc_ref, kl_ref, vl_ref, o_ref = _attn_refs(refs, lat)

    def scores(hh, j):
        qk = slice(hh * MLA_QK_PAD, (hh + 1) * MLA_QK_PAD)
        q = q_ref[j * SUB:(j + 1) * SUB, qk]
        return _dot_nt(q, kc_ref[:, qk]), (_dot_nt(q, kl_ref[:, qk]) if lat else None)

    def finish(hh, j, s):
        vv = slice(hh * MLA_V_PAD, (hh + 1) * MLA_V_PAD)
        e_c, e_l = _softmax_numerators(*s)
        o = _dot(e_c.astype(BF16), vc_ref[:, vv])
        if lat:
            o = o + _dot(e_l.astype(BF16), vl_ref[:, vv])
        o_ref[j * SUB:(j + 1) * SUB, hh * MLA_V:(hh + 1) * MLA_V] = (o[:, :MLA_V] / o[:, MLA_V:]).astype(BF16)

    _one_ahead([(hh, j) for hh in range(heads) for j in range(n_sub)], scores, finish)


def _diff_kernel(lam_ref, sub_ref, *refs, heads, n_sub, lat, lam_init):
    q_ref, kc_ref, vc_ref, kl_ref, vl_ref, o_ref = _attn_refs(refs, lat)
    lf = lam_ref[...]
    lam = (jnp.exp(jnp.sum(lf[0:1] * lf[1:2], axis=-1, keepdims=True))
           - jnp.exp(jnp.sum(lf[2:3] * lf[3:4], axis=-1, keepdims=True)) + lam_init)
    first = lax.broadcasted_iota(jnp.int32, (1, 2 * DIFF_QK), 1) < DIFF_QK

    def scores(hh, j):
        cols = slice(hh * DIFF_V, (hh + 1) * DIFF_V)
        q = q_ref[j * SUB:(j + 1) * SUB, cols]
        q1 = jnp.where(first, q, jnp.zeros_like(q))
        q2 = jnp.where(first, jnp.zeros_like(q), q)
        kc = kc_ref[:, cols]
        if not lat:
            return _dot_nt(q1, kc), None, _dot_nt(q2, kc), None
        kl = kl_ref[:, cols]
        return _dot_nt(q1, kc), _dot_nt(q1, kl), _dot_nt(q2, kc), _dot_nt(q2, kl)

    def finish(hh, j, s):
        cols = slice(hh * DIFF_V, (hh + 1) * DIFF_V)
        e1c, e1l, d1 = _softmax_parts(s[0], s[1])
        e2c, e2l, d2 = _softmax_parts(s[2], s[3])
        ratio = lam * d1 / d2
        o = _dot((e1c - e2c * ratio).astype(BF16), vc_ref[:, cols])
        if lat:
            o = o + _dot((e1l - e2l * ratio).astype(BF16), vl_ref[:, cols])
        o = o / d1
        o_ref[j * SUB:(j + 1) * SUB, cols] = (_rms(o) * sub_ref[...] * (1.0 - lam_init)).astype(BF16)

    _one_ahead([(hh, j) for hh in range(heads) for j in range(n_sub)], scores, finish)


def _attention(kernel_fn, name, q, k, v, prev, *, n_heads, w_qk, w_v, w_o, tq, extra_args=(), extra_specs=()):
    n_extra = len(extra_args)
    if prev is None:
        per_seg = SEG // tq
        in_specs = list(extra_specs) + [
            pl.BlockSpec((tq, w_qk), lambda b, h, t: ((b + 1) * per_seg + t, h)),
            pl.BlockSpec((CTX_LEN, w_qk), lambda b, h, t: (b, h)),
            pl.BlockSpec((CTX_LEN, w_v), lambda b, h, t: (b, h)),
            pl.BlockSpec((SEG, w_qk), lambda b, h, t: (b + 1, h)),
            pl.BlockSpec((SEG, w_v), lambda b, h, t: (b + 1, h)),
        ]
        return pl.pallas_call(
            functools.partial(kernel_fn, heads=1, n_sub=tq // SUB, lat=True),
            out_shape=jax.ShapeDtypeStruct((ROWS, n_heads * w_o), BF16),
            grid=(BATCH, n_heads, per_seg),
            in_specs=in_specs,
            out_specs=pl.BlockSpec((tq, w_o), lambda b, h, t: ((b + 1) * per_seg + t, h)),
            compiler_params=_params("parallel", "parallel", "arbitrary"),
            name=name,
        )(*extra_args, q, k, v, k, v)
    in_specs = list(extra_specs) + [
        pl.BlockSpec((CTX_LEN, n_heads * w_qk), lambda b: (b, 0)),
        pl.BlockSpec((CTX_LEN, n_heads * w_qk), lambda b: (b, 0)),
        pl.BlockSpec((CTX_LEN, n_heads * w_v), lambda b: (b, 0)),
        pl.BlockSpec(memory_space=pl.ANY),
    ]
    return pl.pallas_call(
        functools.partial(kernel_fn, heads=n_heads, n_sub=CTX_LEN // SUB, lat=False),
        out_shape=jax.ShapeDtypeStruct((ROWS, n_heads * w_o), BF16),
        grid=(BATCH,),
        in_specs=in_specs,
        out_specs=pl.BlockSpec((CTX_LEN, n_heads * w_o), lambda b: (b, 0)),
        input_output_aliases={n_extra + 3: 0},
        compiler_params=_params("parallel"),
        name=name + "_ctx",
    )(*extra_args, q, k, v, prev)


def _mla_attention(qmla, kmla, vmla, prev=None, *, tq=1024):
    return _attention(_mla_kernel, "mla_attention", qmla, kmla, vmla, prev,
                      n_heads=MLA_HEADS, w_qk=MLA_QK_PAD, w_v=MLA_V_PAD, w_o=MLA_V, tq=tq)


def _diff_attention(dq, dk, dv, diff_lambda, subln, l, lam_init, prev=None, *, tq=1024):
    layer = lambda *_: (l, 0, 0)
    extra_specs = (pl.BlockSpec((None, 4, DIFF_QK), layer), pl.BlockSpec((None, 1, DIFF_V), layer))
    return _attention(functools.partial(_diff_kernel, lam_init=lam_init), "diff_attention", dq, dk, dv, prev,
                      n_heads=DIFF_HEADS, w_qk=2 * DIFF_QK, w_v=DIFF_V, w_o=DIFF_V, tq=tq,
                      extra_args=(diff_lambda, subln.reshape(DEPTH, 1, DIFF_V)), extra_specs=extra_specs)


def _dft_tables(n_pos):
    k = np.arange(n_pos, dtype=np.int64)
    ang = 2.0 * np.pi * ((k[:, None] * k[None, :]) % n_pos) / n_pos
    pos = np.concatenate([np.cos(ang), -np.sin(ang)], axis=1) / math.sqrt(n_pos)
    c = np.arange(FOURIER_CH, dtype=np.int64)
    angc = 2.0 * np.pi * ((c[:, None] * c[None, :]) % FOURIER_CH) / FOURIER_CH
    ch = np.concatenate([np.cos(angc), np.sin(angc)], axis=1) / math.sqrt(FOURIER_CH)
    return jnp.asarray(pos, dtype=BF16), jnp.asarray(ch, dtype=BF16)


def _fourier_kernel(u_ref, ch_ref, pos_ref, *rest, n_pos):
    o_ref, rhs_scr = rest[-2], rest[-1]

    @pl.when(pl.program_id(1) == 0)
    def _():
        for g in range(FOURIER_GROUPS):
            a = _dot(u_ref[:, g * FOURIER_CH:(g + 1) * FOURIER_CH], ch_ref[...])
            rhs_scr[0:n_pos, g * FOURIER_CH:(g + 1) * FOURIER_CH] = a[:, :FOURIER_CH].astype(BF16)
            rhs_scr[n_pos:2 * n_pos, g * FOURIER_CH:(g + 1) * FOURIER_CH] = a[:, FOURIER_CH:].astype(BF16)

    o_ref[...] = _dot(pos_ref[...], rhs_scr[...]).astype(BF16)


def _fourier(u, prev=None, *, n_pos, first_block, tr=256):
    pos_t, ch_t = _dft_tables(n_pos)
    n_j = n_pos // tr
    in_specs = [
        pl.BlockSpec((n_pos, FOURIER_WIDTH), lambda b, j: (b + first_block, 0)),
        pl.BlockSpec((FOURIER_CH, 2 * FOURIER_CH), lambda b, j: (0, 0)),
        pl.BlockSpec((tr, 2 * n_pos), lambda b, j: (j, 0)),
    ]
    args = [u, ch_t, pos_t]
    aliases = {}
    if prev is not None:
        in_specs.append(pl.BlockSpec(memory_space=pl.ANY))
        args.append(prev)
        aliases = {3: 0}
    return pl.pallas_call(
        functools.partial(_fourier_kernel, n_pos=n_pos),
        out_shape=jax.ShapeDtypeStruct((ROWS, FOURIER_WIDTH), BF16),
        grid=(BATCH, n_j),
        in_specs=in_specs,
        out_specs=pl.BlockSpec((tr, FOURIER_WIDTH), lambda b, j: ((b + first_block) * n_j + j, 0)),
        scratch_shapes=[pltpu.VMEM((2 * n_pos, FOURIER_WIDTH), BF16)],
        input_output_aliases=aliases,
        compiler_params=_params("parallel", "arbitrary"),
        name="fourier_mix",
    )(*args)


def _out_kernel(x_ref, mod_ref, om_ref, od_ref, of_ref, w_ref, o_ref):
    y = (_dot(om_ref[...], w_ref[0:MLA_WIDTH, :])
         + _dot(od_ref[...], w_ref[MLA_WIDTH:MLA_WIDTH + DIFF_WIDTH, :])
         + _dot(of_ref[...], w_ref[MLA_WIDTH + DIFF_WIDTH:, :]))
    o_ref[...] = x_ref[...] + mod_ref[2:3, :] * y


def _out_proj(x, mod, l, o_mla, o_diff, o_four, w_out, *, row_off=0, tr=512):
    per_seg = SEG // tr
    off = row_off // tr
    row = lambda w: pl.BlockSpec((tr, w), lambda i: (i + off, 0))
    return pl.pallas_call(
        _out_kernel,
        out_shape=jax.ShapeDtypeStruct((ROWS - row_off, D_MODEL), F32),
        grid=((ROWS - row_off) // tr,),
        in_specs=[
            row(D_MODEL),
            pl.BlockSpec((None, None, None, 3, D_MODEL), lambda i: (l, (i + off) // per_seg, 1, 0, 0)),
            row(MLA_WIDTH), row(DIFF_WIDTH), row(FOURIER_WIDTH),
            _resident((None, D_MODEL, D_MODEL), lambda i: (l, 0, 0)),
        ],
        out_specs=pl.BlockSpec((tr, D_MODEL), lambda i: (i, 0)),
        compiler_params=_params("parallel"),
        name="out_proj",
    )(x, mod, o_mla, o_diff, o_four, w_out)


def _rope_tables():
    rows = SEQ // GRID_W
    pos_r = jnp.repeat(jnp.arange(rows), GRID_W)
    pos_c = jnp.tile(jnp.arange(GRID_W), rows)
    d = MLA_ROPE // 2
    half = d // 2
    inv = ROPE_THETA ** (-2.0 * jnp.arange(half, dtype=F32) / d)

    def tabs(pos):
        ang = pos.astype(F32)[:, None] * inv[None, :]
        return jnp.cos(ang), jnp.sin(ang)

    cr, sr = tabs(pos_r)
    cc, sc = tabs(pos_c)
    cos64 = jnp.concatenate([cr, cr, cc, cc], axis=-1)
    sin64 = jnp.concatenate([-sr, sr, -sc, sc], axis=-1)
    cos_t = jnp.stack([jnp.ones((SEG, 128), F32), jnp.tile(cos64, (1, 2))])
    sin_t = jnp.stack([jnp.zeros((SEG, 128), F32), jnp.tile(sin64, (1, 2))])
    return cos_t, sin_t


def kernel(x, c, ctx, c_ctx, ada_w, ada_b, norm_g, ffn_wg, ffn_wu, ffn_wd, w_in, mla_q_norm, mla_kv_norm,
           mla_w_uq, mla_w_ukv, diff_lambda, diff_subln, w_out, final_norm):
    z = jnp.concatenate([ctx.reshape(BATCH * CTX_LEN, D_MODEL), x.reshape(BATCH * SEQ, D_MODEL)], axis=0)
    s_in = jnp.concatenate([c_ctx[None, :], c, jnp.zeros((MOD_ROWS - 1 - BATCH, D_MODEL), F32)], axis=0)
    wg = ffn_wg.astype(BF16)
    wu = ffn_wu.astype(BF16)
    wd = ffn_wd.astype(BF16)
    w_in_p = jnp.concatenate(
        [w_in[..., :P_KROPE + MLA_ROPE].astype(BF16),
         jnp.zeros((DEPTH, D_MODEL, P_DK - P_KROPE - MLA_ROPE), BF16),
         w_in[..., P_KROPE + MLA_ROPE:].astype(BF16)], axis=-1)
    w_uq_p = jnp.pad(mla_w_uq.astype(BF16).reshape(DEPTH, MLA_Q_RANK, MLA_HEADS, MLA_NOPE + MLA_ROPE),
                     ((0, 0), (0, 0), (0, 0), (0, MLA_QK_PAD - MLA_NOPE - MLA_ROPE))
                     ).reshape(DEPTH, MLA_Q_RANK, MLA_HEADS * MLA_QK_PAD)
    w_ukv = mla_w_ukv.astype(BF16)
    w_o = w_out.astype(BF16)
    cos_t, sin_t = _rope_tables()

    mod = _ada_table(s_in, ada_w, ada_b).reshape(DEPTH, MOD_ROWS, 3, 3, D_MODEL)

    for l in range(DEPTH):
        last = l == DEPTH - 1
        lam_init = 0.8 - 0.6 * math.exp(-0.3 * l)
        z = _ffn(z, mod, l, 0, norm_g[l, 0], wg, wu, wd)
        kmla, vmla, qmla, dk, dv, dq, u = _proj(
            z, mod, l, norm_g[l, 1], w_in_p, mla_kv_norm, w_ukv, mla_q_norm, w_uq_p, cos_t, sin_t)
        o_mla = _mla_attention(qmla, kmla, vmla)
        o_diff = _diff_attention(dq, dk, dv, diff_lambda, diff_subln, l, lam_init)
        o_four = _fourier(u, n_pos=SEQ, first_block=1)
        if last:
            z = _out_proj(z, mod, l, o_mla, o_diff, o_four, w_o, row_off=SEG)
            z = _ffn(z, mod, l, 1, norm_g[l, 2], wg, wu, wd, seg_base=1, final_g=final_norm)
        else:
            o_mla = _mla_attention(qmla, kmla, vmla, o_mla)
            o_diff = _diff_attention(dq, dk, dv, diff_lambda, diff_subln, l, lam_init, o_diff)
            o_four = _fourier(u, o_four, n_pos=CTX_LEN, first_block=0)
            z = _out_proj(z, mod, l, o_mla, o_diff, o_four, w_o)
            z = _ffn(z, mod, l, 1, norm_g[l, 2], wg, wu, wd)
    return z.reshape(BATCH, SEQ, D_MODEL)
```

```python
import functools
import math

import numpy as np
import jax
import jax.numpy as jnp
from jax import lax
from jax.experimental import pallas as pl
from jax.experimental.pallas import tpu as pltpu

D_MODEL = 2048
BATCH = 8
SEQ = 2048
DEPTH = 2
GRID_W = 64
CTX_LEN = 256
EPS = 1e-6
ROPE_THETA = 10000.0
D_FF = 5632
N_MOD = 9

MLA_HEADS = 8
MLA_Q_RANK = 512
MLA_KV_RANK = 256
MLA_NOPE = 128
MLA_ROPE = 64
MLA_V = 128
MLA_SCALE = (MLA_NOPE + MLA_ROPE) ** -0.5
MLA_QK_PAD = 256
MLA_V_PAD = 256

DIFF_HEADS = 4
DIFF_QK = 64
DIFF_V = 2 * DIFF_QK
DIFF_SCALE = DIFF_QK ** -0.5

FOURIER_GROUPS = 4
FOURIER_CH = 128
MLA_WIDTH = MLA_HEADS * MLA_V
DIFF_WIDTH = DIFF_HEADS * DIFF_V
FOURIER_WIDTH = FOURIER_GROUPS * FOURIER_CH

SEG = SEQ
N_SEG = BATCH + 1
ROWS = N_SEG * SEG
MOD_ROWS = 16

P_CKV = 0
P_KROPE = 256
P_DK = 384
P_DV = 896
P_CQ = 1408
P_DQ = 1920
P_U = 2432
P_COLS = 2944

VMEM_LIMIT = 56 * 1024 * 1024
VMEM_LIMIT_FFN = 63 * 1024 * 1024 + 512 * 1024
LOG2_E = math.log2(math.e)

F32 = jnp.float32
BF16 = jnp.bfloat16


def _silu(v):
    return v / (1.0 + jnp.exp(-v))


def _dot(a, b):
    return jnp.dot(a, b, preferred_element_type=F32)


def _dot_nt(a, b):
    return lax.dot_general(a, b, (((1,), (1,)), ((), ())), preferred_element_type=F32)


def _rms(v):
    return v * lax.rsqrt(jnp.mean(v * v, axis=-1, keepdims=True) + EPS)


def _modulated(z, g, mod_ref):
    return _rms(z) * g * (1.0 + mod_ref[1:2, :]) + mod_ref[0:1, :]


def _params(*sem, vmem=VMEM_LIMIT):
    return pltpu.CompilerParams(dimension_semantics=sem, vmem_limit_bytes=vmem)


def _resident(shape, index_map):
    return pl.BlockSpec(shape, index_map, pipeline_mode=pl.Buffered(1))


def _ada_kernel(s_ref, w_ref, b_ref, o_ref):
    s = _silu(s_ref[...])
    o_ref[...] = _dot(s.astype(BF16), w_ref[...].astype(BF16)) + b_ref[...]


def _ada_table(s_in, ada_w, ada_b):
    tn = 1024
    n_cols = N_MOD * D_MODEL
    return pl.pallas_call(
        _ada_kernel,
        out_shape=jax.ShapeDtypeStruct((DEPTH, MOD_ROWS, n_cols), F32),
        grid=(DEPTH, n_cols // tn),
        in_specs=[
            pl.BlockSpec((MOD_ROWS, D_MODEL), lambda l, n: (0, 0)),
            pl.BlockSpec((None, D_MODEL, tn), lambda l, n: (l, 0, n)),
            pl.BlockSpec((None, 1, tn), lambda l, n: (l, 0, n)),
        ],
        out_specs=pl.BlockSpec((None, MOD_ROWS, tn), lambda l, n: (l, 0, n)),
        compiler_params=_params("parallel", "parallel"),
        name="ada_table",
    )(s_in, ada_w, ada_b.reshape(DEPTH, 1, n_cols))


DOWN_COLS = 512


def _ffn_kernel(z_ref, mod_ref, g_ref, wg_ref, wu_ref, wd_ref, *rest, n_f, final, aliased):
    rest = list(rest)
    fn_ref = rest.pop(0) if final else None
    if aliased:
        rest.pop(0)
    o_ref, h_scr = rest
    f = pl.program_id(1)

    @pl.when(f == 0)
    def _():
        h_scr[...] = _modulated(z_ref[...], g_ref[...], mod_ref).astype(BF16)
        o_ref[...] = jnp.zeros_like(o_ref)

    h = h_scr[...]
    a = (_silu(_dot(h, wg_ref[...])) * _dot(h, wu_ref[...])).astype(BF16)
    for c in range(0, D_MODEL, DOWN_COLS):
        o_ref[:, c:c + DOWN_COLS] += _dot(a, wd_ref[:, c:c + DOWN_COLS])

    @pl.when(f == n_f - 1)
    def _():
        y = z_ref[...] + 0.5 * mod_ref[2:3, :] * o_ref[...]
        if final:
            y = _rms(y) * fn_ref[...]
        o_ref[...] = y


def _ffn(z, mod, l, k, g, wg, wu, wd, *, row_off=0, seg_base=0, out_rows=None, out_off=0, prev=None,
         final_g=None, tm=1024, tf=512):
    n_rows = z.shape[0] - row_off
    if prev is not None:
        out_rows = prev.shape[0]
    elif out_rows is None:
        out_rows = n_rows
    off = row_off // tm
    o_off = out_off // tm
    n_f = D_FF // tf
    final = final_g is not None
    grp = 2 * k
    in_specs = [
        pl.BlockSpec((tm, D_MODEL), lambda i, f: (i + off, 0)),
        pl.BlockSpec((None, None, None, 3, D_MODEL),
                     lambda i, f: (l, seg_base + (i + off) * tm // SEG, grp, 0, 0)),
        pl.BlockSpec((1, D_MODEL), lambda i, f: (0, 0)),
        pl.BlockSpec((None, None, D_MODEL, tf), lambda i, f: (l, k, 0, f)),
        pl.BlockSpec((None, None, D_MODEL, tf), lambda i, f: (l, k, 0, f)),
        pl.BlockSpec((None, None, tf, D_MODEL), lambda i, f: (l, k, f, 0)),
    ]
    args = [z, mod, g.reshape(1, D_MODEL), wg, wu, wd]
    if final:
        in_specs.append(pl.BlockSpec((1, D_MODEL), lambda i, f: (0, 0)))
        args.append(final_g.reshape(1, D_MODEL))
    aliases = {}
    if prev is not None:
        in_specs.append(pl.BlockSpec(memory_space=pl.ANY))
        args.append(prev)
        aliases = {len(args) - 1: 0}
    return pl.pallas_call(
        functools.partial(_ffn_kernel, n_f=n_f, final=final, aliased=prev is not None),
        out_shape=jax.ShapeDtypeStruct((out_rows, D_MODEL), F32),
        grid=(n_rows // tm, n_f),
        in_specs=in_specs,
        out_specs=pl.BlockSpec((tm, D_MODEL), lambda i, f: (i + o_off, 0)),
        scratch_shapes=[pltpu.VMEM((tm, D_MODEL), BF16)],
        input_output_aliases=aliases,
        compiler_params=_params("parallel", "arbitrary", vmem=VMEM_LIMIT_FFN),
        name="ffn_final" if final else "ffn",
    )(*args)


def _rope(v, cos, sin, lo_half):
    swapped = jnp.where(lo_half, pltpu.roll(v, 112, 1), pltpu.roll(v, 16, 1))
    return v * cos + swapped * sin


def _proj_kernel(x_ref, mod_ref, g_ref, win_ref, kvn_ref, wukv_ref, qn_ref, wuq_ref, cos_ref, sin_ref,
                 kmla_ref, vmla_ref, qmla_ref, dk_ref, dv_ref, dq_ref, u_ref):
    hb = _modulated(x_ref[...], g_ref[...], mod_ref).astype(BF16)
    cos = cos_ref[...]
    sin = sin_ref[...]
    lane = lax.broadcasted_iota(jnp.int32, (1, 128), 1)
    lo_half = (lane % 32) < 16
    rope = functools.partial(_rope, cos=cos, sin=sin, lo_half=lo_half)
    q_scale = MLA_SCALE * LOG2_E
    dq_scale = DIFF_SCALE * LOG2_E

    ckv = _rms(_dot(hb, win_ref[:, P_CKV:P_KROPE])) * kvn_ref[...]
    kv = _dot(ckv.astype(BF16), wukv_ref[...])
    k_rope = rope(_dot(hb, win_ref[:, P_KROPE:P_DK])).astype(BF16)
    for h in range(MLA_HEADS):
        c0 = h * (MLA_NOPE + MLA_V)
        kmla_ref[:, h * MLA_QK_PAD:h * MLA_QK_PAD + MLA_NOPE] = kv[:, c0:c0 + MLA_NOPE].astype(BF16)
        kmla_ref[:, h * MLA_QK_PAD + MLA_NOPE:(h + 1) * MLA_QK_PAD] = k_rope
        vmla_ref[:, h * MLA_V_PAD:h * MLA_V_PAD + MLA_V] = kv[:, c0 + MLA_NOPE:c0 + MLA_NOPE + MLA_V].astype(BF16)
        vmla_ref[:, h * MLA_V_PAD + MLA_V:(h + 1) * MLA_V_PAD] = jnp.ones((kv.shape[0], MLA_V_PAD - MLA_V), BF16)
    dk = _dot(hb, win_ref[:, P_DK:P_DV])
    for j in range(DIFF_HEADS):
        dk_ref[:, j * 128:(j + 1) * 128] = rope(dk[:, j * 128:(j + 1) * 128]).astype(BF16)
    dv_ref[...] = _dot(hb, win_ref[:, P_DV:P_CQ]).astype(BF16)

    cq = _rms(_dot(hb, win_ref[:, P_CQ:P_DQ])) * qn_ref[...]
    q = _dot(cq.astype(BF16), wuq_ref[...])
    for h in range(MLA_HEADS):
        c0 = h * MLA_QK_PAD
        qmla_ref[:, c0:c0 + 128] = (q[:, c0:c0 + 128] * q_scale).astype(BF16)
        qmla_ref[:, c0 + 128:c0 + 256] = (rope(q[:, c0 + 128:c0 + 256]) * q_scale).astype(BF16)
    dq = _dot(hb, win_ref[:, P_DQ:P_U])
    for j in range(DIFF_HEADS):
        dq_ref[:, j * 128:(j + 1) * 128] = (rope(dq[:, j * 128:(j + 1) * 128]) * dq_scale).astype(BF16)
    u_ref[...] = _dot(hb, win_ref[:, P_U:P_COLS]).astype(BF16)


def _proj(x, mod, l, g, w_in, kv_norm, w_ukv, q_norm, w_uq, cos_t, sin_t, *, tr=512):
    per_seg = SEG // tr
    row = lambda w: pl.BlockSpec((tr, w), lambda i: (i, 0))
    tab = pl.BlockSpec((None, tr, 128), lambda i: (jnp.minimum(i // per_seg, 1), i % per_seg, 0))
    vec = lambda w: _resident((None, 1, w), lambda i: (l, 0, 0))
    mat = lambda r, c: _resident((None, r, c), lambda i: (l, 0, 0))
    widths = (MLA_HEADS * MLA_QK_PAD, MLA_HEADS * MLA_V_PAD, MLA_HEADS * MLA_QK_PAD, 512, 512, 512, 512)
    return pl.pallas_call(
        _proj_kernel,
        out_shape=[jax.ShapeDtypeStruct((ROWS, w), BF16) for w in widths],
        grid=(ROWS // tr,),
        in_specs=[
            row(D_MODEL),
            pl.BlockSpec((None, None, None, 3, D_MODEL), lambda i: (l, i // per_seg, 1, 0, 0)),
            _resident((1, D_MODEL), lambda i: (0, 0)),
            mat(D_MODEL, P_COLS),
            vec(MLA_KV_RANK),
            mat(MLA_KV_RANK, MLA_HEADS * (MLA_NOPE + MLA_V)),
            vec(MLA_Q_RANK),
            mat(MLA_Q_RANK, MLA_HEADS * MLA_QK_PAD),
            tab,
            tab,
        ],
        out_specs=[row(w) for w in widths],
        compiler_params=_params("parallel"),
        name="in_proj",
    )(x, mod, g.reshape(1, D_MODEL), w_in, kv_norm.reshape(DEPTH, 1, -1), w_ukv,
      q_norm.reshape(DEPTH, 1, -1), w_uq, cos_t, sin_t)


SUB = 256


def _softmax_numerators(s_c, s_l):
    m = jnp.max(s_c, axis=-1, keepdims=True)
    if s_l is not None:
        m = jnp.maximum(m, jnp.max(s_l, axis=-1, keepdims=True))
    return jnp.exp2(s_c - m), (jnp.exp2(s_l - m) if s_l is not None else None)


def _softmax_parts(s_c, s_l):
    e_c, e_l = _softmax_numerators(s_c, s_l)
    den = jnp.sum(e_c, axis=-1, keepdims=True)
    if e_l is not None:
        den = den + jnp.sum(e_l, axis=-1, keepdims=True)
    return e_c, e_l, den


def _attn_refs(refs, lat):
    if lat:
        return refs
    q_ref, kc_ref, vc_ref, _, o_ref = refs
    return q_ref, kc_ref, vc_ref, None, None, o_ref


def _one_ahead(items, scores, finish):
    nxt = scores(*items[0])
    for i, item in enumerate(items):
        cur, nxt = nxt, (scores(*items[i + 1]) if i + 1 < len(items) else None)
        finish(*item, cur)


def _mla_kernel(*refs, heads, n_sub, lat):
    q_ref, kc_ref, vc_ref, kl_ref, vl_ref, o_ref = _attn_refs(refs, lat)

    def scores(hh, j):
        qk = slice(hh * MLA_QK_PAD, (hh + 1) * MLA_QK_PAD)
        q = q_ref[j * SUB:(j + 1) * SUB, qk]
        return _dot_nt(q, kc_ref[:, qk]), (_dot_nt(q, kl_ref[:, qk]) if lat else None)

    def finish(hh, j, s):
        vv = slice(hh * MLA_V_PAD, (hh + 1) * MLA_V_PAD)
        e_c, e_l = _softmax_numerators(*s)
        o = _dot(e_c.astype(BF16), vc_ref[:, vv])
        if lat:
            o = o + _dot(e_l.astype(BF16), vl_ref[:, vv])
        o_ref[j * SUB:(j + 1) * SUB, hh * MLA_V:(hh + 1) * MLA_V] = (o[:, :MLA_V] / o[:, MLA_V:]).astype(BF16)

    _one_ahead([(hh, j) for hh in range(heads) for j in range(n_sub)], scores, finish)


def _diff_kernel(lam_ref, sub_ref, *refs, heads, n_sub, lat, lam_init):
    q_ref, kc_ref, vc_ref, kl_ref, vl_ref, o_ref = _attn_refs(refs, lat)
    lf = lam_ref[...]
    lam = (jnp.exp(jnp.sum(lf[0:1] * lf[1:2], axis=-1, keepdims=True))
           - jnp.exp(jnp.sum(lf[2:3] * lf[3:4], axis=-1, keepdims=True)) + lam_init)
    first = lax.broadcasted_iota(jnp.int32, (1, 2 * DIFF_QK), 1) < DIFF_QK

    def scores(hh, j):
        cols = slice(hh * DIFF_V, (hh + 1) * DIFF_V)
        q = q_ref[j * SUB:(j + 1) * SUB, cols]
        q1 = jnp.where(first, q, jnp.zeros_like(q))
        q2 = jnp.where(first, jnp.zeros_like(q), q)
        kc = kc_ref[:, cols]
        if not lat:
            return _dot_nt(q1, kc), None, _dot_nt(q2, kc), None
        kl = kl_ref[:, cols]
        return _dot_nt(q1, kc), _dot_nt(q1, kl), _dot_nt(q2, kc), _dot_nt(q2, kl)

    def finish(hh, j, s):
        cols = slice(hh * DIFF_V, (hh + 1) * DIFF_V)
        e1c, e1l, d1 = _softmax_parts(s[0], s[1])
        e2c, e2l, d2 = _softmax_parts(s[2], s[3])
        ratio = lam * d1 / d2
        o = _dot((e1c - e2c * ratio).astype(BF16), vc_ref[:, cols])
        if lat:
            o = o + _dot((e1l - e2l * ratio).astype(BF16), vl_ref[:, cols])
        o = o / d1
        o_ref[j * SUB:(j + 1) * SUB, cols] = (_rms(o) * sub_ref[...] * (1.0 - lam_init)).astype(BF16)

    _one_ahead([(hh, j) for hh in range(heads) for j in range(n_sub)], scores, finish)


def _attention(kernel_fn, name, q, k, v, prev, *, n_heads, w_qk, w_v, w_o, tq, extra_args=(), extra_specs=()):
    n_extra = len(extra_args)
    if prev is None:
        per_seg = SEG // tq
        in_specs = list(extra_specs) + [
            pl.BlockSpec((tq, w_qk), lambda b, h, t: ((b + 1) * per_seg + t, h)),
            pl.BlockSpec((CTX_LEN, w_qk), lambda b, h, t: (b, h)),
            pl.BlockSpec((CTX_LEN, w_v), lambda b, h, t: (b, h)),
            pl.BlockSpec((SEG, w_qk), lambda b, h, t: (b + 1, h)),
            pl.BlockSpec((SEG, w_v), lambda b, h, t: (b + 1, h)),
        ]
        return pl.pallas_call(
            functools.partial(kernel_fn, heads=1, n_sub=tq // SUB, lat=True),
            out_shape=jax.ShapeDtypeStruct((ROWS, n_heads * w_o), BF16),
            grid=(BATCH, n_heads, per_seg),
            in_specs=in_specs,
            out_specs=pl.BlockSpec((tq, w_o), lambda b, h, t: ((b + 1) * per_seg + t, h)),
            compiler_params=_params("parallel", "parallel", "arbitrary"),
            name=name,
        )(*extra_args, q, k, v, k, v)
    in_specs = list(extra_specs) + [
        pl.BlockSpec((CTX_LEN, n_heads * w_qk), lambda b: (b, 0)),
        pl.BlockSpec((CTX_LEN, n_heads * w_qk), lambda b: (b, 0)),
        pl.BlockSpec((CTX_LEN, n_heads * w_v), lambda b: (b, 0)),
        pl.BlockSpec(memory_space=pl.ANY),
    ]
    return pl.pallas_call(
        functools.partial(kernel_fn, heads=n_heads, n_sub=CTX_LEN // SUB, lat=False),
        out_shape=jax.ShapeDtypeStruct((ROWS, n_heads * w_o), BF16),
        grid=(BATCH,),
        in_specs=in_specs,
        out_specs=pl.BlockSpec((CTX_LEN, n_heads * w_o), lambda b: (b, 0)),
        input_output_aliases={n_extra + 3: 0},
        compiler_params=_params("parallel"),
        name=name + "_ctx",
    )(*extra_args, q, k, v, prev)


def _mla_attention(qmla, kmla, vmla, prev=None, *, tq=1024):
    return _attention(_mla_kernel, "mla_attention", qmla, kmla, vmla, prev,
                      n_heads=MLA_HEADS, w_qk=MLA_QK_PAD, w_v=MLA_V_PAD, w_o=MLA_V, tq=tq)


def _diff_attention(dq, dk, dv, diff_lambda, subln, l, lam_init, prev=None, *, tq=1024):
    layer = lambda *_: (l, 0, 0)
    extra_specs = (pl.BlockSpec((None, 4, DIFF_QK), layer), pl.BlockSpec((None, 1, DIFF_V), layer))
    return _attention(functools.partial(_diff_kernel, lam_init=lam_init), "diff_attention", dq, dk, dv, prev,
                      n_heads=DIFF_HEADS, w_qk=2 * DIFF_QK, w_v=DIFF_V, w_o=DIFF_V, tq=tq,
                      extra_args=(diff_lambda, subln.reshape(DEPTH, 1, DIFF_V)), extra_specs=extra_specs)


def _dft_tables(n_pos):
    k = np.arange(n_pos, dtype=np.int64)
    ang = 2.0 * np.pi * ((k[:, None] * k[None, :]) % n_pos) / n_pos
    pos = np.concatenate([np.cos(ang), -np.sin(ang)], axis=1) / math.sqrt(n_pos)
    c = np.arange(FOURIER_CH, dtype=np.int64)
    angc = 2.0 * np.pi * ((c[:, None] * c[None, :]) % FOURIER_CH) / FOURIER_CH
    ch = np.concatenate([np.cos(angc), np.sin(angc)], axis=1) / math.sqrt(FOURIER_CH)
    return jnp.asarray(pos, dtype=BF16), jnp.asarray(ch, dtype=BF16)


def _fourier_kernel(u_ref, ch_ref, pos_ref, *rest, n_pos):
    o_ref, rhs_scr = rest[-2], rest[-1]

    @pl.when(pl.program_id(1) == 0)
    def _():
        for g in range(FOURIER_GROUPS):
            a = _dot(u_ref[:, g * FOURIER_CH:(g + 1) * FOURIER_CH], ch_ref[...])
            rhs_scr[0:n_pos, g * FOURIER_CH:(g + 1) * FOURIER_CH] = a[:, :FOURIER_CH].astype(BF16)
            rhs_scr[n_pos:2 * n_pos, g * FOURIER_CH:(g + 1) * FOURIER_CH] = a[:, FOURIER_CH:].astype(BF16)

    o_ref[...] = _dot(pos_ref[...], rhs_scr[...]).astype(BF16)


def _fourier(u, prev=None, *, n_pos, first_block, tr=256):
    pos_t, ch_t = _dft_tables(n_pos)
    n_j = n_pos // tr
    in_specs = [
        pl.BlockSpec((n_pos, FOURIER_WIDTH), lambda b, j: (b + first_block, 0)),
        pl.BlockSpec((FOURIER_CH, 2 * FOURIER_CH), lambda b, j: (0, 0)),
        pl.BlockSpec((tr, 2 * n_pos), lambda b, j: (j, 0)),
    ]
    args = [u, ch_t, pos_t]
    aliases = {}
    if prev is not None:
        in_specs.append(pl.BlockSpec(memory_space=pl.ANY))
        args.append(prev)
        aliases = {3: 0}
    return pl.pallas_call(
        functools.partial(_fourier_kernel, n_pos=n_pos),
        out_shape=jax.ShapeDtypeStruct((ROWS, FOURIER_WIDTH), BF16),
        grid=(BATCH, n_j),
        in_specs=in_specs,
        out_specs=pl.BlockSpec((tr, FOURIER_WIDTH), lambda b, j: ((b + first_block) * n_j + j, 0)),
        scratch_shapes=[pltpu.VMEM((2 * n_pos, FOURIER_WIDTH), BF16)],
        input_output_aliases=aliases,
        compiler_params=_params("parallel", "arbitrary"),
        name="fourier_mix",
    )(*args)


def _out_kernel(x_ref, mod_ref, om_ref, od_ref, of_ref, w_ref, o_ref):
    y = (_dot(om_ref[...], w_ref[0:MLA_WIDTH, :])
         + _dot(od_ref[...], w_ref[MLA_WIDTH:MLA_WIDTH + DIFF_WIDTH, :])
         + _dot(of_ref[...], w_ref[MLA_WIDTH + DIFF_WIDTH:, :]))
    o_ref[...] = x_ref[...] + mod_ref[2:3, :] * y


def _out_proj(x, mod, l, o_mla, o_diff, o_four, w_out, *, row_off=0, tr=512):
    per_seg = SEG // tr
    off = row_off // tr
    row = lambda w: pl.BlockSpec((tr, w), lambda i: (i + off, 0))
    return pl.pallas_call(
        _out_kernel,
        out_shape=jax.ShapeDtypeStruct((ROWS - row_off, D_MODEL), F32),
        grid=((ROWS - row_off) // tr,),
        in_specs=[
            row(D_MODEL),
            pl.BlockSpec((None, None, None, 3, D_MODEL), lambda i: (l, (i + off) // per_seg, 1, 0, 0)),
            row(MLA_WIDTH), row(DIFF_WIDTH), row(FOURIER_WIDTH),
            _resident((None, D_MODEL, D_MODEL), lambda i: (l, 0, 0)),
        ],
        out_specs=pl.BlockSpec((tr, D_MODEL), lambda i: (i, 0)),
        compiler_params=_params("parallel"),
        name="out_proj",
    )(x, mod, o_mla, o_diff, o_four, w_out)


def _rope_tables():
    rows = SEQ // GRID_W
    pos_r = jnp.repeat(jnp.arange(rows), GRID_W)
    pos_c = jnp.tile(jnp.arange(GRID_W), rows)
    d = MLA_ROPE // 2
    half = d // 2
    inv = ROPE_THETA ** (-2.0 * jnp.arange(half, dtype=F32) / d)

    def tabs(pos):
        ang = pos.astype(F32)[:, None] * inv[None, :]
        return jnp.cos(ang), jnp.sin(ang)

    cr, sr = tabs(pos_r)
    cc, sc = tabs(pos_c)
    cos64 = jnp.concatenate([cr, cr, cc, cc], axis=-1)
    sin64 = jnp.concatenate([-sr, sr, -sc, sc], axis=-1)
    cos_t = jnp.stack([jnp.ones((SEG, 128), F32), jnp.tile(cos64, (1, 2))])
    sin_t = jnp.stack([jnp.zeros((SEG, 128), F32), jnp.tile(sin64, (1, 2))])
    return cos_t, sin_t


def kernel(x, c, ctx, c_ctx, ada_w, ada_b, norm_g, ffn_wg, ffn_wu, ffn_wd, w_in, mla_q_norm, mla_kv_norm,
           mla_w_uq, mla_w_ukv, diff_lambda, diff_subln, w_out, final_norm):
    s_in = jnp.concatenate([c_ctx[None, :], c, jnp.zeros((MOD_ROWS - 1 - BATCH, D_MODEL), F32)], axis=0)
    wg = ffn_wg.astype(BF16)
    wu = ffn_wu.astype(BF16)
    wd = ffn_wd.astype(BF16)
    w_in_p = jnp.concatenate(
        [w_in[..., :P_KROPE + MLA_ROPE].astype(BF16),
         jnp.zeros((DEPTH, D_MODEL, P_DK - P_KROPE - MLA_ROPE), BF16),
         w_in[..., P_KROPE + MLA_ROPE:].astype(BF16)], axis=-1)
    w_uq_p = jnp.pad(mla_w_uq.astype(BF16).reshape(DEPTH, MLA_Q_RANK, MLA_HEADS, MLA_NOPE + MLA_ROPE),
                     ((0, 0), (0, 0), (0, 0), (0, MLA_QK_PAD - MLA_NOPE - MLA_ROPE))
                     ).reshape(DEPTH, MLA_Q_RANK, MLA_HEADS * MLA_QK_PAD)
    w_ukv = mla_w_ukv.astype(BF16)
    w_o = w_out.astype(BF16)
    cos_t, sin_t = _rope_tables()

    mod = _ada_table(s_in, ada_w, ada_b).reshape(DEPTH, MOD_ROWS, 3, 3, D_MODEL)

    for l in range(DEPTH):
        last = l == DEPTH - 1
        lam_init = 0.8 - 0.6 * math.exp(-0.3 * l)
        if l == 0:
            z = _ffn(x.reshape(BATCH * SEQ, D_MODEL), mod, l, 0, norm_g[l, 0], wg, wu, wd,
                     seg_base=1, out_rows=ROWS, out_off=SEG)
            z = _ffn(ctx.reshape(BATCH * CTX_LEN, D_MODEL), mod, l, 0, norm_g[l, 0], wg, wu, wd, prev=z)
        else:
            z = _ffn(z, mod, l, 0, norm_g[l, 0], wg, wu, wd)
        kmla, vmla, qmla, dk, dv, dq, u = _proj(
            z, mod, l, norm_g[l, 1], w_in_p, mla_kv_norm, w_ukv, mla_q_norm, w_uq_p, cos_t, sin_t)
        o_mla = _mla_attention(qmla, kmla, vmla)
        o_diff = _diff_attention(dq, dk, dv, diff_lambda, diff_subln, l, lam_init)
        o_four = _fourier(u, n_pos=SEQ, first_block=1)
        if last:
            z = _out_proj(z, mod, l, o_mla, o_diff, o_four, w_o, row_off=SEG)
            z = _ffn(z, mod, l, 1, norm_g[l, 2], wg, wu, wd, seg_base=1, final_g=final_norm)
        else:
            o_mla = _mla_attention(qmla, kmla, vmla, o_mla)
            o_diff = _diff_attention(dq, dk, dv, diff_lambda, diff_subln, l, lam_init, o_diff)
            o_four = _fourier(u, o_four, n_pos=CTX_LEN, first_block=0)
            z = _out_proj(z, mod, l, o_mla, o_diff, o_four, w_o)
            z = _ffn(z, mod, l, 1, norm_g[l, 2], wg, wu, wd)
    return z.reshape(BATCH, SEQ, D_MODEL)
```

```python
import functools
import math

import numpy as np
import jax
import jax.numpy as jnp
from jax import lax
from jax.experimental import pallas as pl
from jax.experimental.pallas import tpu as pltpu

D_MODEL = 2048
BATCH = 8
SEQ = 2048
DEPTH = 2
GRID_W = 64
CTX_LEN = 256
EPS = 1e-6
ROPE_THETA = 10000.0
D_FF = 5632
N_MOD = 9

MLA_HEADS = 8
MLA_Q_RANK = 512
MLA_KV_RANK = 256
MLA_NOPE = 128
MLA_ROPE = 64
MLA_V = 128
MLA_SCALE = (MLA_NOPE + MLA_ROPE) ** -0.5
MLA_QK_PAD = 256
MLA_V_PAD = 256

DIFF_HEADS = 4
DIFF_QK = 64
DIFF_V = 2 * DIFF_QK
DIFF_SCALE = DIFF_QK ** -0.5

FOURIER_GROUPS = 4
FOURIER_CH = 128
MLA_WIDTH = MLA_HEADS * MLA_V
DIFF_WIDTH = DIFF_HEADS * DIFF_V
FOURIER_WIDTH = FOURIER_GROUPS * FOURIER_CH

SEG = SEQ
LAT_ROWS = BATCH * SEQ
CTX_ROWS = BATCH * CTX_LEN
MOD_ROWS = 16

P_CKV = 0
P_KROPE = 256
P_DK = 384
P_DV = 896
P_CQ = 1408
P_DQ = 1920
P_U = 2432
P_COLS = 2944

VMEM_LIMIT = 56 * 1024 * 1024
VMEM_LIMIT_FFN = 63 * 1024 * 1024 + 512 * 1024
LOG2_E = math.log2(math.e)

F32 = jnp.float32
BF16 = jnp.bfloat16


def _silu(v):
    return v / (1.0 + jnp.exp(-v))


def _dot(a, b):
    return jnp.dot(a, b, preferred_element_type=F32)


def _dot_nt(a, b):
    return lax.dot_general(a, b, (((1,), (1,)), ((), ())), preferred_element_type=F32)


def _rms(v):
    return v * lax.rsqrt(jnp.mean(v * v, axis=-1, keepdims=True) + EPS)


def _row_chunks(n_rows, chunk):
    return [slice(r, r + chunk) for r in range(0, n_rows, chunk)]


def _modulate_rows(z_ref, g_ref, mod_ref, h_ref):
    gain = g_ref[...] * (1.0 + mod_ref[1:2, :])
    shift = mod_ref[0:1, :]
    for rows in _row_chunks(z_ref.shape[0], 16):
        h_ref[rows, :] = (_rms(z_ref[rows, :]) * gain + shift).astype(BF16)


def _params(*sem, vmem=VMEM_LIMIT):
    return pltpu.CompilerParams(dimension_semantics=sem, vmem_limit_bytes=vmem)


def _resident(shape, index_map):
    return pl.BlockSpec(shape, index_map, pipeline_mode=pl.Buffered(1))


def _mod_spec(l, seg0, tile, grp, n_grid):
    if n_grid == 1:
        return pl.BlockSpec((None, None, None, 3, D_MODEL), lambda i: (l, seg0 + i * tile // SEG, grp, 0, 0))
    return pl.BlockSpec((None, None, None, 3, D_MODEL), lambda i, f: (l, seg0 + i * tile // SEG, grp, 0, 0))


def _ada_kernel(s_ref, w_ref, b_ref, o_ref):
    s = _silu(s_ref[...])
    o_ref[...] = _dot(s.astype(BF16), w_ref[...].astype(BF16)) + b_ref[...]


def _ada_table(s_in, ada_w, ada_b):
    tn = 1024
    n_cols = N_MOD * D_MODEL
    return pl.pallas_call(
        _ada_kernel,
        out_shape=jax.ShapeDtypeStruct((DEPTH, MOD_ROWS, n_cols), F32),
        grid=(DEPTH, n_cols // tn),
        in_specs=[
            pl.BlockSpec((MOD_ROWS, D_MODEL), lambda l, n: (0, 0)),
            pl.BlockSpec((None, D_MODEL, tn), lambda l, n: (l, 0, n)),
            pl.BlockSpec((None, 1, tn), lambda l, n: (l, 0, n)),
        ],
        out_specs=pl.BlockSpec((None, MOD_ROWS, tn), lambda l, n: (l, 0, n)),
        compiler_params=_params("parallel", "parallel"),
        name="ada_table",
    )(s_in, ada_w, ada_b.reshape(DEPTH, 1, n_cols))


DOWN_COLS = 512


def _ffn_kernel(z_ref, mod_ref, g_ref, wg_ref, wu_ref, wd_ref, *rest, n_f, final):
    if final:
        fn_ref, o_ref, h_scr = rest
    else:
        o_ref, h_scr = rest
    f = pl.program_id(1)

    @pl.when(f == 0)
    def _():
        _modulate_rows(z_ref, g_ref, mod_ref, h_scr)
        o_ref[...] = jnp.zeros_like(o_ref)

    h = h_scr[...]
    a = (_silu(_dot(h, wg_ref[...])) * _dot(h, wu_ref[...])).astype(BF16)
    for c in range(0, D_MODEL, DOWN_COLS):
        o_ref[:, c:c + DOWN_COLS] += _dot(a, wd_ref[:, c:c + DOWN_COLS])

    @pl.when(f == n_f - 1)
    def _():
        half_gate = 0.5 * mod_ref[2:3, :]
        for rows in _row_chunks(o_ref.shape[0], 8):
            y = z_ref[rows, :] + half_gate * o_ref[rows, :]
            if final:
                y = _rms(y) * fn_ref[...]
            o_ref[rows, :] = y


def _ffn(z, mod, l, k, seg0, g, wg, wu, wd, *, final_g=None, tm=1024, tf=512):
    n_rows = z.shape[0]
    n_f = D_FF // tf
    final = final_g is not None
    in_specs = [
        pl.BlockSpec((tm, D_MODEL), lambda i, f: (i, 0)),
        _mod_spec(l, seg0, tm, 2 * k, 2),
        pl.BlockSpec((1, D_MODEL), lambda i, f: (0, 0)),
        pl.BlockSpec((None, None, D_MODEL, tf), lambda i, f: (l, k, 0, f)),
        pl.BlockSpec((None, None, D_MODEL, tf), lambda i, f: (l, k, 0, f)),
        pl.BlockSpec((None, None, tf, D_MODEL), lambda i, f: (l, k, f, 0)),
    ]
    args = [z, mod, g.reshape(1, D_MODEL), wg, wu, wd]
    if final:
        in_specs.append(pl.BlockSpec((1, D_MODEL), lambda i, f: (0, 0)))
        args.append(final_g.reshape(1, D_MODEL))
    return pl.pallas_call(
        functools.partial(_ffn_kernel, n_f=n_f, final=final),
        out_shape=jax.ShapeDtypeStruct((n_rows, D_MODEL), F32),
        grid=(n_rows // tm, n_f),
        in_specs=in_specs,
        out_specs=pl.BlockSpec((tm, D_MODEL), lambda i, f: (i, 0)),
        scratch_shapes=[pltpu.VMEM((tm, D_MODEL), BF16)],
        compiler_params=_params("parallel", "arbitrary", vmem=VMEM_LIMIT_FFN),
        name="ffn_final" if final else "ffn",
    )(*args)


def _rope(v, cos, sin, lo_half):
    swapped = jnp.where(lo_half, pltpu.roll(v, 112, 1), pltpu.roll(v, 16, 1))
    return v * cos + swapped * sin


def _proj_kernel(*refs, rope, kv_only):
    refs = list(refs)
    x_ref, mod_ref, g_ref, win_ref, kvn_ref, wukv_ref = refs[:6]
    del refs[:6]
    if not kv_only:
        qn_ref, wuq_ref = refs[:2]
        del refs[:2]
    if rope:
        cos_ref, sin_ref = refs[:2]
        del refs[:2]
    h_scr = refs.pop()
    if kv_only:
        kmla_ref, vmla_ref, dk_ref, dv_ref = refs
    else:
        kmla_ref, vmla_ref, dk_ref, dv_ref, qmla_ref, dq_ref, u_ref = refs

    _modulate_rows(x_ref, g_ref, mod_ref, h_scr)
    hb = h_scr[...]
    if rope:
        lane = lax.broadcasted_iota(jnp.int32, (1, 128), 1)
        rotate = functools.partial(_rope, cos=cos_ref[...], sin=sin_ref[...], lo_half=(lane % 32) < 16)
    else:
        rotate = lambda v: v
    q_scale = MLA_SCALE * LOG2_E
    dq_scale = DIFF_SCALE * LOG2_E

    ckv = _rms(_dot(hb, win_ref[:, P_CKV:P_KROPE])) * kvn_ref[...]
    kv = _dot(ckv.astype(BF16), wukv_ref[...])
    k_rope = rotate(_dot(hb, win_ref[:, P_KROPE:P_DK])).astype(BF16)
    for h in range(MLA_HEADS):
        c0 = h * (MLA_NOPE + MLA_V)
        kmla_ref[:, h * MLA_QK_PAD:h * MLA_QK_PAD + MLA_NOPE] = kv[:, c0:c0 + MLA_NOPE].astype(BF16)
        kmla_ref[:, h * MLA_QK_PAD + MLA_NOPE:(h + 1) * MLA_QK_PAD] = k_rope
        vmla_ref[:, h * MLA_V_PAD:h * MLA_V_PAD + MLA_V] = kv[:, c0 + MLA_NOPE:c0 + MLA_NOPE + MLA_V].astype(BF16)
        vmla_ref[:, h * MLA_V_PAD + MLA_V:(h + 1) * MLA_V_PAD] = jnp.ones((kv.shape[0], MLA_V_PAD - MLA_V), BF16)
    dk = _dot(hb, win_ref[:, P_DK:P_DV])
    for j in range(DIFF_HEADS):
        dk_ref[:, j * 128:(j + 1) * 128] = rotate(dk[:, j * 128:(j + 1) * 128]).astype(BF16)
    dv_ref[...] = _dot(hb, win_ref[:, P_DV:P_CQ]).astype(BF16)
    if kv_only:
        return

    cq = _rms(_dot(hb, win_ref[:, P_CQ:P_DQ])) * qn_ref[...]
    q = _dot(cq.astype(BF16), wuq_ref[...])
    for h in range(MLA_HEADS):
        c0 = h * MLA_QK_PAD
        qmla_ref[:, c0:c0 + 128] = (q[:, c0:c0 + 128] * q_scale).astype(BF16)
        qmla_ref[:, c0 + 128:c0 + 256] = (rotate(q[:, c0 + 128:c0 + 256]) * q_scale).astype(BF16)
    dq = _dot(hb, win_ref[:, P_DQ:P_U])
    for j in range(DIFF_HEADS):
        dq_ref[:, j * 128:(j + 1) * 128] = (rotate(dq[:, j * 128:(j + 1) * 128]) * dq_scale).astype(BF16)
    u_ref[...] = _dot(hb, win_ref[:, P_U:P_COLS]).astype(BF16)


def _proj(x, mod, l, seg0, g, w_in, kv_norm, w_ukv, q_norm, w_uq, rope_tabs=None, *, kv_only=False, tr=512):
    n_rows = x.shape[0]
    row = lambda w: pl.BlockSpec((tr, w), lambda i: (i, 0))
    vec = lambda w: _resident((None, 1, w), lambda i: (l, 0, 0))
    mat = lambda r, c: _resident((None, r, c), lambda i: (l, 0, 0))
    in_specs = [row(D_MODEL), _mod_spec(l, seg0, tr, 1, 1), _resident((1, D_MODEL), lambda i: (0, 0)),
                mat(D_MODEL, P_COLS), vec(MLA_KV_RANK), mat(MLA_KV_RANK, MLA_HEADS * (MLA_NOPE + MLA_V))]
    args = [x, mod, g.reshape(1, D_MODEL), w_in, kv_norm.reshape(DEPTH, 1, -1), w_ukv]
    widths = [MLA_HEADS * MLA_QK_PAD, MLA_HEADS * MLA_V_PAD, DIFF_WIDTH, DIFF_WIDTH]
    if not kv_only:
        in_specs += [vec(MLA_Q_RANK), mat(MLA_Q_RANK, MLA_HEADS * MLA_QK_PAD)]
        args += [q_norm.reshape(DEPTH, 1, -1), w_uq]
        widths += [MLA_HEADS * MLA_QK_PAD, DIFF_WIDTH, FOURIER_WIDTH]
    if rope_tabs is not None:
        per_seg = SEG // tr
        in_specs += [pl.BlockSpec((tr, 128), lambda i: (i % per_seg, 0))] * 2
        args += list(rope_tabs)
    return pl.pallas_call(
        functools.partial(_proj_kernel, rope=rope_tabs is not None, kv_only=kv_only),
        out_shape=[jax.ShapeDtypeStruct((n_rows, w), BF16) for w in widths],
        grid=(n_rows // tr,),
        in_specs=in_specs,
        out_specs=[row(w) for w in widths],
        scratch_shapes=[pltpu.VMEM((tr, D_MODEL), BF16)],
        compiler_params=_params("parallel"),
        name="in_proj",
    )(*args)


SUB = 256


def _softmax_numerators(s_c, s_l):
    m = jnp.max(s_c, axis=-1, keepdims=True)
    if s_l is not None:
        m = jnp.maximum(m, jnp.max(s_l, axis=-1, keepdims=True))
    return jnp.exp2(s_c - m), (jnp.exp2(s_l - m) if s_l is not None else None)


def _softmax_parts(s_c, s_l):
    e_c, e_l = _softmax_numerators(s_c, s_l)
    den = jnp.sum(e_c, axis=-1, keepdims=True)
    if e_l is not None:
        den = den + jnp.sum(e_l, axis=-1, keepdims=True)
    return e_c, e_l, den


def _attn_refs(refs, lat):
    if lat:
        return refs
    q_ref, kc_ref, vc_ref, o_ref = refs
    return q_ref, kc_ref, vc_ref, None, None, o_ref


def _one_ahead(items, scores, finish):
    nxt = scores(*items[0])
    for i, item in enumerate(items):
        cur, nxt = nxt, (scores(*items[i + 1]) if i + 1 < len(items) else None)
        finish(*item, cur)


def _mla_kernel(*refs, heads, n_sub, lat):
    q_ref, kc_ref, vc_ref, kl_ref, vl_ref, o_ref = _attn_refs(refs, lat)

    def scores(hh, j):
        qk = slice(hh * MLA_QK_PAD, (hh + 1) * MLA_QK_PAD)
        q = q_ref[j * SUB:(j + 1) * SUB, qk]
        return _dot_nt(q, kc_ref[:, qk]), (_dot_nt(q, kl_ref[:, qk]) if lat else None)

    def finish(hh, j, s):
        vv = slice(hh * MLA_V_PAD, (hh + 1) * MLA_V_PAD)
        e_c, e_l = _softmax_numerators(*s)
        o = _dot(e_c.astype(BF16), vc_ref[:, vv])
        if lat:
            o = o + _dot(e_l.astype(BF16), vl_ref[:, vv])
        o_ref[j * SUB:(j + 1) * SUB, hh * MLA_V:(hh + 1) * MLA_V] = (o[:, :MLA_V] / o[:, MLA_V:]).astype(BF16)

    _one_ahead([(hh, j) for hh in range(heads) for j in range(n_sub)], scores, finish)


def _diff_kernel(lam_ref, sub_ref, *refs, heads, n_sub, lat, lam_init):
    q_ref, kc_ref, vc_ref, kl_ref, vl_ref, o_ref = _attn_refs(refs, lat)
    lf = lam_ref[...]
    lam = (jnp.exp(jnp.sum(lf[0:1] * lf[1:2], axis=-1, keepdims=True))
           - jnp.exp(jnp.sum(lf[2:3] * lf[3:4], axis=-1, keepdims=True)) + lam_init)
    first = lax.broadcasted_iota(jnp.int32, (1, 2 * DIFF_QK), 1) < DIFF_QK

    def scores(hh, j):
        cols = slice(hh * DIFF_V, (hh + 1) * DIFF_V)
        q = q_ref[j * SUB:(j + 1) * SUB, cols]
        q1 = jnp.where(first, q, jnp.zeros_like(q))
        q2 = jnp.where(first, jnp.zeros_like(q), q)
        kc = kc_ref[:, cols]
        if not lat:
            return _dot_nt(q1, kc), None, _dot_nt(q2, kc), None
        kl = kl_ref[:, cols]
        return _dot_nt(q1, kc), _dot_nt(q1, kl), _dot_nt(q2, kc), _dot_nt(q2, kl)

    def finish(hh, j, s):
        cols = slice(hh * DIFF_V, (hh + 1) * DIFF_V)
        e1c, e1l, d1 = _softmax_parts(s[0], s[1])
        e2c, e2l, d2 = _softmax_parts(s[2], s[3])
        ratio = lam * d1 / d2
        o = _dot((e1c - e2c * ratio).astype(BF16), vc_ref[:, cols])
        if lat:
            o = o + _dot((e1l - e2l * ratio).astype(BF16), vl_ref[:, cols])
        o = o / d1
        o_ref[j * SUB:(j + 1) * SUB, cols] = (_rms(o) * sub_ref[...] * (1.0 - lam_init)).astype(BF16)

    _one_ahead([(hh, j) for hh in range(heads) for j in range(n_sub)], scores, finish)


def _attention(kernel_fn, name, q, k_ctx, v_ctx, k_lat=None, v_lat=None, *, n_heads, w_qk, w_v, w_o, tq,
               extra_args=(), extra_specs=()):
    if k_lat is not None:
        per_seg = SEQ // tq
        in_specs = list(extra_specs) + [
            pl.BlockSpec((tq, w_qk), lambda b, h, t: (b * per_seg + t, h)),
            pl.BlockSpec((CTX_LEN, w_qk), lambda b, h, t: (b, h)),
            pl.BlockSpec((CTX_LEN, w_v), lambda b, h, t: (b, h)),
            pl.BlockSpec((SEQ, w_qk), lambda b, h, t: (b, h)),
            pl.BlockSpec((SEQ, w_v), lambda b, h, t: (b, h)),
        ]
        return pl.pallas_call(
            functools.partial(kernel_fn, heads=1, n_sub=tq // SUB, lat=True),
            out_shape=jax.ShapeDtypeStruct((LAT_ROWS, n_heads * w_o), BF16),
            grid=(BATCH, n_heads, per_seg),
            in_specs=in_specs,
            out_specs=pl.BlockSpec((tq, w_o), lambda b, h, t: (b * per_seg + t, h)),
            compiler_params=_params("parallel", "parallel", "arbitrary"),
            name=name,
        )(*extra_args, q, k_ctx, v_ctx, k_lat, v_lat)
    in_specs = list(extra_specs) + [
        pl.BlockSpec((CTX_LEN, n_heads * w_qk), lambda b: (b, 0)),
        pl.BlockSpec((CTX_LEN, n_heads * w_qk), lambda b: (b, 0)),
        pl.BlockSpec((CTX_LEN, n_heads * w_v), lambda b: (b, 0)),
    ]
    return pl.pallas_call(
        functools.partial(kernel_fn, heads=n_heads, n_sub=CTX_LEN // SUB, lat=False),
        out_shape=jax.ShapeDtypeStruct((CTX_ROWS, n_heads * w_o), BF16),
        grid=(BATCH,),
        in_specs=in_specs,
        out_specs=pl.BlockSpec((CTX_LEN, n_heads * w_o), lambda b: (b, 0)),
        compiler_params=_params("parallel"),
        name=name + "_ctx",
    )(*extra_args, q, k_ctx, v_ctx)


def _mla_attention(q, k_ctx, v_ctx, k_lat=None, v_lat=None, *, tq=1024):
    return _attention(_mla_kernel, "mla_attention", q, k_ctx, v_ctx, k_lat, v_lat,
                      n_heads=MLA_HEADS, w_qk=MLA_QK_PAD, w_v=MLA_V_PAD, w_o=MLA_V, tq=tq)


def _diff_attention(diff_lambda, subln, l, lam_init, q, k_ctx, v_ctx, k_lat=None, v_lat=None, *, tq=1024):
    layer = lambda *_: (l, 0, 0)
    extra_specs = (pl.BlockSpec((None, 4, DIFF_QK), layer), pl.BlockSpec((None, 1, DIFF_V), layer))
    return _attention(functools.partial(_diff_kernel, lam_init=lam_init), "diff_attention",
                      q, k_ctx, v_ctx, k_lat, v_lat,
                      n_heads=DIFF_HEADS, w_qk=2 * DIFF_QK, w_v=DIFF_V, w_o=DIFF_V, tq=tq,
                      extra_args=(diff_lambda, subln.reshape(DEPTH, 1, DIFF_V)), extra_specs=extra_specs)


def _dft_tables(n_pos):
    k = np.arange(n_pos, dtype=np.int64)
    ang = 2.0 * np.pi * ((k[:, None] * k[None, :]) % n_pos) / n_pos
    pos = np.concatenate([np.cos(ang), -np.sin(ang)], axis=1) / math.sqrt(n_pos)
    c = np.arange(FOURIER_CH, dtype=np.int64)
    angc = 2.0 * np.pi * ((c[:, None] * c[None, :]) % FOURIER_CH) / FOURIER_CH
    ch = np.concatenate([np.cos(angc), np.sin(angc)], axis=1) / math.sqrt(FOURIER_CH)
    return jnp.asarray(pos, dtype=BF16), jnp.asarray(ch, dtype=BF16)


def _fourier_kernel(u_ref, ch_ref, pos_ref, o_ref, rhs_scr, *, n_pos):
    @pl.when(pl.program_id(1) == 0)
    def _():
        for g in range(FOURIER_GROUPS):
            a = _dot(u_ref[:, g * FOURIER_CH:(g + 1) * FOURIER_CH], ch_ref[...])
            rhs_scr[0:n_pos, g * FOURIER_CH:(g + 1) * FOURIER_CH] = a[:, :FOURIER_CH].astype(BF16)
            rhs_scr[n_pos:2 * n_pos, g * FOURIER_CH:(g + 1) * FOURIER_CH] = a[:, FOURIER_CH:].astype(BF16)

    o_ref[...] = _dot(pos_ref[...], rhs_scr[...]).astype(BF16)


def _fourier(u, n_pos, *, tr=256):
    pos_t, ch_t = _dft_tables(n_pos)
    n_j = n_pos // tr
    return pl.pallas_call(
        functools.partial(_fourier_kernel, n_pos=n_pos),
        out_shape=jax.ShapeDtypeStruct((BATCH * n_pos, FOURIER_WIDTH), BF16),
        grid=(BATCH, n_j),
        in_specs=[
            pl.BlockSpec((n_pos, FOURIER_WIDTH), lambda b, j: (b, 0)),
            pl.BlockSpec((FOURIER_CH, 2 * FOURIER_CH), lambda b, j: (0, 0)),
            pl.BlockSpec((tr, 2 * n_pos), lambda b, j: (j, 0)),
        ],
        out_specs=pl.BlockSpec((tr, FOURIER_WIDTH), lambda b, j: (b * n_j + j, 0)),
        scratch_shapes=[pltpu.VMEM((2 * n_pos, FOURIER_WIDTH), BF16)],
        compiler_params=_params("parallel", "arbitrary"),
        name="fourier_mix",
    )(u, ch_t, pos_t)


def _out_kernel(x_ref, mod_ref, om_ref, od_ref, of_ref, w_ref, o_ref):
    y = (_dot(om_ref[...], w_ref[0:MLA_WIDTH, :])
         + _dot(od_ref[...], w_ref[MLA_WIDTH:MLA_WIDTH + DIFF_WIDTH, :])
         + _dot(of_ref[...], w_ref[MLA_WIDTH + DIFF_WIDTH:, :]))
    o_ref[...] = x_ref[...] + mod_ref[2:3, :] * y


def _out_proj(x, mod, l, seg0, o_mla, o_diff, o_four, w_out, *, tr=512):
    n_rows = x.shape[0]
    row = lambda w: pl.BlockSpec((tr, w), lambda i: (i, 0))
    return pl.pallas_call(
        _out_kernel,
        out_shape=jax.ShapeDtypeStruct((n_rows, D_MODEL), F32),
        grid=(n_rows // tr,),
        in_specs=[
            row(D_MODEL), _mod_spec(l, seg0, tr, 1, 1),
            row(MLA_WIDTH), row(DIFF_WIDTH), row(FOURIER_WIDTH),
            _resident((None, D_MODEL, D_MODEL), lambda i: (l, 0, 0)),
        ],
        out_specs=row(D_MODEL),
        compiler_params=_params("parallel"),
        name="out_proj",
    )(x, mod, o_mla, o_diff, o_four, w_out)


def _rope_tables():
    rows = SEQ // GRID_W
    pos_r = jnp.repeat(jnp.arange(rows), GRID_W)
    pos_c = jnp.tile(jnp.arange(GRID_W), rows)
    d = MLA_ROPE // 2
    half = d // 2
    inv = ROPE_THETA ** (-2.0 * jnp.arange(half, dtype=F32) / d)

    def tabs(pos):
        ang = pos.astype(F32)[:, None] * inv[None, :]
        return jnp.cos(ang), jnp.sin(ang)

    cr, sr = tabs(pos_r)
    cc, sc = tabs(pos_c)
    cos64 = jnp.concatenate([cr, cr, cc, cc], axis=-1)
    sin64 = jnp.concatenate([-sr, sr, -sc, sc], axis=-1)
    return jnp.tile(cos64, (1, 2)), jnp.tile(sin64, (1, 2))


def kernel(x, c, ctx, c_ctx, ada_w, ada_b, norm_g, ffn_wg, ffn_wu, ffn_wd, w_in, mla_q_norm, mla_kv_norm,
           mla_w_uq, mla_w_ukv, diff_lambda, diff_subln, w_out, final_norm):
    s_in = jnp.concatenate([c_ctx[None, :], c, jnp.zeros((MOD_ROWS - 1 - BATCH, D_MODEL), F32)], axis=0)
    wg = ffn_wg.astype(BF16)
    wu = ffn_wu.astype(BF16)
    wd = ffn_wd.astype(BF16)
    w_in_p = jnp.concatenate(
        [w_in[..., :P_KROPE + MLA_ROPE].astype(BF16),
         jnp.zeros((DEPTH, D_MODEL, P_DK - P_KROPE - MLA_ROPE), BF16),
         w_in[..., P_KROPE + MLA_ROPE:].astype(BF16)], axis=-1)
    w_uq_p = jnp.pad(mla_w_uq.astype(BF16).reshape(DEPTH, MLA_Q_RANK, MLA_HEADS, MLA_NOPE + MLA_ROPE),
                     ((0, 0), (0, 0), (0, 0), (0, MLA_QK_PAD - MLA_NOPE - MLA_ROPE))
                     ).reshape(DEPTH, MLA_Q_RANK, MLA_HEADS * MLA_QK_PAD)
    w_ukv = mla_w_ukv.astype(BF16)
    w_o = w_out.astype(BF16)
    rope_tabs = _rope_tables()

    mod = _ada_table(s_in, ada_w, ada_b).reshape(DEPTH, MOD_ROWS, 3, 3, D_MODEL)

    zl = x.reshape(LAT_ROWS, D_MODEL)
    zc = ctx.reshape(CTX_ROWS, D_MODEL)
    for l in range(DEPTH):
        last = l == DEPTH - 1
        lam_init = 0.8 - 0.6 * math.exp(-0.3 * l)
        ffn = functools.partial(_ffn, mod=mod, l=l, wg=wg, wu=wu, wd=wd)
        proj = functools.partial(_proj, mod=mod, l=l, g=norm_g[l, 1], w_in=w_in_p, kv_norm=mla_kv_norm,
                                 w_ukv=w_ukv, q_norm=mla_q_norm, w_uq=w_uq_p)
        diff_attn = functools.partial(_diff_attention, diff_lambda, diff_subln, l, lam_init)

        zl = ffn(zl, k=0, seg0=1, g=norm_g[l, 0])
        zc = ffn(zc, k=0, seg0=0, g=norm_g[l, 0])
        kl, vl, dkl, dvl, ql, dql, ul = proj(zl, seg0=1, rope_tabs=rope_tabs)
        if last:
            kc, vc, dkc, dvc = proj(zc, seg0=0, kv_only=True)
        else:
            kc, vc, dkc, dvc, qc, dqc, uc = proj(zc, seg0=0)
        zl = _out_proj(zl, mod, l, 1, _mla_attention(ql, kc, vc, kl, vl), diff_attn(dql, dkc, dvc, dkl, dvl),
                       _fourier(ul, SEQ), w_o)
        if last:
            zl = ffn(zl, k=1, seg0=1, g=norm_g[l, 2], final_g=final_norm)
        else:
            zc = _out_proj(zc, mod, l, 0, _mla_attention(qc, kc, vc), diff_attn(dqc, dkc, dvc),
                           _fourier(uc, CTX_LEN), w_o)
            zl = ffn(zl, k=1, seg0=1, g=norm_g[l, 2])
            zc = ffn(zc, k=1, seg0=0, g=norm_g[l, 2])
    return zl.reshape(BATCH, SEQ, D_MODEL)
```

```python
import functools
import math

import numpy as np
import jax
import jax.numpy as jnp
from jax import lax
from jax.experimental import pallas as pl
from jax.experimental.pallas import tpu as pltpu

D_MODEL = 2048
BATCH = 8
SEQ = 2048
DEPTH = 2
GRID_W = 64
CTX_LEN = 256
EPS = 1e-6
ROPE_THETA = 10000.0
D_FF = 5632
N_MOD = 9

MLA_HEADS = 8
MLA_Q_RANK = 512
MLA_KV_RANK = 256
MLA_NOPE = 128
MLA_ROPE = 64
MLA_V = 128
MLA_SCALE = (MLA_NOPE + MLA_ROPE) ** -0.5
MLA_QK_PAD = 256
MLA_V_PAD = 256

DIFF_HEADS = 4
DIFF_QK = 64
DIFF_V = 2 * DIFF_QK
DIFF_SCALE = DIFF_QK ** -0.5

FOURIER_GROUPS = 4
FOURIER_CH = 128
MLA_WIDTH = MLA_HEADS * MLA_V
DIFF_WIDTH = DIFF_HEADS * DIFF_V
FOURIER_WIDTH = FOURIER_GROUPS * FOURIER_CH

SEG = SEQ
LAT_ROWS = BATCH * SEQ
CTX_ROWS = BATCH * CTX_LEN
MOD_ROWS = 16

IN_KROPE = MLA_KV_RANK
IN_REST = MLA_KV_RANK + MLA_ROPE
ROPE_PAD = 128
R_DK = 0
R_DV = 512
R_CQ = 1024
R_DQ = 1536
R_U = 2048
R_COLS = 2560

VMEM_LIMIT = 56 * 1024 * 1024
VMEM_LIMIT_FFN = 63 * 1024 * 1024 + 512 * 1024
LOG2_E = math.log2(math.e)

F32 = jnp.float32
BF16 = jnp.bfloat16


def _silu(v):
    return v / (1.0 + jnp.exp(-v))


def _dot(a, b):
    return jnp.dot(a, b, preferred_element_type=F32)


def _dot_nt(a, b):
    return lax.dot_general(a, b, (((1,), (1,)), ((), ())), preferred_element_type=F32)


def _rms(v):
    return v * lax.rsqrt(jnp.mean(v * v, axis=-1, keepdims=True) + EPS)


def _row_chunks(n_rows, chunk):
    return [slice(r, r + chunk) for r in range(0, n_rows, chunk)]


def _modulate_rows(z_ref, g_ref, mod_ref, h_ref):
    gain = g_ref[...] * (1.0 + mod_ref[1:2, :])
    shift = mod_ref[0:1, :]
    for rows in _row_chunks(z_ref.shape[0], 16):
        h_ref[rows, :] = (_rms(z_ref[rows, :]) * gain + shift).astype(BF16)


def _params(*sem, vmem=VMEM_LIMIT):
    return pltpu.CompilerParams(dimension_semantics=sem, vmem_limit_bytes=vmem)


def _resident(shape, index_map):
    return pl.BlockSpec(shape, index_map, pipeline_mode=pl.Buffered(1))


def _mod_spec(l, seg0, tile, grp, n_grid):
    if n_grid == 1:
        return pl.BlockSpec((None, None, None, 3, D_MODEL), lambda i: (l, seg0 + i * tile // SEG, grp, 0, 0))
    return pl.BlockSpec((None, None, None, 3, D_MODEL), lambda i, f: (l, seg0 + i * tile // SEG, grp, 0, 0))


def _ada_kernel(s_ref, w_ref, b_ref, o_ref):
    s = _silu(s_ref[...])
    o_ref[...] = _dot(s.astype(BF16), w_ref[...].astype(BF16)) + b_ref[...]


def _ada_table(s_in, ada_w, ada_b):
    tn = 1024
    n_cols = N_MOD * D_MODEL
    return pl.pallas_call(
        _ada_kernel,
        out_shape=jax.ShapeDtypeStruct((DEPTH, MOD_ROWS, n_cols), F32),
        grid=(DEPTH, n_cols // tn),
        in_specs=[
            pl.BlockSpec((MOD_ROWS, D_MODEL), lambda l, n: (0, 0)),
            pl.BlockSpec((None, D_MODEL, tn), lambda l, n: (l, 0, n)),
            pl.BlockSpec((None, 1, tn), lambda l, n: (l, 0, n)),
        ],
        out_specs=pl.BlockSpec((None, MOD_ROWS, tn), lambda l, n: (l, 0, n)),
        compiler_params=_params("parallel", "parallel"),
        name="ada_table",
    )(s_in, ada_w, ada_b.reshape(DEPTH, 1, n_cols))


DOWN_COLS = 512


def _ffn_kernel(z_ref, mod_ref, g_ref, wg_ref, wu_ref, wd_ref, *rest, n_f, final):
    if final:
        fn_ref, o_ref, h_scr = rest
    else:
        o_ref, h_scr = rest
    f = pl.program_id(1)

    @pl.when(f == 0)
    def _():
        _modulate_rows(z_ref, g_ref, mod_ref, h_scr)
        o_ref[...] = jnp.zeros_like(o_ref)

    h = h_scr[...]
    a = (_silu(_dot(h, wg_ref[...])) * _dot(h, wu_ref[...])).astype(BF16)
    for c in range(0, D_MODEL, DOWN_COLS):
        o_ref[:, c:c + DOWN_COLS] += _dot(a, wd_ref[:, c:c + DOWN_COLS])

    @pl.when(f == n_f - 1)
    def _():
        half_gate = 0.5 * mod_ref[2:3, :]
        for rows in _row_chunks(o_ref.shape[0], 8):
            y = z_ref[rows, :] + half_gate * o_ref[rows, :]
            if final:
                y = _rms(y) * fn_ref[...]
            o_ref[rows, :] = y


def _ffn(z, mod, l, k, seg0, g, wg, wu, wd, *, final_g=None, tm=1024, tf=512):
    n_rows = z.shape[0]
    n_f = D_FF // tf
    final = final_g is not None
    in_specs = [
        pl.BlockSpec((tm, D_MODEL), lambda i, f: (i, 0)),
        _mod_spec(l, seg0, tm, 2 * k, 2),
        pl.BlockSpec((1, D_MODEL), lambda i, f: (0, 0)),
        pl.BlockSpec((D_MODEL, tf), lambda i, f: (0, f)),
        pl.BlockSpec((D_MODEL, tf), lambda i, f: (0, f)),
        pl.BlockSpec((tf, D_MODEL), lambda i, f: (f, 0)),
    ]
    args = [z, mod, g.reshape(1, D_MODEL), wg, wu, wd]
    if final:
        in_specs.append(pl.BlockSpec((1, D_MODEL), lambda i, f: (0, 0)))
        args.append(final_g.reshape(1, D_MODEL))
    return pl.pallas_call(
        functools.partial(_ffn_kernel, n_f=n_f, final=final),
        out_shape=jax.ShapeDtypeStruct((n_rows, D_MODEL), F32),
        grid=(n_rows // tm, n_f),
        in_specs=in_specs,
        out_specs=pl.BlockSpec((tm, D_MODEL), lambda i, f: (i, 0)),
        scratch_shapes=[pltpu.VMEM((tm, D_MODEL), BF16)],
        compiler_params=_params("parallel", "arbitrary", vmem=VMEM_LIMIT_FFN),
        name="ffn_final" if final else "ffn",
    )(*args)


def _rope(v, cos, sin, lo_half):
    swapped = jnp.where(lo_half, pltpu.roll(v, 112, 1), pltpu.roll(v, 16, 1))
    return v * cos + swapped * sin


def _proj_kernel(*refs, rope, kv_only):
    refs = list(refs)
    x_ref, mod_ref, g_ref, wckv_ref, wkr_ref, wrest_ref, kvn_ref, wukv_ref = refs[:8]
    del refs[:8]
    if not kv_only:
        qn_ref, wuq_ref = refs[:2]
        del refs[:2]
    if rope:
        cos_ref, sin_ref = refs[:2]
        del refs[:2]
    h_scr = refs.pop()
    if kv_only:
        kmla_ref, vmla_ref, dk_ref, dv_ref = refs
    else:
        kmla_ref, vmla_ref, dk_ref, dv_ref, qmla_ref, dq_ref, u_ref = refs

    _modulate_rows(x_ref, g_ref, mod_ref, h_scr)
    hb = h_scr[...]
    if rope:
        lane = lax.broadcasted_iota(jnp.int32, (1, 128), 1)
        rotate = functools.partial(_rope, cos=cos_ref[...], sin=sin_ref[...], lo_half=(lane % 32) < 16)
    else:
        rotate = lambda v: v
    q_scale = MLA_SCALE * LOG2_E
    dq_scale = DIFF_SCALE * LOG2_E

    ckv = _rms(_dot(hb, wckv_ref[...])) * kvn_ref[...]
    kv = _dot(ckv.astype(BF16), wukv_ref[...])
    k_rope = rotate(_dot(hb, wkr_ref[...])).astype(BF16)
    for h in range(MLA_HEADS):
        c0 = h * (MLA_NOPE + MLA_V)
        kmla_ref[:, h * MLA_QK_PAD:h * MLA_QK_PAD + MLA_NOPE] = kv[:, c0:c0 + MLA_NOPE].astype(BF16)
        kmla_ref[:, h * MLA_QK_PAD + MLA_NOPE:(h + 1) * MLA_QK_PAD] = k_rope
        vmla_ref[:, h * MLA_V_PAD:h * MLA_V_PAD + MLA_V] = kv[:, c0 + MLA_NOPE:c0 + MLA_NOPE + MLA_V].astype(BF16)
        vmla_ref[:, h * MLA_V_PAD + MLA_V:(h + 1) * MLA_V_PAD] = jnp.ones((kv.shape[0], MLA_V_PAD - MLA_V), BF16)
    dk = _dot(hb, wrest_ref[:, R_DK:R_DV])
    for j in range(DIFF_HEADS):
        dk_ref[:, j * 128:(j + 1) * 128] = rotate(dk[:, j * 128:(j + 1) * 128]).astype(BF16)
    dv_ref[...] = _dot(hb, wrest_ref[:, R_DV:R_CQ]).astype(BF16)
    if kv_only:
        return

    cq = _rms(_dot(hb, wrest_ref[:, R_CQ:R_DQ])) * qn_ref[...]
    q = _dot(cq.astype(BF16), wuq_ref[...])
    for h in range(MLA_HEADS):
        c0 = h * MLA_QK_PAD
        qmla_ref[:, c0:c0 + 128] = (q[:, c0:c0 + 128] * q_scale).astype(BF16)
        qmla_ref[:, c0 + 128:c0 + 256] = (rotate(q[:, c0 + 128:c0 + 256]) * q_scale).astype(BF16)
    dq = _dot(hb, wrest_ref[:, R_DQ:R_U])
    for j in range(DIFF_HEADS):
        dq_ref[:, j * 128:(j + 1) * 128] = (rotate(dq[:, j * 128:(j + 1) * 128]) * dq_scale).astype(BF16)
    u_ref[...] = _dot(hb, wrest_ref[:, R_U:R_COLS]).astype(BF16)


def _proj(x, mod, l, seg0, g, w_in, kv_norm, w_ukv, q_norm, w_uq, rope_tabs=None, *, kv_only=False, tr=512):
    n_rows = x.shape[0]
    row = lambda w: pl.BlockSpec((tr, w), lambda i: (i, 0))
    vec = lambda w: _resident((None, 1, w), lambda i: (l, 0, 0))
    mat = lambda r, c: _resident((None, r, c), lambda i: (l, 0, 0))
    in_specs = [row(D_MODEL), _mod_spec(l, seg0, tr, 1, 1), _resident((1, D_MODEL), lambda i: (0, 0)),
                mat(D_MODEL, MLA_KV_RANK), mat(D_MODEL, ROPE_PAD), mat(D_MODEL, R_COLS),
                vec(MLA_KV_RANK), mat(MLA_KV_RANK, MLA_HEADS * (MLA_NOPE + MLA_V))]
    args = [x, mod, g.reshape(1, D_MODEL), *w_in, kv_norm.reshape(DEPTH, 1, -1), w_ukv]
    widths = [MLA_HEADS * MLA_QK_PAD, MLA_HEADS * MLA_V_PAD, DIFF_WIDTH, DIFF_WIDTH]
    if not kv_only:
        in_specs += [vec(MLA_Q_RANK), mat(MLA_Q_RANK, MLA_HEADS * MLA_QK_PAD)]
        args += [q_norm.reshape(DEPTH, 1, -1), w_uq]
        widths += [MLA_HEADS * MLA_QK_PAD, DIFF_WIDTH, FOURIER_WIDTH]
    if rope_tabs is not None:
        per_seg = SEG // tr
        in_specs += [pl.BlockSpec((tr, 128), lambda i: (i % per_seg, 0))] * 2
        args += list(rope_tabs)
    return pl.pallas_call(
        functools.partial(_proj_kernel, rope=rope_tabs is not None, kv_only=kv_only),
        out_shape=[jax.ShapeDtypeStruct((n_rows, w), BF16) for w in widths],
        grid=(n_rows // tr,),
        in_specs=in_specs,
        out_specs=[row(w) for w in widths],
        scratch_shapes=[pltpu.VMEM((tr, D_MODEL), BF16)],
        compiler_params=_params("parallel"),
        name="in_proj",
    )(*args)


SUB = 256
HALF = 8


def _attn_refs(refs, lat):
    if not lat:
        q_ref, kc_ref, vc_ref, o_ref = refs
        return q_ref, kc_ref, vc_ref, None, None, o_ref
    n_cast = (len(refs) - 6) // 2
    for src, dst in zip(refs[5:5 + n_cast], refs[6 + n_cast:]):
        dst[...] = src[...].astype(BF16)
    return (*refs[:5], refs[5 + n_cast])


def _cast_rows(n_rows, n_steps):
    return min(r for r in range(16, n_rows + 1, 16) if n_rows % r == 0 and n_rows // r <= n_steps)


def _one_ahead(items, scores, finish):
    scores(0, *items[0])
    for i, item in enumerate(items):
        if i + 1 < len(items):
            scores((i + 1) % 2, *items[i + 1])
        finish(i % 2, *item)


def _store_scores(s_scr, slot, q, kc, kl):
    s_scr[slot, :, :CTX_LEN] = _dot_nt(q, kc)
    if kl is not None:
        s_scr[slot, :, CTX_LEN:] = _dot_nt(q, kl)


def _weighted_values(p_scr, slot, vc, vl):
    o = _dot(p_scr[slot, :, :CTX_LEN], vc)
    if vl is not None:
        o = o + _dot(p_scr[slot, :, CTX_LEN:], vl)
    return o


def _chunk_numerators(s):
    return jnp.exp2(s - jnp.max(s, axis=-1, keepdims=True))


def _mla_kernel(*refs, heads, n_sub, lat):
    *io, s_scr, p_scr = refs
    q_ref, kc_ref, vc_ref, kl_ref, vl_ref, o_ref = _attn_refs(io, lat)

    def scores(slot, hh, j):
        qk = slice(hh * MLA_QK_PAD, (hh + 1) * MLA_QK_PAD)
        _store_scores(s_scr, slot, q_ref[j * SUB:(j + 1) * SUB, qk], kc_ref[:, qk], kl_ref[:, qk] if lat else None)

    def finish(slot, hh, j):
        for r in range(0, SUB, 2 * HALF):
            e = [_chunk_numerators(s_scr[slot, r0:r0 + HALF, :]) for r0 in (r, r + HALF)]
            p_scr[slot, r:r + 2 * HALF, :] = jnp.concatenate(e, axis=0).astype(BF16)
        vv = slice(hh * MLA_V_PAD, (hh + 1) * MLA_V_PAD)
        o = _weighted_values(p_scr, slot, vc_ref[:, vv], vl_ref[:, vv] if lat else None)
        o_ref[j * SUB:(j + 1) * SUB, hh * MLA_V:(hh + 1) * MLA_V] = (o[:, :MLA_V] / o[:, MLA_V:]).astype(BF16)

    _one_ahead([(hh, j) for hh in range(heads) for j in range(n_sub)], scores, finish)


def _mla_scratch(n_keys):
    return [pltpu.VMEM((2, SUB, n_keys), F32), pltpu.VMEM((2, SUB, n_keys), BF16)]


def _diff_kernel(lam_ref, sub_ref, *refs, heads, n_sub, lat, lam_init):
    *io, s1_scr, s2_scr, p_scr, d_scr = refs
    q_ref, kc_ref, vc_ref, kl_ref, vl_ref, o_ref = _attn_refs(io, lat)
    lf = lam_ref[...]
    lam = (jnp.exp(jnp.sum(lf[0:1] * lf[1:2], axis=-1, keepdims=True))
           - jnp.exp(jnp.sum(lf[2:3] * lf[3:4], axis=-1, keepdims=True)) + lam_init)
    first = lax.broadcasted_iota(jnp.int32, (1, 2 * DIFF_QK), 1) < DIFF_QK

    def scores(slot, hh, j):
        cols = slice(hh * DIFF_V, (hh + 1) * DIFF_V)
        q = q_ref[j * SUB:(j + 1) * SUB, cols]
        kc = kc_ref[:, cols]
        kl = kl_ref[:, cols] if lat else None
        _store_scores(s1_scr, slot, jnp.where(first, q, jnp.zeros_like(q)), kc, kl)
        _store_scores(s2_scr, slot, jnp.where(first, jnp.zeros_like(q), q), kc, kl)

    def finish(slot, hh, j):
        for r in range(0, SUB, 2 * HALF):
            p = []
            for r0 in (r, r + HALF):
                e1 = _chunk_numerators(s1_scr[slot, r0:r0 + HALF, :])
                e2 = _chunk_numerators(s2_scr[slot, r0:r0 + HALF, :])
                d1 = jnp.sum(e1, axis=-1, keepdims=True)
                d2 = jnp.sum(e2, axis=-1, keepdims=True)
                p.append(e1 - e2 * (lam * d1 / d2))
                d_scr[slot, r0:r0 + HALF, :] = jnp.broadcast_to(d1, (HALF, DIFF_V))
            p_scr[slot, r:r + 2 * HALF, :] = jnp.concatenate(p, axis=0).astype(BF16)
        cols = slice(hh * DIFF_V, (hh + 1) * DIFF_V)
        o = _weighted_values(p_scr, slot, vc_ref[:, cols], vl_ref[:, cols] if lat else None) / d_scr[slot]
        o_ref[j * SUB:(j + 1) * SUB, cols] = (_rms(o) * sub_ref[...] * (1.0 - lam_init)).astype(BF16)

    _one_ahead([(hh, j) for hh in range(heads) for j in range(n_sub)], scores, finish)


def _diff_scratch(n_keys):
    return [pltpu.VMEM((2, SUB, n_keys), F32), pltpu.VMEM((2, SUB, n_keys), F32),
            pltpu.VMEM((2, SUB, n_keys), BF16), pltpu.VMEM((2, SUB, DIFF_V), F32)]


def _attention(kernel_fn, scratch_fn, name, q, k_ctx, v_ctx, k_lat=None, v_lat=None, *, n_heads, w_qk, w_v, w_o,
               tq, extra_args=(), extra_specs=(), cast=()):
    if k_lat is not None:
        per_seg = SEQ // tq
        n_steps = BATCH * n_heads * per_seg
        in_specs = list(extra_specs) + [
            pl.BlockSpec((tq, w_qk), lambda b, h, t: (b * per_seg + t, h)),
            pl.BlockSpec((CTX_LEN, w_qk), lambda b, h, t: (b, h)),
            pl.BlockSpec((CTX_LEN, w_v), lambda b, h, t: (b, h)),
            pl.BlockSpec((SEQ, w_qk), lambda b, h, t: (b, h)),
            pl.BlockSpec((SEQ, w_v), lambda b, h, t: (b, h)),
        ]
        out_shape = [jax.ShapeDtypeStruct((LAT_ROWS, n_heads * w_o), BF16)]
        out_specs = [pl.BlockSpec((tq, w_o), lambda b, h, t: (b * per_seg + t, h))]
        for w, l, k in cast:
            n_r, n_c = w.shape[2:]
            rows = _cast_rows(n_r, n_steps)
            blk = lambda b, h, t, last=n_r // rows - 1: jnp.minimum((b * n_heads + h) * per_seg + t, last)
            in_specs.append(pl.BlockSpec((None, None, rows, n_c), lambda b, h, t, l=l, k=k, blk=blk: (l, k, blk(b, h, t), 0)))
            out_shape.append(jax.ShapeDtypeStruct((n_r, n_c), BF16))
            out_specs.append(pl.BlockSpec((rows, n_c), lambda b, h, t, blk=blk: (blk(b, h, t), 0)))
        out = pl.pallas_call(
            functools.partial(kernel_fn, heads=1, n_sub=tq // SUB, lat=True),
            out_shape=out_shape,
            grid=(BATCH, n_heads, per_seg),
            in_specs=in_specs,
            out_specs=out_specs,
            scratch_shapes=scratch_fn(CTX_LEN + SEQ),
            compiler_params=_params(*(("arbitrary",) * 3 if cast else ("parallel", "parallel", "arbitrary"))),
            name=name,
        )(*extra_args, q, k_ctx, v_ctx, k_lat, v_lat, *[w for w, _, _ in cast])
        return out if cast else out[0]
    in_specs = list(extra_specs) + [
        pl.BlockSpec((CTX_LEN, n_heads * w_qk), lambda b: (b, 0)),
        pl.BlockSpec((CTX_LEN, n_heads * w_qk), lambda b: (b, 0)),
        pl.BlockSpec((CTX_LEN, n_heads * w_v), lambda b: (b, 0)),
    ]
    return pl.pallas_call(
        functools.partial(kernel_fn, heads=n_heads, n_sub=CTX_LEN // SUB, lat=False),
        out_shape=jax.ShapeDtypeStruct((CTX_ROWS, n_heads * w_o), BF16),
        grid=(BATCH,),
        in_specs=in_specs,
        out_specs=pl.BlockSpec((CTX_LEN, n_heads * w_o), lambda b: (b, 0)),
        scratch_shapes=scratch_fn(CTX_LEN),
        compiler_params=_params("parallel"),
        name=name + "_ctx",
    )(*extra_args, q, k_ctx, v_ctx)


def _mla_attention(q, k_ctx, v_ctx, k_lat=None, v_lat=None, *, tq=1024, cast=()):
    return _attention(_mla_kernel, _mla_scratch, "mla_attention", q, k_ctx, v_ctx, k_lat, v_lat,
                      n_heads=MLA_HEADS, w_qk=MLA_QK_PAD, w_v=MLA_V_PAD, w_o=MLA_V, tq=tq, cast=cast)


def _diff_attention(diff_lambda, subln, l, lam_init, q, k_ctx, v_ctx, k_lat=None, v_lat=None, *, tq=1024, cast=()):
    layer = lambda *_: (l, 0, 0)
    extra_specs = (pl.BlockSpec((None, 4, DIFF_QK), layer), pl.BlockSpec((None, 1, DIFF_V), layer))
    return _attention(functools.partial(_diff_kernel, lam_init=lam_init), _diff_scratch, "diff_attention",
                      q, k_ctx, v_ctx, k_lat, v_lat,
                      n_heads=DIFF_HEADS, w_qk=2 * DIFF_QK, w_v=DIFF_V, w_o=DIFF_V, tq=tq,
                      extra_args=(diff_lambda, subln.reshape(DEPTH, 1, DIFF_V)), extra_specs=extra_specs, cast=cast)


def _dft_tables(n_pos):
    k = np.arange(n_pos, dtype=np.int64)
    ang = 2.0 * np.pi * ((k[:, None] * k[None, :]) % n_pos) / n_pos
    pos = np.concatenate([np.cos(ang), -np.sin(ang)], axis=1) / math.sqrt(n_pos)
    c = np.arange(FOURIER_CH, dtype=np.int64)
    angc = 2.0 * np.pi * ((c[:, None] * c[None, :]) % FOURIER_CH) / FOURIER_CH
    ch = np.concatenate([np.cos(angc), np.sin(angc)], axis=1) / math.sqrt(FOURIER_CH)
    return jnp.asarray(pos, dtype=BF16), jnp.asarray(ch, dtype=BF16)


def _fourier_kernel(u_ref, ch_ref, pos_ref, o_ref, rhs_scr, *, n_pos):
    @pl.when(pl.program_id(1) == 0)
    def _():
        for g in range(FOURIER_GROUPS):
            a = _dot(u_ref[:, g * FOURIER_CH:(g + 1) * FOURIER_CH], ch_ref[...])
            rhs_scr[0:n_pos, g * FOURIER_CH:(g + 1) * FOURIER_CH] = a[:, :FOURIER_CH].astype(BF16)
            rhs_scr[n_pos:2 * n_pos, g * FOURIER_CH:(g + 1) * FOURIER_CH] = a[:, FOURIER_CH:].astype(BF16)

    o_ref[...] = _dot(pos_ref[...], rhs_scr[...]).astype(BF16)


def _fourier(u, n_pos, *, tr=256):
    pos_t, ch_t = _dft_tables(n_pos)
    n_j = n_pos // tr
    return pl.pallas_call(
        functools.partial(_fourier_kernel, n_pos=n_pos),
        out_shape=jax.ShapeDtypeStruct((BATCH * n_pos, FOURIER_WIDTH), BF16),
        grid=(BATCH, n_j),
        in_specs=[
            pl.BlockSpec((n_pos, FOURIER_WIDTH), lambda b, j: (b, 0)),
            pl.BlockSpec((FOURIER_CH, 2 * FOURIER_CH), lambda b, j: (0, 0)),
            pl.BlockSpec((tr, 2 * n_pos), lambda b, j: (j, 0)),
        ],
        out_specs=pl.BlockSpec((tr, FOURIER_WIDTH), lambda b, j: (b * n_j + j, 0)),
        scratch_shapes=[pltpu.VMEM((2 * n_pos, FOURIER_WIDTH), BF16)],
        compiler_params=_params("parallel", "arbitrary"),
        name="fourier_mix",
    )(u, ch_t, pos_t)


def _out_kernel(x_ref, mod_ref, om_ref, od_ref, of_ref, w_ref, o_ref):
    y = (_dot(om_ref[...], w_ref[0:MLA_WIDTH, :])
         + _dot(od_ref[...], w_ref[MLA_WIDTH:MLA_WIDTH + DIFF_WIDTH, :])
         + _dot(of_ref[...], w_ref[MLA_WIDTH + DIFF_WIDTH:, :]))
    o_ref[...] = x_ref[...] + mod_ref[2:3, :] * y


def _out_proj(x, mod, l, seg0, o_mla, o_diff, o_four, w_out, *, tr=512):
    n_rows = x.shape[0]
    row = lambda w: pl.BlockSpec((tr, w), lambda i: (i, 0))
    return pl.pallas_call(
        _out_kernel,
        out_shape=jax.ShapeDtypeStruct((n_rows, D_MODEL), F32),
        grid=(n_rows // tr,),
        in_specs=[
            row(D_MODEL), _mod_spec(l, seg0, tr, 1, 1),
            row(MLA_WIDTH), row(DIFF_WIDTH), row(FOURIER_WIDTH),
            _resident((None, D_MODEL, D_MODEL), lambda i: (l, 0, 0)),
        ],
        out_specs=row(D_MODEL),
        compiler_params=_params("parallel"),
        name="out_proj",
    )(x, mod, o_mla, o_diff, o_four, w_out)


def _rope_tables():
    rows = SEQ // GRID_W
    pos_r = jnp.repeat(jnp.arange(rows), GRID_W)
    pos_c = jnp.tile(jnp.arange(GRID_W), rows)
    d = MLA_ROPE // 2
    half = d // 2
    inv = ROPE_THETA ** (-2.0 * jnp.arange(half, dtype=F32) / d)

    def tabs(pos):
        ang = pos.astype(F32)[:, None] * inv[None, :]
        return jnp.cos(ang), jnp.sin(ang)

    cr, sr = tabs(pos_r)
    cc, sc = tabs(pos_c)
    cos64 = jnp.concatenate([cr, cr, cc, cc], axis=-1)
    sin64 = jnp.concatenate([-sr, sr, -sc, sc], axis=-1)
    return jnp.tile(cos64, (1, 2)), jnp.tile(sin64, (1, 2))


def kernel(x, c, ctx, c_ctx, ada_w, ada_b, norm_g, ffn_wg, ffn_wu, ffn_wd, w_in, mla_q_norm, mla_kv_norm,
           mla_w_uq, mla_w_ukv, diff_lambda, diff_subln, w_out, final_norm):
    assert DEPTH == 2
    s_in = jnp.concatenate([c_ctx[None, :], c, jnp.zeros((MOD_ROWS - 1 - BATCH, D_MODEL), F32)], axis=0)
    w_in_p = (w_in[..., :IN_KROPE].astype(BF16),
              jnp.pad(w_in[..., IN_KROPE:IN_REST].astype(BF16), ((0, 0), (0, 0), (0, ROPE_PAD - MLA_ROPE))),
              w_in[..., IN_REST:].astype(BF16))
    w_uq_p = jnp.pad(mla_w_uq.astype(BF16).reshape(DEPTH, MLA_Q_RANK, MLA_HEADS, MLA_NOPE + MLA_ROPE),
                     ((0, 0), (0, 0), (0, 0), (0, MLA_QK_PAD - MLA_NOPE - MLA_ROPE))
                     ).reshape(DEPTH, MLA_Q_RANK, MLA_HEADS * MLA_QK_PAD)
    w_ukv = mla_w_ukv.astype(BF16)
    w_o = w_out.astype(BF16)
    rope_tabs = _rope_tables()
    ffn_w32 = (ffn_wg, ffn_wu, ffn_wd)
    ffn_w = {(0, 0): tuple(w[0, 0].astype(BF16) for w in ffn_w32)}

    mod =_ada_table(s_in, ada_w, ada_b).reshape(DEPTH, MOD_ROWS, 3, 3, D_MODEL)

    zl = x.reshape(LAT_ROWS, D_MODEL)
    zc = ctx.reshape(CTX_ROWS, D_MODEL)
    for l in range(DEPTH):
        last = l == DEPTH - 1
        lam_init = 0.8 - 0.6 * math.exp(-0.3 * l)
        proj = functools.partial(_proj, mod=mod, l=l, g=norm_g[l, 1], w_in=w_in_p, kv_norm=mla_kv_norm,
                                 w_ukv=w_ukv, q_norm=mla_q_norm, w_uq=w_uq_p)
        diff_attn = functools.partial(_diff_attention, diff_lambda, diff_subln, l, lam_init)

        def ffn(z, k, seg0, **kw):
            wg, wu, wd = ffn_w[l, k]
            return _ffn(z, mod, l, k, seg0, norm_g[l, 2 * k], wg, wu, wd, **kw)

        zl = ffn(zl, 0, 1)
        zc = ffn(zc, 0, 0)
        kl, vl, dkl, dvl, ql, dql, ul = proj(zl, seg0=1, rope_tabs=rope_tabs)
        if last:
            kc, vc, dkc, dvc = proj(zc, seg0=0, kv_only=True)
            o_mla = _mla_attention(ql, kc, vc, kl, vl)
            o_diff = diff_attn(dql, dkc, dvc, dkl, dvl)
        else:
            kc, vc, dkc, dvc, qc, dqc, uc = proj(zc, seg0=0)
            o_mla, *w_a = _mla_attention(ql, kc, vc, kl, vl,
                                         cast=[(w, 0, 1) for w in ffn_w32] + [(w, 1, 0) for w in ffn_w32])
            o_diff, *w_b = diff_attn(dql, dkc, dvc, dkl, dvl, cast=[(w, 1, 1) for w in ffn_w32])
            ffn_w.update({(0, 1): tuple(w_a[:3]), (1, 0): tuple(w_a[3:]), (1, 1): tuple(w_b)})
        zl = _out_proj(zl, mod, l, 1, o_mla, o_diff, _fourier(ul, SEQ), w_o)
        if last:
            zl = ffn(zl, 1, 1, final_g=final_norm)
        else:
            zc = _out_proj(zc, mod, l, 0, _mla_attention(qc, kc, vc), diff_attn(dqc, dkc, dvc),
                           _fourier(uc, CTX_LEN), w_o)
            zl = ffn(zl, 1, 1)
            zc = ffn(zc, 1, 0)
    return zl.reshape(BATCH, SEQ, D_MODEL)
```

```python
import functools
import math

import numpy as np
import jax
import jax.numpy as jnp
from jax import lax
from jax.experimental import pallas as pl
from jax.experimental.pallas import tpu as pltpu

D_MODEL = 2048
BATCH = 8
SEQ = 2048
DEPTH = 2
GRID_W = 64
CTX_LEN = 256
EPS = 1e-6
ROPE_THETA = 10000.0
D_FF = 5632
N_MOD = 9

MLA_HEADS = 8
MLA_Q_RANK = 512
MLA_KV_RANK = 256
MLA_NOPE = 128
MLA_ROPE = 64
MLA_V = 128
MLA_SCALE = (MLA_NOPE + MLA_ROPE) ** -0.5
MLA_QK_PAD = 256
MLA_V_PAD = 256

DIFF_HEADS = 4
DIFF_QK = 64
DIFF_V = 2 * DIFF_QK
DIFF_SCALE = DIFF_QK ** -0.5

FOURIER_GROUPS = 4
FOURIER_CH = 128
MLA_WIDTH = MLA_HEADS * MLA_V
DIFF_WIDTH = DIFF_HEADS * DIFF_V
FOURIER_WIDTH = FOURIER_GROUPS * FOURIER_CH

SEG = SEQ
LAT_ROWS = BATCH * SEQ
CTX_ROWS = BATCH * CTX_LEN
MOD_ROWS = 16

IN_KROPE = MLA_KV_RANK
IN_REST = MLA_KV_RANK + MLA_ROPE
ROPE_PAD = 128
R_DK = 0
R_DV = 512
R_CQ = 1024
R_DQ = 1536
R_U = 2048
R_COLS = 2560

VMEM_LIMIT = 56 * 1024 * 1024
VMEM_LIMIT_FFN = 63 * 1024 * 1024 + 512 * 1024
LOG2_E = math.log2(math.e)

F32 = jnp.float32
BF16 = jnp.bfloat16


def _silu(v):
    return v / (1.0 + jnp.exp(-v))


def _dot(a, b):
    return jnp.dot(a, b, preferred_element_type=F32)


def _dot_nt(a, b):
    return lax.dot_general(a, b, (((1,), (1,)), ((), ())), preferred_element_type=F32)


def _rms(v):
    return v * lax.rsqrt(jnp.mean(v * v, axis=-1, keepdims=True) + EPS)


def _row_chunks(n_rows, chunk):
    return [slice(r, r + chunk) for r in range(0, n_rows, chunk)]


def _modulate_rows(z_ref, g_ref, mod_ref, h_ref):
    gain = g_ref[...] * (1.0 + mod_ref[1:2, :])
    shift = mod_ref[0:1, :]
    for rows in _row_chunks(z_ref.shape[0], 16):
        h_ref[rows, :] = (_rms(z_ref[rows, :]) * gain + shift).astype(BF16)


def _params(*sem, vmem=VMEM_LIMIT):
    return pltpu.CompilerParams(dimension_semantics=sem, vmem_limit_bytes=vmem)


def _resident(shape, index_map):
    return pl.BlockSpec(shape, index_map, pipeline_mode=pl.Buffered(1))


def _mod_spec(l, seg0, tile, grp, n_grid):
    if n_grid == 1:
        return pl.BlockSpec((None, None, None, 3, D_MODEL), lambda i: (l, seg0 + i * tile // SEG, grp, 0, 0))
    return pl.BlockSpec((None, None, None, 3, D_MODEL), lambda i, f: (l, seg0 + i * tile // SEG, grp, 0, 0))


def _ada_kernel(s_ref, w_ref, b_ref, o_ref):
    s = _silu(s_ref[...])
    o_ref[...] = _dot(s.astype(BF16), w_ref[...].astype(BF16)) + b_ref[...]


def _ada_table(s_in, ada_w, ada_b):
    tn = 1024
    n_cols = N_MOD * D_MODEL
    return pl.pallas_call(
        _ada_kernel,
        out_shape=jax.ShapeDtypeStruct((DEPTH, MOD_ROWS, n_cols), F32),
        grid=(DEPTH, n_cols // tn),
        in_specs=[
            pl.BlockSpec((MOD_ROWS, D_MODEL), lambda l, n: (0, 0)),
            pl.BlockSpec((None, D_MODEL, tn), lambda l, n: (l, 0, n)),
            pl.BlockSpec((None, 1, tn), lambda l, n: (l, 0, n)),
        ],
        out_specs=pl.BlockSpec((None, MOD_ROWS, tn), lambda l, n: (l, 0, n)),
        compiler_params=_params("parallel", "parallel"),
        name="ada_table",
    )(s_in, ada_w, ada_b.reshape(DEPTH, 1, n_cols))


DOWN_COLS = 512


def _ffn_kernel(z_ref, mod_ref, g_ref, wg_ref, wu_ref, wd_ref, *rest, n_f, final):
    if final:
        fn_ref, o_ref, h_scr = rest
    else:
        o_ref, h_scr = rest
    f = pl.program_id(1)

    @pl.when(f == 0)
    def _():
        _modulate_rows(z_ref, g_ref, mod_ref, h_scr)
        o_ref[...] = jnp.zeros_like(o_ref)

    h = h_scr[...]
    a = (_silu(_dot(h, wg_ref[...])) * _dot(h, wu_ref[...])).astype(BF16)
    for c in range(0, D_MODEL, DOWN_COLS):
        o_ref[:, c:c + DOWN_COLS] += _dot(a, wd_ref[:, c:c + DOWN_COLS])

    @pl.when(f == n_f - 1)
    def _():
        half_gate = 0.5 * mod_ref[2:3, :]
        for rows in _row_chunks(o_ref.shape[0], 8):
            y = z_ref[rows, :] + half_gate * o_ref[rows, :]
            if final:
                y = _rms(y) * fn_ref[...]
            o_ref[rows, :] = y


def _ffn(z, mod, l, k, seg0, g, wg, wu, wd, *, final_g=None, tm=1024, tf=512):
    n_rows = z.shape[0]
    n_f = D_FF // tf
    final = final_g is not None
    in_specs = [
        pl.BlockSpec((tm, D_MODEL), lambda i, f: (i, 0)),
        _mod_spec(l, seg0, tm, 2 * k, 2),
        pl.BlockSpec((1, D_MODEL), lambda i, f: (0, 0)),
        pl.BlockSpec((D_MODEL, tf), lambda i, f: (0, f)),
        pl.BlockSpec((D_MODEL, tf), lambda i, f: (0, f)),
        pl.BlockSpec((tf, D_MODEL), lambda i, f: (f, 0)),
    ]
    args = [z, mod, g.reshape(1, D_MODEL), wg, wu, wd]
    if final:
        in_specs.append(pl.BlockSpec((1, D_MODEL), lambda i, f: (0, 0)))
        args.append(final_g.reshape(1, D_MODEL))
    return pl.pallas_call(
        functools.partial(_ffn_kernel, n_f=n_f, final=final),
        out_shape=jax.ShapeDtypeStruct((n_rows, D_MODEL), F32),
        grid=(n_rows // tm, n_f),
        in_specs=in_specs,
        out_specs=pl.BlockSpec((tm, D_MODEL), lambda i, f: (i, 0)),
        scratch_shapes=[pltpu.VMEM((tm, D_MODEL), BF16)],
        compiler_params=_params("parallel", "arbitrary", vmem=VMEM_LIMIT_FFN),
        name="ffn_final" if final else "ffn",
    )(*args)


def _rope(v, cos, sin, lo_half):
    swapped = jnp.where(lo_half, pltpu.roll(v, 112, 1), pltpu.roll(v, 16, 1))
    return v * cos + swapped * sin


def _proj_kernel(*refs, rope, kv_only):
    refs = list(refs)
    x_ref, mod_ref, g_ref, wckv_ref, wkr_ref, wrest_ref, kvn_ref, wukv_ref = refs[:8]
    del refs[:8]
    if not kv_only:
        qn_ref, wuq_ref = refs[:2]
        del refs[:2]
    if rope:
        cos_ref, sin_ref = refs[:2]
        del refs[:2]
    h_scr = refs.pop()
    if kv_only:
        kmla_ref, vmla_ref, dk_ref, dv_ref = refs
    else:
        kmla_ref, vmla_ref, dk_ref, dv_ref, qmla_ref, dq_ref, u_ref = refs

    _modulate_rows(x_ref, g_ref, mod_ref, h_scr)
    hb = h_scr[...]
    if rope:
        lane = lax.broadcasted_iota(jnp.int32, (1, 128), 1)
        rotate = functools.partial(_rope, cos=cos_ref[...], sin=sin_ref[...], lo_half=(lane % 32) < 16)
    else:
        rotate = lambda v: v
    q_scale = MLA_SCALE * LOG2_E
    dq_scale = DIFF_SCALE * LOG2_E

    ckv = _rms(_dot(hb, wckv_ref[...])) * kvn_ref[...]
    kv = _dot(ckv.astype(BF16), wukv_ref[...])
    k_rope = rotate(_dot(hb, wkr_ref[...])).astype(BF16)
    for h in range(MLA_HEADS):
        c0 = h * (MLA_NOPE + MLA_V)
        kmla_ref[:, h * MLA_QK_PAD:h * MLA_QK_PAD + MLA_NOPE] = kv[:, c0:c0 + MLA_NOPE].astype(BF16)
        kmla_ref[:, h * MLA_QK_PAD + MLA_NOPE:(h + 1) * MLA_QK_PAD] = k_rope
        vmla_ref[:, h * MLA_V_PAD:h * MLA_V_PAD + MLA_V] = kv[:, c0 + MLA_NOPE:c0 + MLA_NOPE + MLA_V].astype(BF16)
        vmla_ref[:, h * MLA_V_PAD + MLA_V:(h + 1) * MLA_V_PAD] = jnp.ones((kv.shape[0], MLA_V_PAD - MLA_V), BF16)
    dk = _dot(hb, wrest_ref[:, R_DK:R_DV])
    for j in range(DIFF_HEADS):
        dk_ref[:, j * 128:(j + 1) * 128] = rotate(dk[:, j * 128:(j + 1) * 128]).astype(BF16)
    dv_ref[...] = _dot(hb, wrest_ref[:, R_DV:R_CQ]).astype(BF16)
    if kv_only:
        return

    cq = _rms(_dot(hb, wrest_ref[:, R_CQ:R_DQ])) * qn_ref[...]
    q = _dot(cq.astype(BF16), wuq_ref[...])
    for h in range(MLA_HEADS):
        c0 = h * MLA_QK_PAD
        qmla_ref[:, c0:c0 + 128] = (q[:, c0:c0 + 128] * q_scale).astype(BF16)
        qmla_ref[:, c0 + 128:c0 + 256] = (rotate(q[:, c0 + 128:c0 + 256]) * q_scale).astype(BF16)
    dq = _dot(hb, wrest_ref[:, R_DQ:R_U])
    for j in range(DIFF_HEADS):
        dq_ref[:, j * 128:(j + 1) * 128] = (rotate(dq[:, j * 128:(j + 1) * 128]) * dq_scale).astype(BF16)
    u_ref[...] = _dot(hb, wrest_ref[:, R_U:R_COLS]).astype(BF16)


def _proj(x, mod, l, seg0, g, w_in, kv_norm, w_ukv, q_norm, w_uq, rope_tabs=None, *, kv_only=False, tr=512):
    n_rows = x.shape[0]
    row = lambda w: pl.BlockSpec((tr, w), lambda i: (i, 0))
    vec = lambda w: _resident((None, 1, w), lambda i: (l, 0, 0))
    mat = lambda r, c: _resident((None, r, c), lambda i: (l, 0, 0))
    in_specs = [row(D_MODEL), _mod_spec(l, seg0, tr, 1, 1), _resident((1, D_MODEL), lambda i: (0, 0)),
                mat(D_MODEL, MLA_KV_RANK), mat(D_MODEL, ROPE_PAD), mat(D_MODEL, R_COLS),
                vec(MLA_KV_RANK), mat(MLA_KV_RANK, MLA_HEADS * (MLA_NOPE + MLA_V))]
    args = [x, mod, g.reshape(1, D_MODEL), *w_in, kv_norm.reshape(DEPTH, 1, -1), w_ukv]
    widths = [MLA_HEADS * MLA_QK_PAD, MLA_HEADS * MLA_V_PAD, DIFF_WIDTH, DIFF_WIDTH]
    if not kv_only:
        in_specs += [vec(MLA_Q_RANK), mat(MLA_Q_RANK, MLA_HEADS * MLA_QK_PAD)]
        args += [q_norm.reshape(DEPTH, 1, -1), w_uq]
        widths += [MLA_HEADS * MLA_QK_PAD, DIFF_WIDTH, FOURIER_WIDTH]
    if rope_tabs is not None:
        per_seg = SEG // tr
        in_specs += [pl.BlockSpec((tr, 128), lambda i: (i % per_seg, 0))] * 2
        args += list(rope_tabs)
    return pl.pallas_call(
        functools.partial(_proj_kernel, rope=rope_tabs is not None, kv_only=kv_only),
        out_shape=[jax.ShapeDtypeStruct((n_rows, w), BF16) for w in widths],
        grid=(n_rows // tr,),
        in_specs=in_specs,
        out_specs=[row(w) for w in widths],
        scratch_shapes=[pltpu.VMEM((tr, D_MODEL), BF16)],
        compiler_params=_params("parallel"),
        name="in_proj",
    )(*args)


SUB = 256
CHUNK_VREGS = 18


def _half_rows(n_keys):
    half = 8
    while 2 * half * n_keys <= CHUNK_VREGS * 1024 and 4 * half <= SUB:
        half *= 2
    return half


def _attn_refs(refs, lat):
    if not lat:
        q_ref, kc_ref, vc_ref, o_ref = refs
        return q_ref, kc_ref, vc_ref, None, None, o_ref
    n_cast = (len(refs) - 6) // 2
    for src, dst in zip(refs[5:5 + n_cast], refs[6 + n_cast:]):
        dst[...] = src[...].astype(BF16)
    return (*refs[:5], refs[5 + n_cast])


def _cast_rows(n_rows, n_steps):
    return min(r for r in range(16, n_rows + 1, 16) if n_rows % r == 0 and n_rows // r <= n_steps)


def _one_ahead(items, scores, finish):
    scores(0, *items[0])
    for i, item in enumerate(items):
        if i + 1 < len(items):
            scores((i + 1) % 2, *items[i + 1])
        finish(i % 2, *item)


def _store_scores(s_scr, slot, q, kc, kl):
    s_scr[slot, :, :CTX_LEN] = _dot_nt(q, kc)
    if kl is not None:
        s_scr[slot, :, CTX_LEN:] = _dot_nt(q, kl)


def _weighted_values(p_scr, slot, vc, vl):
    o = _dot(p_scr[slot, :, :CTX_LEN], vc)
    if vl is not None:
        o = o + _dot(p_scr[slot, :, CTX_LEN:], vl)
    return o


def _chunk_numerators(s):
    return jnp.exp2(s - jnp.max(s, axis=-1, keepdims=True))


def _mla_kernel(*refs, heads, n_sub, lat):
    *io, s_scr, p_scr = refs
    q_ref, kc_ref, vc_ref, kl_ref, vl_ref, o_ref = _attn_refs(io, lat)

    def scores(slot, hh, j):
        qk = slice(hh * MLA_QK_PAD, (hh + 1) * MLA_QK_PAD)
        _store_scores(s_scr, slot, q_ref[j * SUB:(j + 1) * SUB, qk], kc_ref[:, qk], kl_ref[:, qk] if lat else None)

    def finish(slot, hh, j):
        half = _half_rows(s_scr.shape[-1])
        for r in range(0, SUB, 2 * half):
            e = [_chunk_numerators(s_scr[slot, r0:r0 + half, :]) for r0 in (r, r + half)]
            p_scr[slot, r:r + 2 * half, :] = jnp.concatenate(e, axis=0).astype(BF16)
        vv = slice(hh * MLA_V_PAD, (hh + 1) * MLA_V_PAD)
        o = _weighted_values(p_scr, slot, vc_ref[:, vv], vl_ref[:, vv] if lat else None)
        o_ref[j * SUB:(j + 1) * SUB, hh * MLA_V:(hh + 1) * MLA_V] = (o[:, :MLA_V] / o[:, MLA_V:]).astype(BF16)

    _one_ahead([(hh, j) for hh in range(heads) for j in range(n_sub)], scores, finish)


def _mla_scratch(n_keys):
    return [pltpu.VMEM((2, SUB, n_keys), F32), pltpu.VMEM((2, SUB, n_keys), BF16)]


def _diff_kernel(lam_ref, sub_ref, *refs, heads, n_sub, lat, lam_init):
    *io, s1_scr, s2_scr, p_scr, d_scr = refs
    q_ref, kc_ref, vc_ref, kl_ref, vl_ref, o_ref = _attn_refs(io, lat)
    lf = lam_ref[...]
    lam = (jnp.exp(jnp.sum(lf[0:1] * lf[1:2], axis=-1, keepdims=True))
           - jnp.exp(jnp.sum(lf[2:3] * lf[3:4], axis=-1, keepdims=True)) + lam_init)
    first = lax.broadcasted_iota(jnp.int32, (1, 2 * DIFF_QK), 1) < DIFF_QK

    def scores(slot, hh, j):
        cols = slice(hh * DIFF_V, (hh + 1) * DIFF_V)
        q = q_ref[j * SUB:(j + 1) * SUB, cols]
        kc = kc_ref[:, cols]
        kl = kl_ref[:, cols] if lat else None
        _store_scores(s1_scr, slot, jnp.where(first, q, jnp.zeros_like(q)), kc, kl)
        _store_scores(s2_scr, slot, jnp.where(first, jnp.zeros_like(q), q), kc, kl)

    def finish(slot, hh, j):
        half = _half_rows(s1_scr.shape[-1])
        for r in range(0, SUB, 2 * half):
            p = []
            for r0 in (r, r + half):
                e1 = _chunk_numerators(s1_scr[slot, r0:r0 + half, :])
                e2 = _chunk_numerators(s2_scr[slot, r0:r0 + half, :])
                d1 = jnp.sum(e1, axis=-1, keepdims=True)
                d2 = jnp.sum(e2, axis=-1, keepdims=True)
                p.append(e1 - e2 * (lam * d1 / d2))
                d_scr[slot, r0:r0 + half, :] = jnp.broadcast_to(d1, (half, DIFF_V))
            p_scr[slot, r:r + 2 * half, :] = jnp.concatenate(p, axis=0).astype(BF16)
        cols = slice(hh * DIFF_V, (hh + 1) * DIFF_V)
        o = _weighted_values(p_scr, slot, vc_ref[:, cols], vl_ref[:, cols] if lat else None) / d_scr[slot]
        o_ref[j * SUB:(j + 1) * SUB, cols] = (_rms(o) * sub_ref[...] * (1.0 - lam_init)).astype(BF16)

    _one_ahead([(hh, j) for hh in range(heads) for j in range(n_sub)], scores, finish)


def _diff_scratch(n_keys):
    return [pltpu.VMEM((2, SUB, n_keys), F32), pltpu.VMEM((2, SUB, n_keys), F32),
            pltpu.VMEM((2, SUB, n_keys), BF16), pltpu.VMEM((2, SUB, DIFF_V), F32)]


def _attention(kernel_fn, scratch_fn, name, q, k_ctx, v_ctx, k_lat=None, v_lat=None, *, n_heads, w_qk, w_v, w_o,
               tq, extra_args=(), extra_specs=(), cast=()):
    if k_lat is not None:
        per_seg = SEQ // tq
        n_steps = BATCH * n_heads * per_seg
        in_specs = list(extra_specs) + [
            pl.BlockSpec((tq, w_qk), lambda b, h, t: (b * per_seg + t, h)),
            pl.BlockSpec((CTX_LEN, w_qk), lambda b, h, t: (b, h)),
            pl.BlockSpec((CTX_LEN, w_v), lambda b, h, t: (b, h)),
            pl.BlockSpec((SEQ, w_qk), lambda b, h, t: (b, h)),
            pl.BlockSpec((SEQ, w_v), lambda b, h, t: (b, h)),
        ]
        out_shape = [jax.ShapeDtypeStruct((LAT_ROWS, n_heads * w_o), BF16)]
        out_specs = [pl.BlockSpec((tq, w_o), lambda b, h, t: (b * per_seg + t, h))]
        for w, l, k in cast:
            n_r, n_c = w.shape[2:]
            rows = _cast_rows(n_r, n_steps)
            blk = lambda b, h, t, last=n_r // rows - 1: jnp.minimum((b * n_heads + h) * per_seg + t, last)
            in_specs.append(pl.BlockSpec((None, None, rows, n_c), lambda b, h, t, l=l, k=k, blk=blk: (l, k, blk(b, h, t), 0)))
            out_shape.append(jax.ShapeDtypeStruct((n_r, n_c), BF16))
            out_specs.append(pl.BlockSpec((rows, n_c), lambda b, h, t, blk=blk: (blk(b, h, t), 0)))
        out = pl.pallas_call(
            functools.partial(kernel_fn, heads=1, n_sub=tq // SUB, lat=True),
            out_shape=out_shape,
            grid=(BATCH, n_heads, per_seg),
            in_specs=in_specs,
            out_specs=out_specs,
            scratch_shapes=scratch_fn(CTX_LEN + SEQ),
            compiler_params=_params(*(("arbitrary",) * 3 if cast else ("parallel", "parallel", "arbitrary"))),
            name=name,
        )(*extra_args, q, k_ctx, v_ctx, k_lat, v_lat, *[w for w, _, _ in cast])
        return out if cast else out[0]
    in_specs = list(extra_specs) + [
        pl.BlockSpec((CTX_LEN, n_heads * w_qk), lambda b: (b, 0)),
        pl.BlockSpec((CTX_LEN, n_heads * w_qk), lambda b: (b, 0)),
        pl.BlockSpec((CTX_LEN, n_heads * w_v), lambda b: (b, 0)),
    ]
    return pl.pallas_call(
        functools.partial(kernel_fn, heads=n_heads, n_sub=CTX_LEN // SUB, lat=False),
        out_shape=jax.ShapeDtypeStruct((CTX_ROWS, n_heads * w_o), BF16),
        grid=(BATCH,),
        in_specs=in_specs,
        out_specs=pl.BlockSpec((CTX_LEN, n_heads * w_o), lambda b: (b, 0)),
        scratch_shapes=scratch_fn(CTX_LEN),
        compiler_params=_params("parallel"),
        name=name + "_ctx",
    )(*extra_args, q, k_ctx, v_ctx)


def _mla_attention(q, k_ctx, v_ctx, k_lat=None, v_lat=None, *, tq=2048, cast=()):
    return _attention(_mla_kernel, _mla_scratch, "mla_attention", q, k_ctx, v_ctx, k_lat, v_lat,
                      n_heads=MLA_HEADS, w_qk=MLA_QK_PAD, w_v=MLA_V_PAD, w_o=MLA_V, tq=tq, cast=cast)


def _diff_attention(diff_lambda, subln, l, lam_init, q, k_ctx, v_ctx, k_lat=None, v_lat=None, *, tq=2048, cast=()):
    layer = lambda *_: (l, 0, 0)
    extra_specs = (pl.BlockSpec((None, 4, DIFF_QK), layer), pl.BlockSpec((None, 1, DIFF_V), layer))
    return _attention(functools.partial(_diff_kernel, lam_init=lam_init), _diff_scratch, "diff_attention",
                      q, k_ctx, v_ctx, k_lat, v_lat,
                      n_heads=DIFF_HEADS, w_qk=2 * DIFF_QK, w_v=DIFF_V, w_o=DIFF_V, tq=tq,
                      extra_args=(diff_lambda, subln.reshape(DEPTH, 1, DIFF_V)), extra_specs=extra_specs, cast=cast)


def _dft_tables(n_pos):
    k = np.arange(n_pos, dtype=np.int64)
    ang = 2.0 * np.pi * ((k[:, None] * k[None, :]) % n_pos) / n_pos
    pos = np.concatenate([np.cos(ang), -np.sin(ang)], axis=1) / math.sqrt(n_pos)
    c = np.arange(FOURIER_CH, dtype=np.int64)
    angc = 2.0 * np.pi * ((c[:, None] * c[None, :]) % FOURIER_CH) / FOURIER_CH
    ch = np.concatenate([np.cos(angc), np.sin(angc)], axis=1) / math.sqrt(FOURIER_CH)
    return jnp.asarray(pos, dtype=BF16), jnp.asarray(ch, dtype=BF16)


def _fourier_kernel(u_ref, ch_ref, pos_ref, o_ref, rhs_scr, *, n_pos):
    @pl.when(pl.program_id(1) == 0)
    def _():
        for g in range(FOURIER_GROUPS):
            a = _dot(u_ref[:, g * FOURIER_CH:(g + 1) * FOURIER_CH], ch_ref[...])
            rhs_scr[0:n_pos, g * FOURIER_CH:(g + 1) * FOURIER_CH] = a[:, :FOURIER_CH].astype(BF16)
            rhs_scr[n_pos:2 * n_pos, g * FOURIER_CH:(g + 1) * FOURIER_CH] = a[:, FOURIER_CH:].astype(BF16)

    o_ref[...] = _dot(pos_ref[...], rhs_scr[...]).astype(BF16)


def _fourier(u, n_pos, *, tr=1024):
    pos_t, ch_t = _dft_tables(n_pos)
    tr = min(tr, n_pos)
    n_j = n_pos // tr
    return pl.pallas_call(
        functools.partial(_fourier_kernel, n_pos=n_pos),
        out_shape=jax.ShapeDtypeStruct((BATCH * n_pos, FOURIER_WIDTH), BF16),
        grid=(BATCH, n_j),
        in_specs=[
            pl.BlockSpec((n_pos, FOURIER_WIDTH), lambda b, j: (b, 0)),
            pl.BlockSpec((FOURIER_CH, 2 * FOURIER_CH), lambda b, j: (0, 0)),
            pl.BlockSpec((tr, 2 * n_pos), lambda b, j: (j, 0)),
        ],
        out_specs=pl.BlockSpec((tr, FOURIER_WIDTH), lambda b, j: (b * n_j + j, 0)),
        scratch_shapes=[pltpu.VMEM((2 * n_pos, FOURIER_WIDTH), BF16)],
        compiler_params=_params("parallel", "arbitrary"),
        name="fourier_mix",
    )(u, ch_t, pos_t)


def _out_kernel(x_ref, mod_ref, om_ref, od_ref, of_ref, w_ref, o_ref):
    y = (_dot(om_ref[...], w_ref[0:MLA_WIDTH, :])
         + _dot(od_ref[...], w_ref[MLA_WIDTH:MLA_WIDTH + DIFF_WIDTH, :])
         + _dot(of_ref[...], w_ref[MLA_WIDTH + DIFF_WIDTH:, :]))
    o_ref[...] = x_ref[...] + mod_ref[2:3, :] * y


def _out_proj(x, mod, l, seg0, o_mla, o_diff, o_four, w_out, *, tr=512):
    n_rows = x.shape[0]
    row = lambda w: pl.BlockSpec((tr, w), lambda i: (i, 0))
    return pl.pallas_call(
        _out_kernel,
        out_shape=jax.ShapeDtypeStruct((n_rows, D_MODEL), F32),
        grid=(n_rows // tr,),
        in_specs=[
            row(D_MODEL), _mod_spec(l, seg0, tr, 1, 1),
            row(MLA_WIDTH), row(DIFF_WIDTH), row(FOURIER_WIDTH),
            _resident((None, D_MODEL, D_MODEL), lambda i: (l, 0, 0)),
        ],
        out_specs=row(D_MODEL),
        compiler_params=_params("parallel"),
        name="out_proj",
    )(x, mod, o_mla, o_diff, o_four, w_out)


def _rope_tables():
    rows = SEQ // GRID_W
    pos_r = jnp.repeat(jnp.arange(rows), GRID_W)
    pos_c = jnp.tile(jnp.arange(GRID_W), rows)
    d = MLA_ROPE // 2
    half = d // 2
    inv = ROPE_THETA ** (-2.0 * jnp.arange(half, dtype=F32) / d)

    def tabs(pos):
        ang = pos.astype(F32)[:, None] * inv[None, :]
        return jnp.cos(ang), jnp.sin(ang)

    cr, sr = tabs(pos_r)
    cc, sc = tabs(pos_c)
    cos64 = jnp.concatenate([cr, cr, cc, cc], axis=-1)
    sin64 = jnp.concatenate([-sr, sr, -sc, sc], axis=-1)
    return jnp.tile(cos64, (1, 2)), jnp.tile(sin64, (1, 2))


def kernel(x, c, ctx, c_ctx, ada_w, ada_b, norm_g, ffn_wg, ffn_wu, ffn_wd, w_in, mla_q_norm, mla_kv_norm,
           mla_w_uq, mla_w_ukv, diff_lambda, diff_subln, w_out, final_norm):
    assert DEPTH == 2
    s_in = jnp.concatenate([c_ctx[None, :], c, jnp.zeros((MOD_ROWS - 1 - BATCH, D_MODEL), F32)], axis=0)
    w_in_p = (w_in[..., :IN_KROPE].astype(BF16),
              jnp.pad(w_in[..., IN_KROPE:IN_REST].astype(BF16), ((0, 0), (0, 0), (0, ROPE_PAD - MLA_ROPE))),
              w_in[..., IN_REST:].astype(BF16))
    w_uq_p = jnp.pad(mla_w_uq.astype(BF16).reshape(DEPTH, MLA_Q_RANK, MLA_HEADS, MLA_NOPE + MLA_ROPE),
                     ((0, 0), (0, 0), (0, 0), (0, MLA_QK_PAD - MLA_NOPE - MLA_ROPE))
                     ).reshape(DEPTH, MLA_Q_RANK, MLA_HEADS * MLA_QK_PAD)
    w_ukv = mla_w_ukv.astype(BF16)
    w_o = w_out.astype(BF16)
    rope_tabs = _rope_tables()
    ffn_w32 = (ffn_wg, ffn_wu, ffn_wd)
    ffn_w = {(0, 0): tuple(w[0, 0].astype(BF16) for w in ffn_w32)}

    mod =_ada_table(s_in, ada_w, ada_b).reshape(DEPTH, MOD_ROWS, 3, 3, D_MODEL)

    zl = x.reshape(LAT_ROWS, D_MODEL)
    zc = ctx.reshape(CTX_ROWS, D_MODEL)
    for l in range(DEPTH):
        last = l == DEPTH - 1
        lam_init = 0.8 - 0.6 * math.exp(-0.3 * l)
        proj = functools.partial(_proj, mod=mod, l=l, g=norm_g[l, 1], w_in=w_in_p, kv_norm=mla_kv_norm,
                                 w_ukv=w_ukv, q_norm=mla_q_norm, w_uq=w_uq_p)
        diff_attn = functools.partial(_diff_attention, diff_lambda, diff_subln, l, lam_init)

        def ffn(z, k, seg0, **kw):
            wg, wu, wd = ffn_w[l, k]
            return _ffn(z, mod, l, k, seg0, norm_g[l, 2 * k], wg, wu, wd, **kw)

        zl = ffn(zl, 0, 1)
        zc = ffn(zc, 0, 0)
        kl, vl, dkl, dvl, ql, dql, ul = proj(zl, seg0=1, rope_tabs=rope_tabs)
        if last:
            kc, vc, dkc, dvc = proj(zc, seg0=0, kv_only=True)
            o_mla = _mla_attention(ql, kc, vc, kl, vl)
            o_diff = diff_attn(dql, dkc, dvc, dkl, dvl)
        else:
            kc, vc, dkc, dvc, qc, dqc, uc = proj(zc, seg0=0)
            o_mla, *w_a = _mla_attention(ql, kc, vc, kl, vl,
                                         cast=[(w, 0, 1) for w in ffn_w32] + [(w, 1, 0) for w in ffn_w32])
            o_diff, *w_b = diff_attn(dql, dkc, dvc, dkl, dvl, cast=[(w, 1, 1) for w in ffn_w32])
            ffn_w.update({(0, 1): tuple(w_a[:3]), (1, 0): tuple(w_a[3:]), (1, 1): tuple(w_b)})
        zl = _out_proj(zl, mod, l, 1, o_mla, o_diff, _fourier(ul, SEQ), w_o)
        if last:
            zl = ffn(zl, 1, 1, final_g=final_norm)
        else:
            zc = _out_proj(zc, mod, l, 0, _mla_attention(qc, kc, vc), diff_attn(dqc, dkc, dvc),
                           _fourier(uc, CTX_LEN), w_o)
            zl = ffn(zl, 1, 1)
            zc = ffn(zc, 1, 0)
    return zl.reshape(BATCH, SEQ, D_MODEL)
```

```python
import functools
import math

import numpy as np
import jax
import jax.numpy as jnp
from jax import lax
from jax.experimental import pallas as pl
from jax.experimental.pallas import tpu as pltpu

D_MODEL = 2048
BATCH = 8
SEQ = 2048
DEPTH = 2
GRID_W = 64
CTX_LEN = 256
EPS = 1e-6
ROPE_THETA = 10000.0
D_FF = 5632
N_MOD = 9

MLA_HEADS = 8
MLA_Q_RANK = 512
MLA_KV_RANK = 256
MLA_NOPE = 128
MLA_ROPE = 64
MLA_V = 128
MLA_SCALE = (MLA_NOPE + MLA_ROPE) ** -0.5
MLA_QK_PAD = 256
MLA_V_PAD = 256

DIFF_HEADS = 4
DIFF_QK = 64
DIFF_V = 2 * DIFF_QK
DIFF_SCALE = DIFF_QK ** -0.5

FOURIER_GROUPS = 4
FOURIER_CH = 128
MLA_WIDTH = MLA_HEADS * MLA_V
DIFF_WIDTH = DIFF_HEADS * DIFF_V
FOURIER_WIDTH = FOURIER_GROUPS * FOURIER_CH

SEG = SEQ
LAT_ROWS = BATCH * SEQ
CTX_ROWS = BATCH * CTX_LEN
MOD_ROWS = 16

IN_KROPE = MLA_KV_RANK
IN_REST = MLA_KV_RANK + MLA_ROPE
ROPE_PAD = 128
R_DK = 0
R_DV = 512
R_CQ = 1024
R_DQ = 1536
R_U = 2048
R_COLS = 2560

VMEM_LIMIT = 56 * 1024 * 1024
VMEM_LIMIT_FFN = 63 * 1024 * 1024 + 512 * 1024
LOG2_E = math.log2(math.e)

F32 = jnp.float32
BF16 = jnp.bfloat16


def _silu(v):
    return v / (1.0 + jnp.exp(-v))


def _dot(a, b):
    return jnp.dot(a, b, preferred_element_type=F32)


def _dot_nt(a, b):
    return lax.dot_general(a, b, (((1,), (1,)), ((), ())), preferred_element_type=F32)


def _rms(v):
    return v * lax.rsqrt(jnp.mean(v * v, axis=-1, keepdims=True) + EPS)


def _row_chunks(n_rows, chunk):
    return [slice(r, r + chunk) for r in range(0, n_rows, chunk)]


def _modulate_rows(z_ref, g_ref, mod_ref, h_ref):
    gain = g_ref[...] * (1.0 + mod_ref[1:2, :])
    shift = mod_ref[0:1, :]
    for rows in _row_chunks(z_ref.shape[0], 16):
        h_ref[rows, :] = (_rms(z_ref[rows, :]) * gain + shift).astype(BF16)


def _params(*sem, vmem=VMEM_LIMIT):
    return pltpu.CompilerParams(dimension_semantics=sem, vmem_limit_bytes=vmem)


def _resident(shape, index_map):
    return pl.BlockSpec(shape, index_map, pipeline_mode=pl.Buffered(1))


def _mod_spec(l, seg0, tile, grp, n_grid):
    if n_grid == 1:
        return pl.BlockSpec((None, None, None, 3, D_MODEL), lambda i: (l, seg0 + i * tile // SEG, grp, 0, 0))
    return pl.BlockSpec((None, None, None, 3, D_MODEL), lambda i, f: (l, seg0 + i * tile // SEG, grp, 0, 0))


def _ada_kernel(s_ref, w_ref, b_ref, o_ref):
    s = _silu(s_ref[...])
    o_ref[...] = _dot(s.astype(BF16), w_ref[...].astype(BF16)) + b_ref[...]


def _ada_table(s_in, ada_w, ada_b):
    tn = 1024
    n_cols = N_MOD * D_MODEL
    return pl.pallas_call(
        _ada_kernel,
        out_shape=jax.ShapeDtypeStruct((DEPTH, MOD_ROWS, n_cols), F32),
        grid=(DEPTH, n_cols // tn),
        in_specs=[
            pl.BlockSpec((MOD_ROWS, D_MODEL), lambda l, n: (0, 0)),
            pl.BlockSpec((None, D_MODEL, tn), lambda l, n: (l, 0, n)),
            pl.BlockSpec((None, 1, tn), lambda l, n: (l, 0, n)),
        ],
        out_specs=pl.BlockSpec((None, MOD_ROWS, tn), lambda l, n: (l, 0, n)),
        compiler_params=_params("parallel", "parallel"),
        name="ada_table",
    )(s_in, ada_w, ada_b.reshape(DEPTH, 1, n_cols))


DOWN_COLS = 512


def _ffn_kernel(z_ref, mod_ref, g_ref, wg_hbm, wu_hbm, wd_hbm, *rest, n_f, tf, final):
    rest = list(rest)
    fn_ref = rest.pop(0) if final else None
    o_hbm, h_scr, acc_scr, wg_buf, wu_buf, wd_buf, w_sem, o_sem = rest
    i = pl.program_id(0)
    n_i = pl.num_programs(0)
    tm = acc_scr.shape[0]
    parity = i % 2

    def slot_of(c):
        return (parity + c) % 2

    def up_copies(c):
        cols = pl.ds((c % n_f) * tf, tf)
        return (pltpu.make_async_copy(wg_hbm.at[:, cols], wg_buf.at[slot_of(c)], w_sem.at[0, slot_of(c)]),
                pltpu.make_async_copy(wu_hbm.at[:, cols], wu_buf.at[slot_of(c)], w_sem.at[1, slot_of(c)]))

    def down_copies(c):
        rows = pl.ds((c % n_f) * tf, tf)
        return (pltpu.make_async_copy(wd_hbm.at[rows, :], wd_buf.at[slot_of(c)], w_sem.at[2, slot_of(c)]),)

    def start(copies, c):
        def go():
            for cp in copies(c):
                cp.start()
        if c < n_f:
            go()
        else:
            pl.when(i + 1 < n_i)(go)

    def wait(copies, c):
        for cp in copies(c):
            cp.wait()

    def gate_up(c):
        h = h_scr[...]
        return _dot(h, wg_buf[slot_of(c)]), _dot(h, wu_buf[slot_of(c)])

    def result_copy(tile):
        return pltpu.make_async_copy(acc_scr, o_hbm.at[pl.ds(tile * tm, tm), :], o_sem.at[0])

    @pl.when(i == 0)
    def _():
        for c in (0, 1):
            start(up_copies, c)
            start(down_copies, c)

    _modulate_rows(z_ref, g_ref, mod_ref, h_scr)

    @pl.when(i > 0)
    def _():
        result_copy(i - 1).wait()

    wait(up_copies, 0)
    gu = gate_up(0)
    start(up_copies, 2)
    for c in range(n_f):
        if c + 1 < n_f:
            wait(up_copies, c + 1)
        wait(down_copies, c)
        gu_next = gate_up(c + 1) if c + 1 < n_f else None
        a = (_silu(gu[0]) * gu[1]).astype(BF16)
        for col in range(0, D_MODEL, DOWN_COLS):
            d = _dot(a, wd_buf[slot_of(c), :, col:col + DOWN_COLS])
            if c == 0:
                acc_scr[:, col:col + DOWN_COLS] = d
            else:
                acc_scr[:, col:col + DOWN_COLS] += d
        if c + 1 < n_f:
            start(up_copies, c + 3)
        start(down_copies, c + 2)
        gu = gu_next

    half_gate = 0.5 * mod_ref[2:3, :]
    for rows in _row_chunks(tm, 8):
        y = z_ref[rows, :] + half_gate * acc_scr[rows, :]
        if final:
            y = _rms(y) * fn_ref[...]
        acc_scr[rows, :] = y
    result_copy(i).start()

    @pl.when(i == n_i - 1)
    def _():
        result_copy(i).wait()


def _ffn(z, mod, l, k, seg0, g, wg, wu, wd, *, final_g=None, tm=1024, tf=512):
    n_rows = z.shape[0]
    n_f = D_FF // tf
    assert n_f % 2 == 1 and n_f >= 3
    final = final_g is not None
    hbm = pl.BlockSpec(memory_space=pl.ANY)
    in_specs = [
        pl.BlockSpec((tm, D_MODEL), lambda i: (i, 0)),
        _mod_spec(l, seg0, tm, 2 * k, 1),
        pl.BlockSpec((1, D_MODEL), lambda i: (0, 0)),
        hbm, hbm, hbm,
    ]
    args = [z, mod, g.reshape(1, D_MODEL), wg, wu, wd]
    if final:
        in_specs.append(pl.BlockSpec((1, D_MODEL), lambda i: (0, 0)))
        args.append(final_g.reshape(1, D_MODEL))
    return pl.pallas_call(
        functools.partial(_ffn_kernel, n_f=n_f, tf=tf, final=final),
        out_shape=jax.ShapeDtypeStruct((n_rows, D_MODEL), F32),
        grid=(n_rows // tm,),
        in_specs=in_specs,
        out_specs=hbm,
        scratch_shapes=[
            pltpu.VMEM((tm, D_MODEL), BF16),
            pltpu.VMEM((tm, D_MODEL), F32),
            pltpu.VMEM((2, D_MODEL, tf), BF16),
            pltpu.VMEM((2, D_MODEL, tf), BF16),
            pltpu.VMEM((2, tf, D_MODEL), BF16),
            pltpu.SemaphoreType.DMA((3, 2)),
            pltpu.SemaphoreType.DMA((1,)),
        ],
        compiler_params=_params("arbitrary", vmem=VMEM_LIMIT),
        name="ffn_final" if final else "ffn",
    )(*args)


def _rope(v, cos, sin, lo_half):
    swapped = jnp.where(lo_half, pltpu.roll(v, 112, 1), pltpu.roll(v, 16, 1))
    return v * cos + swapped * sin


def _proj_kernel(*refs, rope, kv_only):
    refs = list(refs)
    x_ref, mod_ref, g_ref, wckv_ref, wkr_ref, wrest_ref, kvn_ref, wukv_ref = refs[:8]
    del refs[:8]
    if not kv_only:
        qn_ref, wuq_ref = refs[:2]
        del refs[:2]
    if rope:
        cos_ref, sin_ref = refs[:2]
        del refs[:2]
    h_scr = refs.pop()
    if kv_only:
        kmla_ref, vmla_ref, dk_ref, dv_ref = refs
    else:
        kmla_ref, vmla_ref, dk_ref, dv_ref, qmla_ref, dq_ref, u_ref = refs

    _modulate_rows(x_ref, g_ref, mod_ref, h_scr)
    hb = h_scr[...]
    if rope:
        lane = lax.broadcasted_iota(jnp.int32, (1, 128), 1)
        rotate = functools.partial(_rope, cos=cos_ref[...], sin=sin_ref[...], lo_half=(lane % 32) < 16)
    else:
        rotate = lambda v: v
    q_scale = MLA_SCALE * LOG2_E
    dq_scale = DIFF_SCALE * LOG2_E

    ckv = _rms(_dot(hb, wckv_ref[...])) * kvn_ref[...]
    kv = _dot(ckv.astype(BF16), wukv_ref[...])
    k_rope = rotate(_dot(hb, wkr_ref[...])).astype(BF16)
    for h in range(MLA_HEADS):
        c0 = h * (MLA_NOPE + MLA_V)
        kmla_ref[:, h * MLA_QK_PAD:h * MLA_QK_PAD + MLA_NOPE] = kv[:, c0:c0 + MLA_NOPE].astype(BF16)
        kmla_ref[:, h * MLA_QK_PAD + MLA_NOPE:(h + 1) * MLA_QK_PAD] = k_rope
        vmla_ref[:, h * MLA_V_PAD:h * MLA_V_PAD + MLA_V] = kv[:, c0 + MLA_NOPE:c0 + MLA_NOPE + MLA_V].astype(BF16)
        vmla_ref[:, h * MLA_V_PAD + MLA_V:(h + 1) * MLA_V_PAD] = jnp.ones((kv.shape[0], MLA_V_PAD - MLA_V), BF16)
    dk = _dot(hb, wrest_ref[:, R_DK:R_DV])
    for j in range(DIFF_HEADS):
        dk_ref[:, j * 128:(j + 1) * 128] = rotate(dk[:, j * 128:(j + 1) * 128]).astype(BF16)
    dv_ref[...] = _dot(hb, wrest_ref[:, R_DV:R_CQ]).astype(BF16)
    if kv_only:
        return

    cq = _rms(_dot(hb, wrest_ref[:, R_CQ:R_DQ])) * qn_ref[...]
    q = _dot(cq.astype(BF16), wuq_ref[...])
    for h in range(MLA_HEADS):
        c0 = h * MLA_QK_PAD
        qmla_ref[:, c0:c0 + 128] = (q[:, c0:c0 + 128] * q_scale).astype(BF16)
        qmla_ref[:, c0 + 128:c0 + 256] = (rotate(q[:, c0 + 128:c0 + 256]) * q_scale).astype(BF16)
    dq = _dot(hb, wrest_ref[:, R_DQ:R_U])
    for j in range(DIFF_HEADS):
        dq_ref[:, j * 128:(j + 1) * 128] = (rotate(dq[:, j * 128:(j + 1) * 128]) * dq_scale).astype(BF16)
    u_ref[...] = _dot(hb, wrest_ref[:, R_U:R_COLS]).astype(BF16)


def _proj(x, mod, l, seg0, g, w_in, kv_norm, w_ukv, q_norm, w_uq, rope_tabs=None, *, kv_only=False, tr=512):
    n_rows = x.shape[0]
    row = lambda w: pl.BlockSpec((tr, w), lambda i: (i, 0))
    vec = lambda w: _resident((None, 1, w), lambda i: (l, 0, 0))
    mat = lambda r, c: _resident((None, r, c), lambda i: (l, 0, 0))
    in_specs = [row(D_MODEL), _mod_spec(l, seg0, tr, 1, 1), _resident((1, D_MODEL), lambda i: (0, 0)),
                mat(D_MODEL, MLA_KV_RANK), mat(D_MODEL, ROPE_PAD), mat(D_MODEL, R_COLS),
                vec(MLA_KV_RANK), mat(MLA_KV_RANK, MLA_HEADS * (MLA_NOPE + MLA_V))]
    args = [x, mod, g.reshape(1, D_MODEL), *w_in, kv_norm.reshape(DEPTH, 1, -1), w_ukv]
    widths = [MLA_HEADS * MLA_QK_PAD, MLA_HEADS * MLA_V_PAD, DIFF_WIDTH, DIFF_WIDTH]
    if not kv_only:
        in_specs += [vec(MLA_Q_RANK), mat(MLA_Q_RANK, MLA_HEADS * MLA_QK_PAD)]
        args += [q_norm.reshape(DEPTH, 1, -1), w_uq]
        widths += [MLA_HEADS * MLA_QK_PAD, DIFF_WIDTH, FOURIER_WIDTH]
    if rope_tabs is not None:
        per_seg = SEG // tr
        in_specs += [pl.BlockSpec((tr, 128), lambda i: (i % per_seg, 0))] * 2
        args += list(rope_tabs)
    return pl.pallas_call(
        functools.partial(_proj_kernel, rope=rope_tabs is not None, kv_only=kv_only),
        out_shape=[jax.ShapeDtypeStruct((n_rows, w), BF16) for w in widths],
        grid=(n_rows // tr,),
        in_specs=in_specs,
        out_specs=[row(w) for w in widths],
        scratch_shapes=[pltpu.VMEM((tr, D_MODEL), BF16)],
        compiler_params=_params("parallel"),
        name="in_proj",
    )(*args)


SUB = 256
CHUNK_VREGS = 18


def _half_rows(n_keys):
    half = 8
    while 2 * half * n_keys <= CHUNK_VREGS * 1024 and 4 * half <= SUB:
        half *= 2
    return half


def _attn_refs(refs, lat):
    if not lat:
        q_ref, kc_ref, vc_ref, o_ref = refs
        return q_ref, kc_ref, vc_ref, None, None, o_ref
    n_cast = (len(refs) - 6) // 2
    for src, dst in zip(refs[5:5 + n_cast], refs[6 + n_cast:]):
        dst[...] = src[...].astype(BF16)
    return (*refs[:5], refs[5 + n_cast])


def _cast_rows(n_rows, n_steps):
    return min(r for r in range(16, n_rows + 1, 16) if n_rows % r == 0 and n_rows // r <= n_steps)


def _one_ahead(items, scores, finish):
    scores(0, *items[0])
    for i, item in enumerate(items):
        if i + 1 < len(items):
            scores((i + 1) % 2, *items[i + 1])
        finish(i % 2, *item)


def _store_scores(s_scr, slot, q, kc, kl):
    s_scr[slot, :, :CTX_LEN] = _dot_nt(q, kc)
    if kl is not None:
        s_scr[slot, :, CTX_LEN:] = _dot_nt(q, kl)


def _weighted_values(p_scr, slot, vc, vl):
    o = _dot(p_scr[slot, :, :CTX_LEN], vc)
    if vl is not None:
        o = o + _dot(p_scr[slot, :, CTX_LEN:], vl)
    return o


def _chunk_numerators(s):
    return jnp.exp2(s - jnp.max(s, axis=-1, keepdims=True))


def _mla_kernel(*refs, heads, n_sub, lat):
    *io, s_scr, p_scr = refs
    q_ref, kc_ref, vc_ref, kl_ref, vl_ref, o_ref = _attn_refs(io, lat)

    def scores(slot, hh, j):
        qk = slice(hh * MLA_QK_PAD, (hh + 1) * MLA_QK_PAD)
        _store_scores(s_scr, slot, q_ref[j * SUB:(j + 1) * SUB, qk], kc_ref[:, qk], kl_ref[:, qk] if lat else None)

    def finish(slot, hh, j):
        half = _half_rows(s_scr.shape[-1])
        for r in range(0, SUB, 2 * half):
            e = [_chunk_numerators(s_scr[slot, r0:r0 + half, :]) for r0 in (r, r + half)]
            p_scr[slot, r:r + 2 * half, :] = jnp.concatenate(e, axis=0).astype(BF16)
        vv = slice(hh * MLA_V_PAD, (hh + 1) * MLA_V_PAD)
        o = _weighted_values(p_scr, slot, vc_ref[:, vv], vl_ref[:, vv] if lat else None)
        o_ref[j * SUB:(j + 1) * SUB, hh * MLA_V:(hh + 1) * MLA_V] = (o[:, :MLA_V] / o[:, MLA_V:]).astype(BF16)

    _one_ahead([(hh, j) for hh in range(heads) for j in range(n_sub)], scores, finish)


def _mla_scratch(n_keys):
    return [pltpu.VMEM((2, SUB, n_keys), F32), pltpu.VMEM((2, SUB, n_keys), BF16)]


def _diff_kernel(lam_ref, sub_ref, *refs, heads, n_sub, lat, lam_init):
    *io, s1_scr, s2_scr, p_scr, d_scr = refs
    q_ref, kc_ref, vc_ref, kl_ref, vl_ref, o_ref = _attn_refs(io, lat)
    lf = lam_ref[...]
    lam = (jnp.exp(jnp.sum(lf[0:1] * lf[1:2], axis=-1, keepdims=True))
           - jnp.exp(jnp.sum(lf[2:3] * lf[3:4], axis=-1, keepdims=True)) + lam_init)
    first = lax.broadcasted_iota(jnp.int32, (1, 2 * DIFF_QK), 1) < DIFF_QK

    def scores(slot, hh, j):
        cols = slice(hh * DIFF_V, (hh + 1) * DIFF_V)
        q = q_ref[j * SUB:(j + 1) * SUB, cols]
        kc = kc_ref[:, cols]
        kl = kl_ref[:, cols] if lat else None
        _store_scores(s1_scr, slot, jnp.where(first, q, jnp.zeros_like(q)), kc, kl)
        _store_scores(s2_scr, slot, jnp.where(first, jnp.zeros_like(q), q), kc, kl)

    def finish(slot, hh, j):
        half = _half_rows(s1_scr.shape[-1])
        for r in range(0, SUB, 2 * half):
            p = []
            for r0 in (r, r + half):
                e1 = _chunk_numerators(s1_scr[slot, r0:r0 + half, :])
                e2 = _chunk_numerators(s2_scr[slot, r0:r0 + half, :])
                d1 = jnp.sum(e1, axis=-1, keepdims=True)
                d2 = jnp.sum(e2, axis=-1, keepdims=True)
                p.append(e1 - e2 * (lam * d1 / d2))
                d_scr[slot, r0:r0 + half, :] = jnp.broadcast_to(d1, (half, DIFF_V))
            p_scr[slot, r:r + 2 * half, :] = jnp.concatenate(p, axis=0).astype(BF16)
        cols = slice(hh * DIFF_V, (hh + 1) * DIFF_V)
        o = _weighted_values(p_scr, slot, vc_ref[:, cols], vl_ref[:, cols] if lat else None) / d_scr[slot]
        o_ref[j * SUB:(j + 1) * SUB, cols] = (_rms(o) * sub_ref[...] * (1.0 - lam_init)).astype(BF16)

    _one_ahead([(hh, j) for hh in range(heads) for j in range(n_sub)], scores, finish)


def _diff_scratch(n_keys):
    return [pltpu.VMEM((2, SUB, n_keys), F32), pltpu.VMEM((2, SUB, n_keys), F32),
            pltpu.VMEM((2, SUB, n_keys), BF16), pltpu.VMEM((2, SUB, DIFF_V), F32)]


def _attention(kernel_fn, scratch_fn, name, q, k_ctx, v_ctx, k_lat=None, v_lat=None, *, n_heads, w_qk, w_v, w_o,
               tq, extra_args=(), extra_specs=(), cast=()):
    if k_lat is not None:
        per_seg = SEQ // tq
        n_steps = BATCH * n_heads * per_seg
        in_specs = list(extra_specs) + [
            pl.BlockSpec((tq, w_qk), lambda b, h, t: (b * per_seg + t, h)),
            pl.BlockSpec((CTX_LEN, w_qk), lambda b, h, t: (b, h)),
            pl.BlockSpec((CTX_LEN, w_v), lambda b, h, t: (b, h)),
            pl.BlockSpec((SEQ, w_qk), lambda b, h, t: (b, h)),
            pl.BlockSpec((SEQ, w_v), lambda b, h, t: (b, h)),
        ]
        out_shape = [jax.ShapeDtypeStruct((LAT_ROWS, n_heads * w_o), BF16)]
        out_specs = [pl.BlockSpec((tq, w_o), lambda b, h, t: (b * per_seg + t, h))]
        for w, l, k in cast:
            n_r, n_c = w.shape[2:]
            rows = _cast_rows(n_r, n_steps)
            blk = lambda b, h, t, last=n_r // rows - 1: jnp.minimum((b * n_heads + h) * per_seg + t, last)
            in_specs.append(pl.BlockSpec((None, None, rows, n_c), lambda b, h, t, l=l, k=k, blk=blk: (l, k, blk(b, h, t), 0)))
            out_shape.append(jax.ShapeDtypeStruct((n_r, n_c), BF16))
            out_specs.append(pl.BlockSpec((rows, n_c), lambda b, h, t, blk=blk: (blk(b, h, t), 0)))
        out = pl.pallas_call(
            functools.partial(kernel_fn, heads=1, n_sub=tq // SUB, lat=True),
            out_shape=out_shape,
            grid=(BATCH, n_heads, per_seg),
            in_specs=in_specs,
            out_specs=out_specs,
            scratch_shapes=scratch_fn(CTX_LEN + SEQ),
            compiler_params=_params(*(("arbitrary",) * 3 if cast else ("parallel", "parallel", "arbitrary"))),
            name=name,
        )(*extra_args, q, k_ctx, v_ctx, k_lat, v_lat, *[w for w, _, _ in cast])
        return out if cast else out[0]
    in_specs = list(extra_specs) + [
        pl.BlockSpec((CTX_LEN, n_heads * w_qk), lambda b: (b, 0)),
        pl.BlockSpec((CTX_LEN, n_heads * w_qk), lambda b: (b, 0)),
        pl.BlockSpec((CTX_LEN, n_heads * w_v), lambda b: (b, 0)),
    ]
    return pl.pallas_call(
        functools.partial(kernel_fn, heads=n_heads, n_sub=CTX_LEN // SUB, lat=False),
        out_shape=jax.ShapeDtypeStruct((CTX_ROWS, n_heads * w_o), BF16),
        grid=(BATCH,),
        in_specs=in_specs,
        out_specs=pl.BlockSpec((CTX_LEN, n_heads * w_o), lambda b: (b, 0)),
        scratch_shapes=scratch_fn(CTX_LEN),
        compiler_params=_params("parallel"),
        name=name + "_ctx",
    )(*extra_args, q, k_ctx, v_ctx)


def _mla_attention(q, k_ctx, v_ctx, k_lat=None, v_lat=None, *, tq=2048, cast=()):
    return _attention(_mla_kernel, _mla_scratch, "mla_attention", q, k_ctx, v_ctx, k_lat, v_lat,
                      n_heads=MLA_HEADS, w_qk=MLA_QK_PAD, w_v=MLA_V_PAD, w_o=MLA_V, tq=tq, cast=cast)


def _diff_attention(diff_lambda, subln, l, lam_init, q, k_ctx, v_ctx, k_lat=None, v_lat=None, *, tq=2048, cast=()):
    layer = lambda *_: (l, 0, 0)
    extra_specs = (pl.BlockSpec((None, 4, DIFF_QK), layer), pl.BlockSpec((None, 1, DIFF_V), layer))
    return _attention(functools.partial(_diff_kernel, lam_init=lam_init), _diff_scratch, "diff_attention",
                      q, k_ctx, v_ctx, k_lat, v_lat,
                      n_heads=DIFF_HEADS, w_qk=2 * DIFF_QK, w_v=DIFF_V, w_o=DIFF_V, tq=tq,
                      extra_args=(diff_lambda, subln.reshape(DEPTH, 1, DIFF_V)), extra_specs=extra_specs, cast=cast)


def _dft_tables(n_pos):
    k = np.arange(n_pos, dtype=np.int64)
    ang = 2.0 * np.pi * ((k[:, None] * k[None, :]) % n_pos) / n_pos
    pos = np.concatenate([np.cos(ang), -np.sin(ang)], axis=1) / math.sqrt(n_pos)
    c = np.arange(FOURIER_CH, dtype=np.int64)
    angc = 2.0 * np.pi * ((c[:, None] * c[None, :]) % FOURIER_CH) / FOURIER_CH
    ch = np.concatenate([np.cos(angc), np.sin(angc)], axis=1) / math.sqrt(FOURIER_CH)
    return jnp.asarray(pos, dtype=BF16), jnp.asarray(ch, dtype=BF16)


def _fourier_kernel(u_ref, ch_ref, pos_ref, o_ref, rhs_scr, *, n_pos):
    @pl.when(pl.program_id(1) == 0)
    def _():
        for g in range(FOURIER_GROUPS):
            a = _dot(u_ref[:, g * FOURIER_CH:(g + 1) * FOURIER_CH], ch_ref[...])
            rhs_scr[0:n_pos, g * FOURIER_CH:(g + 1) * FOURIER_CH] = a[:, :FOURIER_CH].astype(BF16)
            rhs_scr[n_pos:2 * n_pos, g * FOURIER_CH:(g + 1) * FOURIER_CH] = a[:, FOURIER_CH:].astype(BF16)

    o_ref[...] = _dot(pos_ref[...], rhs_scr[...]).astype(BF16)


def _fourier(u, n_pos, *, tr=1024):
    pos_t, ch_t = _dft_tables(n_pos)
    tr = min(tr, n_pos)
    n_j = n_pos // tr
    return pl.pallas_call(
        functools.partial(_fourier_kernel, n_pos=n_pos),
        out_shape=jax.ShapeDtypeStruct((BATCH * n_pos, FOURIER_WIDTH), BF16),
        grid=(BATCH, n_j),
        in_specs=[
            pl.BlockSpec((n_pos, FOURIER_WIDTH), lambda b, j: (b, 0)),
            pl.BlockSpec((FOURIER_CH, 2 * FOURIER_CH), lambda b, j: (0, 0)),
            pl.BlockSpec((tr, 2 * n_pos), lambda b, j: (j, 0)),
        ],
        out_specs=pl.BlockSpec((tr, FOURIER_WIDTH), lambda b, j: (b * n_j + j, 0)),
        scratch_shapes=[pltpu.VMEM((2 * n_pos, FOURIER_WIDTH), BF16)],
        compiler_params=_params("parallel", "arbitrary"),
        name="fourier_mix",
    )(u, ch_t, pos_t)


def _out_kernel(x_ref, mod_ref, om_ref, od_ref, of_ref, w_ref, o_ref):
    y = (_dot(om_ref[...], w_ref[0:MLA_WIDTH, :])
         + _dot(od_ref[...], w_ref[MLA_WIDTH:MLA_WIDTH + DIFF_WIDTH, :])
         + _dot(of_ref[...], w_ref[MLA_WIDTH + DIFF_WIDTH:, :]))
    o_ref[...] = x_ref[...] + mod_ref[2:3, :] * y


def _out_proj(x, mod, l, seg0, o_mla, o_diff, o_four, w_out, *, tr=512):
    n_rows = x.shape[0]
    row = lambda w: pl.BlockSpec((tr, w), lambda i: (i, 0))
    return pl.pallas_call(
        _out_kernel,
        out_shape=jax.ShapeDtypeStruct((n_rows, D_MODEL), F32),
        grid=(n_rows // tr,),
        in_specs=[
            row(D_MODEL), _mod_spec(l, seg0, tr, 1, 1),
            row(MLA_WIDTH), row(DIFF_WIDTH), row(FOURIER_WIDTH),
            _resident((None, D_MODEL, D_MODEL), lambda i: (l, 0, 0)),
        ],
        out_specs=row(D_MODEL),
        compiler_params=_params("parallel"),
        name="out_proj",
    )(x, mod, o_mla, o_diff, o_four, w_out)


def _rope_tables():
    rows = SEQ // GRID_W
    pos_r = jnp.repeat(jnp.arange(rows), GRID_W)
    pos_c = jnp.tile(jnp.arange(GRID_W), rows)
    d = MLA_ROPE // 2
    half = d // 2
    inv = ROPE_THETA ** (-2.0 * jnp.arange(half, dtype=F32) / d)

    def tabs(pos):
        ang = pos.astype(F32)[:, None] * inv[None, :]
        return jnp.cos(ang), jnp.sin(ang)

    cr, sr = tabs(pos_r)
    cc, sc = tabs(pos_c)
    cos64 = jnp.concatenate([cr, cr, cc, cc], axis=-1)
    sin64 = jnp.concatenate([-sr, sr, -sc, sc], axis=-1)
    return jnp.tile(cos64, (1, 2)), jnp.tile(sin64, (1, 2))


def kernel(x, c, ctx, c_ctx, ada_w, ada_b, norm_g, ffn_wg, ffn_wu, ffn_wd, w_in, mla_q_norm, mla_kv_norm,
           mla_w_uq, mla_w_ukv, diff_lambda, diff_subln, w_out, final_norm):
    assert DEPTH == 2
    s_in = jnp.concatenate([c_ctx[None, :], c, jnp.zeros((MOD_ROWS - 1 - BATCH, D_MODEL), F32)], axis=0)
    w_in_p = (w_in[..., :IN_KROPE].astype(BF16),
              jnp.pad(w_in[..., IN_KROPE:IN_REST].astype(BF16), ((0, 0), (0, 0), (0, ROPE_PAD - MLA_ROPE))),
              w_in[..., IN_REST:].astype(BF16))
    w_uq_p = jnp.pad(mla_w_uq.astype(BF16).reshape(DEPTH, MLA_Q_RANK, MLA_HEADS, MLA_NOPE + MLA_ROPE),
                     ((0, 0), (0, 0), (0, 0), (0, MLA_QK_PAD - MLA_NOPE - MLA_ROPE))
                     ).reshape(DEPTH, MLA_Q_RANK, MLA_HEADS * MLA_QK_PAD)
    w_ukv = mla_w_ukv.astype(BF16)
    w_o = w_out.astype(BF16)
    rope_tabs = _rope_tables()
    ffn_w32 = (ffn_wg, ffn_wu, ffn_wd)
    ffn_w = {(0, 0): tuple(w[0, 0].astype(BF16) for w in ffn_w32)}

    mod =_ada_table(s_in, ada_w, ada_b).reshape(DEPTH, MOD_ROWS, 3, 3, D_MODEL)

    zl = x.reshape(LAT_ROWS, D_MODEL)
    zc = ctx.reshape(CTX_ROWS, D_MODEL)
    for l in range(DEPTH):
        last = l == DEPTH - 1
        lam_init = 0.8 - 0.6 * math.exp(-0.3 * l)
        proj = functools.partial(_proj, mod=mod, l=l, g=norm_g[l, 1], w_in=w_in_p, kv_norm=mla_kv_norm,
                                 w_ukv=w_ukv, q_norm=mla_q_norm, w_uq=w_uq_p)
        diff_attn = functools.partial(_diff_attention, diff_lambda, diff_subln, l, lam_init)

        def ffn(z, k, seg0, **kw):
            wg, wu, wd = ffn_w[l, k]
            return _ffn(z, mod, l, k, seg0, norm_g[l, 2 * k], wg, wu, wd, **kw)

        zl = ffn(zl, 0, 1)
        zc = ffn(zc, 0, 0)
        kl, vl, dkl, dvl, ql, dql, ul = proj(zl, seg0=1, rope_tabs=rope_tabs)
        if last:
            kc, vc, dkc, dvc = proj(zc, seg0=0, kv_only=True)
            o_mla = _mla_attention(ql, kc, vc, kl, vl)
            o_diff = diff_attn(dql, dkc, dvc, dkl, dvl)
        else:
            kc, vc, dkc, dvc, qc, dqc, uc = proj(zc, seg0=0)
            o_mla, *w_a = _mla_attention(ql, kc, vc, kl, vl,
                                         cast=[(w, 0, 1) for w in ffn_w32] + [(w, 1, 0) for w in ffn_w32])
            o_diff, *w_b = diff_attn(dql, dkc, dvc, dkl, dvl, cast=[(w, 1, 1) for w in ffn_w32])
            ffn_w.update({(0, 1): tuple(w_a[:3]), (1, 0): tuple(w_a[3:]), (1, 1): tuple(w_b)})
        zl = _out_proj(zl, mod, l, 1, o_mla, o_diff, _fourier(ul, SEQ), w_o)
        if last:
            zl = ffn(zl, 1, 1, final_g=final_norm)
        else:
            zc = _out_proj(zc, mod, l, 0, _mla_attention(qc, kc, vc), diff_attn(dqc, dkc, dvc),
                           _fourier(uc, CTX_LEN), w_o)
            zl = ffn(zl, 1, 1)
            zc = ffn(zc, 1, 0)
    return zl.reshape(BATCH, SEQ, D_MODEL)
```

```python
import functools
import math

import numpy as np
import jax
import jax.numpy as jnp
from jax import lax
from jax.experimental import pallas as pl
from jax.experimental.pallas import tpu as pltpu

D_MODEL = 2048
BATCH = 8
SEQ = 2048
DEPTH = 2
GRID_W = 64
CTX_LEN = 256
EPS = 1e-6
ROPE_THETA = 10000.0
D_FF = 5632
N_MOD = 9

MLA_HEADS = 8
MLA_Q_RANK = 512
MLA_KV_RANK = 256
MLA_NOPE = 128
MLA_ROPE = 64
MLA_V = 128
MLA_SCALE = (MLA_NOPE + MLA_ROPE) ** -0.5
MLA_QK_PAD = 256
MLA_V_PAD = 256

DIFF_HEADS = 4
DIFF_QK = 64
DIFF_V = 2 * DIFF_QK
DIFF_SCALE = DIFF_QK ** -0.5

FOURIER_GROUPS = 4
FOURIER_CH = 128
MLA_WIDTH = MLA_HEADS * MLA_V
DIFF_WIDTH = DIFF_HEADS * DIFF_V
FOURIER_WIDTH = FOURIER_GROUPS * FOURIER_CH

SEG = SEQ
LAT_ROWS = BATCH * SEQ
CTX_ROWS = BATCH * CTX_LEN
MOD_ROWS = 16

IN_KROPE = MLA_KV_RANK
IN_REST = MLA_KV_RANK + MLA_ROPE
ROPE_PAD = 128
R_DK = 0
R_DV = 512
R_CQ = 1024
R_DQ = 1536
R_U = 2048
R_COLS = 2560

VMEM_LIMIT = 56 * 1024 * 1024
VMEM_LIMIT_FFN = 63 * 1024 * 1024 + 512 * 1024
LOG2_E = math.log2(math.e)

F32 = jnp.float32
BF16 = jnp.bfloat16


def _silu(v):
    return v / (1.0 + jnp.exp(-v))


def _dot(a, b):
    return jnp.dot(a, b, preferred_element_type=F32)


def _dot_nt(a, b):
    return lax.dot_general(a, b, (((1,), (1,)), ((), ())), preferred_element_type=F32)


def _rms(v):
    return v * lax.rsqrt(jnp.mean(v * v, axis=-1, keepdims=True) + EPS)


def _row_chunks(n_rows, chunk):
    return [slice(r, r + chunk) for r in range(0, n_rows, chunk)]


def _modulate_rows(z_ref, g_ref, mod_ref, h_ref):
    gain = g_ref[...] * (1.0 + mod_ref[1:2, :])
    shift = mod_ref[0:1, :]
    for rows in _row_chunks(z_ref.shape[0], 16):
        h_ref[rows, :] = (_rms(z_ref[rows, :]) * gain + shift).astype(BF16)


def _params(*sem, vmem=VMEM_LIMIT):
    return pltpu.CompilerParams(dimension_semantics=sem, vmem_limit_bytes=vmem)


def _resident(shape, index_map):
    return pl.BlockSpec(shape, index_map, pipeline_mode=pl.Buffered(1))


def _mod_spec(l, seg0, tile, grp, n_grid):
    if n_grid == 1:
        return pl.BlockSpec((None, None, None, 3, D_MODEL), lambda i: (l, seg0 + i * tile // SEG, grp, 0, 0))
    return pl.BlockSpec((None, None, None, 3, D_MODEL), lambda i, f: (l, seg0 + i * tile // SEG, grp, 0, 0))


def _ada_kernel(s_ref, w_ref, b_ref, o_ref):
    s = _silu(s_ref[...])
    o_ref[...] = _dot(s.astype(BF16), w_ref[...].astype(BF16)) + b_ref[...]


def _ada_table(s_in, ada_w, ada_b):
    tn = 1024
    n_cols = N_MOD * D_MODEL
    return pl.pallas_call(
        _ada_kernel,
        out_shape=jax.ShapeDtypeStruct((DEPTH, MOD_ROWS, n_cols), F32),
        grid=(DEPTH, n_cols // tn),
        in_specs=[
            pl.BlockSpec((MOD_ROWS, D_MODEL), lambda l, n: (0, 0)),
            pl.BlockSpec((None, D_MODEL, tn), lambda l, n: (l, 0, n)),
            pl.BlockSpec((None, 1, tn), lambda l, n: (l, 0, n)),
        ],
        out_specs=pl.BlockSpec((None, MOD_ROWS, tn), lambda l, n: (l, 0, n)),
        compiler_params=_params("parallel", "parallel"),
        name="ada_table",
    )(s_in, ada_w, ada_b.reshape(DEPTH, 1, n_cols))


DOWN_COLS = 512


def _ffn_kernel(z_ref, mod_ref, g_ref, wg_ref, wu_ref, wd_ref, *rest, n_f, final):
    if final:
        fn_ref, o_ref, h_scr = rest
    else:
        o_ref, h_scr = rest
    f = pl.program_id(1)

    @pl.when(f == 0)
    def _():
        _modulate_rows(z_ref, g_ref, mod_ref, h_scr)
        o_ref[...] = jnp.zeros_like(o_ref)

    h = h_scr[...]
    a = (_silu(_dot(h, wg_ref[...])) * _dot(h, wu_ref[...])).astype(BF16)
    for c in range(0, D_MODEL, DOWN_COLS):
        o_ref[:, c:c + DOWN_COLS] += _dot(a, wd_ref[:, c:c + DOWN_COLS])

    @pl.when(f == n_f - 1)
    def _():
        half_gate = 0.5 * mod_ref[2:3, :]
        for rows in _row_chunks(o_ref.shape[0], 8):
            y = z_ref[rows, :] + half_gate * o_ref[rows, :]
            if final:
                y = _rms(y) * fn_ref[...]
            o_ref[rows, :] = y


def _ffn(z, mod, l, k, seg0, g, wg, wu, wd, *, final_g=None, tm=1024, tf=512):
    n_rows = z.shape[0]
    n_f = D_FF // tf
    final = final_g is not None
    in_specs = [
        pl.BlockSpec((tm, D_MODEL), lambda i, f: (i, 0)),
        _mod_spec(l, seg0, tm, 2 * k, 2),
        pl.BlockSpec((1, D_MODEL), lambda i, f: (0, 0)),
        pl.BlockSpec((D_MODEL, tf), lambda i, f: (0, f)),
        pl.BlockSpec((D_MODEL, tf), lambda i, f: (0, f)),
        pl.BlockSpec((tf, D_MODEL), lambda i, f: (f, 0)),
    ]
    args = [z, mod, g.reshape(1, D_MODEL), wg, wu, wd]
    if final:
        in_specs.append(pl.BlockSpec((1, D_MODEL), lambda i, f: (0, 0)))
        args.append(final_g.reshape(1, D_MODEL))
    return pl.pallas_call(
        functools.partial(_ffn_kernel, n_f=n_f, final=final),
        out_shape=jax.ShapeDtypeStruct((n_rows, D_MODEL), F32),
        grid=(n_rows // tm, n_f),
        in_specs=in_specs,
        out_specs=pl.BlockSpec((tm, D_MODEL), lambda i, f: (i, 0)),
        scratch_shapes=[pltpu.VMEM((tm, D_MODEL), BF16)],
        compiler_params=_params("parallel", "arbitrary", vmem=VMEM_LIMIT_FFN),
        name="ffn_final" if final else "ffn",
    )(*args)


def _rope(v, cos, sin, lo_half):
    swapped = jnp.where(lo_half, pltpu.roll(v, 112, 1), pltpu.roll(v, 16, 1))
    return v * cos + swapped * sin


def _proj_kernel(*refs, rope, kv_only):
    refs = list(refs)
    x_ref, mod_ref, g_ref, wckv_ref, wkr_ref, wrest_ref, kvn_ref, wukv_ref = refs[:8]
    del refs[:8]
    if not kv_only:
        qn_ref, wuq_ref = refs[:2]
        del refs[:2]
    if rope:
        cos_ref, sin_ref = refs[:2]
        del refs[:2]
    h_scr = refs.pop()
    if kv_only:
        kmla_ref, vmla_ref, dk_ref, dv_ref = refs
    else:
        kmla_ref, vmla_ref, dk_ref, dv_ref, qmla_ref, dq_ref, u_ref = refs

    _modulate_rows(x_ref, g_ref, mod_ref, h_scr)
    hb = h_scr[...]
    if rope:
        lane = lax.broadcasted_iota(jnp.int32, (1, 128), 1)
        rotate = functools.partial(_rope, cos=cos_ref[...], sin=sin_ref[...], lo_half=(lane % 32) < 16)
    else:
        rotate = lambda v: v
    q_scale = MLA_SCALE * LOG2_E
    dq_scale = DIFF_SCALE * LOG2_E

    ckv = _rms(_dot(hb, wckv_ref[...])) * kvn_ref[...]
    kv = _dot(ckv.astype(BF16), wukv_ref[...])
    k_rope = rotate(_dot(hb, wkr_ref[...])).astype(BF16)
    for h in range(MLA_HEADS):
        c0 = h * (MLA_NOPE + MLA_V)
        kmla_ref[:, h * MLA_QK_PAD:h * MLA_QK_PAD + MLA_NOPE] = kv[:, c0:c0 + MLA_NOPE].astype(BF16)
        kmla_ref[:, h * MLA_QK_PAD + MLA_NOPE:(h + 1) * MLA_QK_PAD] = k_rope
        vmla_ref[:, h * MLA_V_PAD:h * MLA_V_PAD + MLA_V] = kv[:, c0 + MLA_NOPE:c0 + MLA_NOPE + MLA_V].astype(BF16)
        vmla_ref[:, h * MLA_V_PAD + MLA_V:(h + 1) * MLA_V_PAD] = jnp.ones((kv.shape[0], MLA_V_PAD - MLA_V), BF16)
    dk = _dot(hb, wrest_ref[:, R_DK:R_DV])
    for j in range(DIFF_HEADS):
        dk_ref[:, j * 128:(j + 1) * 128] = rotate(dk[:, j * 128:(j + 1) * 128]).astype(BF16)
    dv_ref[...] = _dot(hb, wrest_ref[:, R_DV:R_CQ]).astype(BF16)
    if kv_only:
        return

    cq = _rms(_dot(hb, wrest_ref[:, R_CQ:R_DQ])) * qn_ref[...]
    q = _dot(cq.astype(BF16), wuq_ref[...])
    for h in range(MLA_HEADS):
        c0 = h * MLA_QK_PAD
        qmla_ref[:, c0:c0 + 128] = (q[:, c0:c0 + 128] * q_scale).astype(BF16)
        qmla_ref[:, c0 + 128:c0 + 256] = (rotate(q[:, c0 + 128:c0 + 256]) * q_scale).astype(BF16)
    dq = _dot(hb, wrest_ref[:, R_DQ:R_U])
    for j in range(DIFF_HEADS):
        dq_ref[:, j * 128:(j + 1) * 128] = (rotate(dq[:, j * 128:(j + 1) * 128]) * dq_scale).astype(BF16)
    u_ref[...] = _dot(hb, wrest_ref[:, R_U:R_COLS]).astype(BF16)


def _proj(x, mod, l, seg0, g, w_in, kv_norm, w_ukv, q_norm, w_uq, rope_tabs=None, *, kv_only=False, tr=512):
    n_rows = x.shape[0]
    row = lambda w: pl.BlockSpec((tr, w), lambda i: (i, 0))
    vec = lambda w: _resident((None, 1, w), lambda i: (l, 0, 0))
    mat = lambda r, c: _resident((None, r, c), lambda i: (l, 0, 0))
    in_specs = [row(D_MODEL), _mod_spec(l, seg0, tr, 1, 1), _resident((1, D_MODEL), lambda i: (0, 0)),
                mat(D_MODEL, MLA_KV_RANK), mat(D_MODEL, ROPE_PAD), mat(D_MODEL, R_COLS),
                vec(MLA_KV_RANK), mat(MLA_KV_RANK, MLA_HEADS * (MLA_NOPE + MLA_V))]
    args = [x, mod, g.reshape(1, D_MODEL), *w_in, kv_norm.reshape(DEPTH, 1, -1), w_ukv]
    widths = [MLA_HEADS * MLA_QK_PAD, MLA_HEADS * MLA_V_PAD, DIFF_WIDTH, DIFF_WIDTH]
    if not kv_only:
        in_specs += [vec(MLA_Q_RANK), mat(MLA_Q_RANK, MLA_HEADS * MLA_QK_PAD)]
        args += [q_norm.reshape(DEPTH, 1, -1), w_uq]
        widths += [MLA_HEADS * MLA_QK_PAD, DIFF_WIDTH, FOURIER_WIDTH]
    if rope_tabs is not None:
        per_seg = SEG // tr
        in_specs += [pl.BlockSpec((tr, 128), lambda i: (i % per_seg, 0))] * 2
        args += list(rope_tabs)
    return pl.pallas_call(
        functools.partial(_proj_kernel, rope=rope_tabs is not None, kv_only=kv_only),
        out_shape=[jax.ShapeDtypeStruct((n_rows, w), BF16) for w in widths],
        grid=(n_rows // tr,),
        in_specs=in_specs,
        out_specs=[row(w) for w in widths],
        scratch_shapes=[pltpu.VMEM((tr, D_MODEL), BF16)],
        compiler_params=_params("parallel"),
        name="in_proj",
    )(*args)


SUB = 256
CHUNK_VREGS = 36


def _half_rows(n_keys):
    half = 8
    while 2 * half * n_keys <= CHUNK_VREGS * 1024 and 4 * half <= SUB:
        half *= 2
    return half


def _attn_refs(refs, lat):
    if not lat:
        q_ref, kc_ref, vc_ref, o_ref = refs
        return q_ref, kc_ref, vc_ref, None, None, o_ref
    n_cast = (len(refs) - 6) // 2
    for src, dst in zip(refs[5:5 + n_cast], refs[6 + n_cast:]):
        dst[...] = src[...].astype(BF16)
    return (*refs[:5], refs[5 + n_cast])


def _cast_rows(n_rows, n_steps):
    return min(r for r in range(16, n_rows + 1, 16) if n_rows % r == 0 and n_rows // r <= n_steps)


def _one_ahead(items, scores, finish):
    scores(0, *items[0])
    for i, item in enumerate(items):
        if i + 1 < len(items):
            scores((i + 1) % 2, *items[i + 1])
        finish(i % 2, *item)


def _store_scores(s_scr, slot, q, kc, kl):
    s_scr[slot, :, :CTX_LEN] = _dot_nt(q, kc)
    if kl is not None:
        s_scr[slot, :, CTX_LEN:] = _dot_nt(q, kl)


def _weighted_values(p_scr, slot, vc, vl):
    o = _dot(p_scr[slot, :, :CTX_LEN], vc)
    if vl is not None:
        o = o + _dot(p_scr[slot, :, CTX_LEN:], vl)
    return o


def _chunk_numerators(s):
    return jnp.exp2(s - jnp.max(s, axis=-1, keepdims=True))


def _mla_kernel(*refs, heads, n_sub, lat):
    *io, s_scr, p_scr = refs
    q_ref, kc_ref, vc_ref, kl_ref, vl_ref, o_ref = _attn_refs(io, lat)

    def scores(slot, hh, j):
        qk = slice(hh * MLA_QK_PAD, (hh + 1) * MLA_QK_PAD)
        _store_scores(s_scr, slot, q_ref[j * SUB:(j + 1) * SUB, qk], kc_ref[:, qk], kl_ref[:, qk] if lat else None)

    def finish(slot, hh, j):
        half = _half_rows(s_scr.shape[-1])
        for r in range(0, SUB, 2 * half):
            e = [_chunk_numerators(s_scr[slot, r0:r0 + half, :]) for r0 in (r, r + half)]
            p_scr[slot, r:r + 2 * half, :] = jnp.concatenate(e, axis=0).astype(BF16)
        vv = slice(hh * MLA_V_PAD, (hh + 1) * MLA_V_PAD)
        o = _weighted_values(p_scr, slot, vc_ref[:, vv], vl_ref[:, vv] if lat else None)
        o_ref[j * SUB:(j + 1) * SUB, hh * MLA_V:(hh + 1) * MLA_V] = (o[:, :MLA_V] / o[:, MLA_V:]).astype(BF16)

    _one_ahead([(hh, j) for hh in range(heads) for j in range(n_sub)], scores, finish)


def _mla_scratch(n_keys):
    return [pltpu.VMEM((2, SUB, n_keys), F32), pltpu.VMEM((2, SUB, n_keys), BF16)]


def _diff_kernel(lam_ref, sub_ref, *refs, heads, n_sub, lat, lam_init):
    *io, s1_scr, s2_scr, p_scr, d_scr = refs
    q_ref, kc_ref, vc_ref, kl_ref, vl_ref, o_ref = _attn_refs(io, lat)
    lf = lam_ref[...]
    lam = (jnp.exp(jnp.sum(lf[0:1] * lf[1:2], axis=-1, keepdims=True))
           - jnp.exp(jnp.sum(lf[2:3] * lf[3:4], axis=-1, keepdims=True)) + lam_init)
    first = lax.broadcasted_iota(jnp.int32, (1, 2 * DIFF_QK), 1) < DIFF_QK

    def scores(slot, hh, j):
        cols = slice(hh * DIFF_V, (hh + 1) * DIFF_V)
        q = q_ref[j * SUB:(j + 1) * SUB, cols]
        kc = kc_ref[:, cols]
        kl = kl_ref[:, cols] if lat else None
        _store_scores(s1_scr, slot, jnp.where(first, q, jnp.zeros_like(q)), kc, kl)
        _store_scores(s2_scr, slot, jnp.where(first, jnp.zeros_like(q), q), kc, kl)

    def finish(slot, hh, j):
        half = _half_rows(s1_scr.shape[-1])
        for r in range(0, SUB, 2 * half):
            p = []
            for r0 in (r, r + half):
                e1 = _chunk_numerators(s1_scr[slot, r0:r0 + half, :])
                e2 = _chunk_numerators(s2_scr[slot, r0:r0 + half, :])
                d1 = jnp.sum(e1, axis=-1, keepdims=True)
                d2 = jnp.sum(e2, axis=-1, keepdims=True)
                p.append(e1 - e2 * (lam * d1 / d2))
                d_scr[slot, r0:r0 + half, :] = jnp.broadcast_to(d1, (half, DIFF_V))
            p_scr[slot, r:r + 2 * half, :] = jnp.concatenate(p, axis=0).astype(BF16)
        cols = slice(hh * DIFF_V, (hh + 1) * DIFF_V)
        o = _weighted_values(p_scr, slot, vc_ref[:, cols], vl_ref[:, cols] if lat else None) / d_scr[slot]
        o_ref[j * SUB:(j + 1) * SUB, cols] = (_rms(o) * sub_ref[...] * (1.0 - lam_init)).astype(BF16)

    _one_ahead([(hh, j) for hh in range(heads) for j in range(n_sub)], scores, finish)


def _diff_scratch(n_keys):
    return [pltpu.VMEM((2, SUB, n_keys), F32), pltpu.VMEM((2, SUB, n_keys), F32),
            pltpu.VMEM((2, SUB, n_keys), BF16), pltpu.VMEM((2, SUB, DIFF_V), F32)]


def _attention(kernel_fn, scratch_fn, name, q, k_ctx, v_ctx, k_lat=None, v_lat=None, *, n_heads, w_qk, w_v, w_o,
               tq, extra_args=(), extra_specs=(), cast=()):
    if k_lat is not None:
        per_seg = SEQ // tq
        n_steps = BATCH * n_heads * per_seg
        in_specs = list(extra_specs) + [
            pl.BlockSpec((tq, w_qk), lambda b, h, t: (b * per_seg + t, h)),
            pl.BlockSpec((CTX_LEN, w_qk), lambda b, h, t: (b, h)),
            pl.BlockSpec((CTX_LEN, w_v), lambda b, h, t: (b, h)),
            pl.BlockSpec((SEQ, w_qk), lambda b, h, t: (b, h)),
            pl.BlockSpec((SEQ, w_v), lambda b, h, t: (b, h)),
        ]
        out_shape = [jax.ShapeDtypeStruct((LAT_ROWS, n_heads * w_o), BF16)]
        out_specs = [pl.BlockSpec((tq, w_o), lambda b, h, t: (b * per_seg + t, h))]
        for w, l, k in cast:
            n_r, n_c = w.shape[2:]
            rows = _cast_rows(n_r, n_steps)
            blk = lambda b, h, t, last=n_r // rows - 1: jnp.minimum((b * n_heads + h) * per_seg + t, last)
            in_specs.append(pl.BlockSpec((None, None, rows, n_c), lambda b, h, t, l=l, k=k, blk=blk: (l, k, blk(b, h, t), 0)))
            out_shape.append(jax.ShapeDtypeStruct((n_r, n_c), BF16))
            out_specs.append(pl.BlockSpec((rows, n_c), lambda b, h, t, blk=blk: (blk(b, h, t), 0)))
        out = pl.pallas_call(
            functools.partial(kernel_fn, heads=1, n_sub=tq // SUB, lat=True),
            out_shape=out_shape,
            grid=(BATCH, n_heads, per_seg),
            in_specs=in_specs,
            out_specs=out_specs,
            scratch_shapes=scratch_fn(CTX_LEN + SEQ),
            compiler_params=_params(*(("arbitrary",) * 3 if cast else ("parallel", "parallel", "arbitrary"))),
            name=name,
        )(*extra_args, q, k_ctx, v_ctx, k_lat, v_lat, *[w for w, _, _ in cast])
        return out if cast else out[0]
    in_specs = list(extra_specs) + [
        pl.BlockSpec((CTX_LEN, n_heads * w_qk), lambda b: (b, 0)),
        pl.BlockSpec((CTX_LEN, n_heads * w_qk), lambda b: (b, 0)),
        pl.BlockSpec((CTX_LEN, n_heads * w_v), lambda b: (b, 0)),
    ]
    return pl.pallas_call(
        functools.partial(kernel_fn, heads=n_heads, n_sub=CTX_LEN // SUB, lat=False),
        out_shape=jax.ShapeDtypeStruct((CTX_ROWS, n_heads * w_o), BF16),
        grid=(BATCH,),
        in_specs=in_specs,
        out_specs=pl.BlockSpec((CTX_LEN, n_heads * w_o), lambda b: (b, 0)),
        scratch_shapes=scratch_fn(CTX_LEN),
        compiler_params=_params("parallel"),
        name=name + "_ctx",
    )(*extra_args, q, k_ctx, v_ctx)


def _mla_attention(q, k_ctx, v_ctx, k_lat=None, v_lat=None, *, tq=2048, cast=()):
    return _attention(_mla_kernel, _mla_scratch, "mla_attention", q, k_ctx, v_ctx, k_lat, v_lat,
                      n_heads=MLA_HEADS, w_qk=MLA_QK_PAD, w_v=MLA_V_PAD, w_o=MLA_V, tq=tq, cast=cast)


def _diff_attention(diff_lambda, subln, l, lam_init, q, k_ctx, v_ctx, k_lat=None, v_lat=None, *, tq=2048, cast=()):
    layer = lambda *_: (l, 0, 0)
    extra_specs = (pl.BlockSpec((None, 4, DIFF_QK), layer), pl.BlockSpec((None, 1, DIFF_V), layer))
    return _attention(functools.partial(_diff_kernel, lam_init=lam_init), _diff_scratch, "diff_attention",
                      q, k_ctx, v_ctx, k_lat, v_lat,
                      n_heads=DIFF_HEADS, w_qk=2 * DIFF_QK, w_v=DIFF_V, w_o=DIFF_V, tq=tq,
                      extra_args=(diff_lambda, subln.reshape(DEPTH, 1, DIFF_V)), extra_specs=extra_specs, cast=cast)


def _dft_tables(n_pos):
    k = np.arange(n_pos, dtype=np.int64)
    ang = 2.0 * np.pi * ((k[:, None] * k[None, :]) % n_pos) / n_pos
    pos = np.concatenate([np.cos(ang), -np.sin(ang)], axis=1) / math.sqrt(n_pos)
    c = np.arange(FOURIER_CH, dtype=np.int64)
    angc = 2.0 * np.pi * ((c[:, None] * c[None, :]) % FOURIER_CH) / FOURIER_CH
    ch = np.concatenate([np.cos(angc), np.sin(angc)], axis=1) / math.sqrt(FOURIER_CH)
    return jnp.asarray(pos, dtype=BF16), jnp.asarray(ch, dtype=BF16)


def _fourier_kernel(u_ref, ch_ref, pos_ref, o_ref, rhs_scr, *, n_pos):
    @pl.when(pl.program_id(1) == 0)
    def _():
        for g in range(FOURIER_GROUPS):
            a = _dot(u_ref[:, g * FOURIER_CH:(g + 1) * FOURIER_CH], ch_ref[...])
            rhs_scr[0:n_pos, g * FOURIER_CH:(g + 1) * FOURIER_CH] = a[:, :FOURIER_CH].astype(BF16)
            rhs_scr[n_pos:2 * n_pos, g * FOURIER_CH:(g + 1) * FOURIER_CH] = a[:, FOURIER_CH:].astype(BF16)

    o_ref[...] = _dot(pos_ref[...], rhs_scr[...]).astype(BF16)


def _fourier(u, n_pos, *, tr=1024):
    pos_t, ch_t = _dft_tables(n_pos)
    tr = min(tr, n_pos)
    n_j = n_pos // tr
    return pl.pallas_call(
        functools.partial(_fourier_kernel, n_pos=n_pos),
        out_shape=jax.ShapeDtypeStruct((BATCH * n_pos, FOURIER_WIDTH), BF16),
        grid=(BATCH, n_j),
        in_specs=[
            pl.BlockSpec((n_pos, FOURIER_WIDTH), lambda b, j: (b, 0)),
            pl.BlockSpec((FOURIER_CH, 2 * FOURIER_CH), lambda b, j: (0, 0)),
            pl.BlockSpec((tr, 2 * n_pos), lambda b, j: (j, 0)),
        ],
        out_specs=pl.BlockSpec((tr, FOURIER_WIDTH), lambda b, j: (b * n_j + j, 0)),
        scratch_shapes=[pltpu.VMEM((2 * n_pos, FOURIER_WIDTH), BF16)],
        compiler_params=_params("parallel", "arbitrary"),
        name="fourier_mix",
    )(u, ch_t, pos_t)


def _out_kernel(x_ref, mod_ref, om_ref, od_ref, of_ref, w_ref, o_ref):
    y = (_dot(om_ref[...], w_ref[0:MLA_WIDTH, :])
         + _dot(od_ref[...], w_ref[MLA_WIDTH:MLA_WIDTH + DIFF_WIDTH, :])
         + _dot(of_ref[...], w_ref[MLA_WIDTH + DIFF_WIDTH:, :]))
    o_ref[...] = x_ref[...] + mod_ref[2:3, :] * y


def _out_proj(x, mod, l, seg0, o_mla, o_diff, o_four, w_out, *, tr=512):
    n_rows = x.shape[0]
    row = lambda w: pl.BlockSpec((tr, w), lambda i: (i, 0))
    return pl.pallas_call(
        _out_kernel,
        out_shape=jax.ShapeDtypeStruct((n_rows, D_MODEL), F32),
        grid=(n_rows // tr,),
        in_specs=[
            row(D_MODEL), _mod_spec(l, seg0, tr, 1, 1),
            row(MLA_WIDTH), row(DIFF_WIDTH), row(FOURIER_WIDTH),
            _resident((None, D_MODEL, D_MODEL), lambda i: (l, 0, 0)),
        ],
        out_specs=row(D_MODEL),
        compiler_params=_params("parallel"),
        name="out_proj",
    )(x, mod, o_mla, o_diff, o_four, w_out)


def _rope_tables():
    rows = SEQ // GRID_W
    pos_r = jnp.repeat(jnp.arange(rows), GRID_W)
    pos_c = jnp.tile(jnp.arange(GRID_W), rows)
    d = MLA_ROPE // 2
    half = d // 2
    inv = ROPE_THETA ** (-2.0 * jnp.arange(half, dtype=F32) / d)

    def tabs(pos):
        ang = pos.astype(F32)[:, None] * inv[None, :]
        return jnp.cos(ang), jnp.sin(ang)

    cr, sr = tabs(pos_r)
    cc, sc = tabs(pos_c)
    cos64 = jnp.concatenate([cr, cr, cc, cc], axis=-1)
    sin64 = jnp.concatenate([-sr, sr, -sc, sc], axis=-1)
    return jnp.tile(cos64, (1, 2)), jnp.tile(sin64, (1, 2))


def kernel(x, c, ctx, c_ctx, ada_w, ada_b, norm_g, ffn_wg, ffn_wu, ffn_wd, w_in, mla_q_norm, mla_kv_norm,
           mla_w_uq, mla_w_ukv, diff_lambda, diff_subln, w_out, final_norm):
    assert DEPTH == 2
    s_in = jnp.concatenate([c_ctx[None, :], c, jnp.zeros((MOD_ROWS - 1 - BATCH, D_MODEL), F32)], axis=0)
    w_in_b = w_in.astype(BF16)
    w_in_p = (w_in_b[..., :IN_KROPE],
              jnp.pad(w_in_b[..., IN_KROPE:IN_REST], ((0, 0), (0, 0), (0, ROPE_PAD - MLA_ROPE))),
              w_in_b[..., IN_REST:])
    w_uq_p = jnp.pad(mla_w_uq.astype(BF16).reshape(DEPTH, MLA_Q_RANK, MLA_HEADS, MLA_NOPE + MLA_ROPE),
                     ((0, 0), (0, 0), (0, 0), (0, MLA_QK_PAD - MLA_NOPE - MLA_ROPE))
                     ).reshape(DEPTH, MLA_Q_RANK, MLA_HEADS * MLA_QK_PAD)
    w_ukv = mla_w_ukv.astype(BF16)
    w_o = w_out.astype(BF16)
    rope_tabs = _rope_tables()
    ffn_w32 = (ffn_wg, ffn_wu, ffn_wd)
    ffn_w = {(0, 0): tuple(w[0, 0].astype(BF16) for w in ffn_w32)}

    mod = _ada_table(s_in, ada_w, ada_b).reshape(DEPTH, MOD_ROWS, 3, 3, D_MODEL)

    zl = x.reshape(LAT_ROWS, D_MODEL)
    zc = ctx.reshape(CTX_ROWS, D_MODEL)
    for l in range(DEPTH):
        last = l == DEPTH - 1
        lam_init = 0.8 - 0.6 * math.exp(-0.3 * l)
        proj = functools.partial(_proj, mod=mod, l=l, g=norm_g[l, 1], w_in=w_in_p, kv_norm=mla_kv_norm,
                                 w_ukv=w_ukv, q_norm=mla_q_norm, w_uq=w_uq_p)
        diff_attn = functools.partial(_diff_attention, diff_lambda, diff_subln, l, lam_init)

        def ffn(z, k, seg0, **kw):
            wg, wu, wd = ffn_w[l, k]
            return _ffn(z, mod, l, k, seg0, norm_g[l, 2 * k], wg, wu, wd, **kw)

        zl = ffn(zl, 0, 1)
        zc = ffn(zc, 0, 0)
        kl, vl, dkl, dvl, ql, dql, ul = proj(zl, seg0=1, rope_tabs=rope_tabs)
        if last:
            kc, vc, dkc, dvc = proj(zc, seg0=0, kv_only=True)
            o_mla = _mla_attention(ql, kc, vc, kl, vl)
            o_diff = diff_attn(dql, dkc, dvc, dkl, dvl)
        else:
            kc, vc, dkc, dvc, qc, dqc, uc = proj(zc, seg0=0)
            o_mla, *w_a = _mla_attention(ql, kc, vc, kl, vl,
                                         cast=[(w, 0, 1) for w in ffn_w32] + [(w, 1, 0) for w in ffn_w32])
            o_diff, *w_b = diff_attn(dql, dkc, dvc, dkl, dvl, cast=[(w, 1, 1) for w in ffn_w32])
            ffn_w.update({(0, 1): tuple(w_a[:3]), (1, 0): tuple(w_a[3:]), (1, 1): tuple(w_b)})
        zl = _out_proj(zl, mod, l, 1, o_mla, o_diff, _fourier(ul, SEQ), w_o)
        if last:
            zl = ffn(zl, 1, 1, final_g=final_norm)
        else:
            zc = _out_proj(zc, mod, l, 0, _mla_attention(qc, kc, vc), diff_attn(dqc, dkc, dvc),
                           _fourier(uc, CTX_LEN), w_o)
            zl = ffn(zl, 1, 1)
            zc = ffn(zc, 1, 0)
    return zl.reshape(BATCH, SEQ, D_MODEL)
```

```python
import functools
import math

import numpy as np
import jax
import jax.numpy as jnp
from jax import lax
from jax.experimental import pallas as pl
from jax.experimental.pallas import tpu as pltpu

D_MODEL = 2048
BATCH = 8
SEQ = 2048
DEPTH = 2
GRID_W = 64
CTX_LEN = 256
EPS = 1e-6
ROPE_THETA = 10000.0
D_FF = 5632
N_MOD = 9

MLA_HEADS = 8
MLA_Q_RANK = 512
MLA_KV_RANK = 256
MLA_NOPE = 128
MLA_ROPE = 64
MLA_V = 128
MLA_SCALE = (MLA_NOPE + MLA_ROPE) ** -0.5
MLA_QK_PAD = 256
MLA_V_PAD = 256

DIFF_HEADS = 4
DIFF_QK = 64
DIFF_V = 2 * DIFF_QK
DIFF_SCALE = DIFF_QK ** -0.5

FOURIER_GROUPS = 4
FOURIER_CH = 128
MLA_WIDTH = MLA_HEADS * MLA_V
DIFF_WIDTH = DIFF_HEADS * DIFF_V
FOURIER_WIDTH = FOURIER_GROUPS * FOURIER_CH

SEG = SEQ
LAT_ROWS = BATCH * SEQ
CTX_ROWS = BATCH * CTX_LEN
MOD_ROWS = 16

IN_KROPE = MLA_KV_RANK
IN_REST = MLA_KV_RANK + MLA_ROPE
ROPE_PAD = 128
R_DK = 0
R_DV = 512
R_CQ = 1024
R_DQ = 1536
R_U = 2048
R_COLS = 2560

VMEM_LIMIT = 56 * 1024 * 1024
VMEM_LIMIT_FFN = 63 * 1024 * 1024 + 512 * 1024
LOG2_E = math.log2(math.e)

F32 = jnp.float32
BF16 = jnp.bfloat16


def _silu(v):
    return v / (1.0 + jnp.exp(-v))


def _dot(a, b):
    return jnp.dot(a, b, preferred_element_type=F32)


def _dot_nt(a, b):
    return lax.dot_general(a, b, (((1,), (1,)), ((), ())), preferred_element_type=F32)


def _rms(v):
    return v * lax.rsqrt(jnp.mean(v * v, axis=-1, keepdims=True) + EPS)


def _row_chunks(n_rows, chunk):
    return [slice(r, r + chunk) for r in range(0, n_rows, chunk)]


def _modulate_rows(z_ref, g_ref, mod_ref, h_ref):
    gain = g_ref[...] * (1.0 + mod_ref[1:2, :])
    shift = mod_ref[0:1, :]
    for rows in _row_chunks(z_ref.shape[0], 16):
        h_ref[rows, :] = (_rms(z_ref[rows, :]) * gain + shift).astype(BF16)


def _params(*sem, vmem=VMEM_LIMIT):
    return pltpu.CompilerParams(dimension_semantics=sem, vmem_limit_bytes=vmem)


def _resident(shape, index_map):
    return pl.BlockSpec(shape, index_map, pipeline_mode=pl.Buffered(1))


def _mod_spec(l, seg0, tile, grp, n_grid):
    if n_grid == 1:
        return pl.BlockSpec((None, None, None, 3, D_MODEL), lambda i: (l, seg0 + i * tile // SEG, grp, 0, 0))
    return pl.BlockSpec((None, None, None, 3, D_MODEL), lambda i, f: (l, seg0 + i * tile // SEG, grp, 0, 0))


def _ada_kernel(s_ref, w_ref, b_ref, o_ref):
    s = _silu(s_ref[...])
    o_ref[...] = _dot(s.astype(BF16), w_ref[...].astype(BF16)) + b_ref[...]


def _ada_table(s_in, ada_w, ada_b):
    tn = 1024
    n_cols = N_MOD * D_MODEL
    return pl.pallas_call(
        _ada_kernel,
        out_shape=jax.ShapeDtypeStruct((DEPTH, MOD_ROWS, n_cols), F32),
        grid=(DEPTH, n_cols // tn),
        in_specs=[
            pl.BlockSpec((MOD_ROWS, D_MODEL), lambda l, n: (0, 0)),
            pl.BlockSpec((None, D_MODEL, tn), lambda l, n: (l, 0, n)),
            pl.BlockSpec((None, 1, tn), lambda l, n: (l, 0, n)),
        ],
        out_specs=pl.BlockSpec((None, MOD_ROWS, tn), lambda l, n: (l, 0, n)),
        compiler_params=_params("parallel", "parallel"),
        name="ada_table",
    )(s_in, ada_w, ada_b.reshape(DEPTH, 1, n_cols))


DOWN_COLS = 512


def _ffn_kernel(z_ref, mod_ref, g_ref, wg_ref, wu_ref, wd_ref, *rest, n_f, final):
    if final:
        fn_ref, o_ref, h_scr = rest
    else:
        o_ref, h_scr = rest
    f = pl.program_id(1)

    @pl.when(f == 0)
    def _():
        _modulate_rows(z_ref, g_ref, mod_ref, h_scr)
        o_ref[...] = jnp.zeros_like(o_ref)

    h = h_scr[...]
    a = (_silu(_dot(h, wg_ref[...])) * _dot(h, wu_ref[...])).astype(BF16)
    for c in range(0, D_MODEL, DOWN_COLS):
        o_ref[:, c:c + DOWN_COLS] += _dot(a, wd_ref[:, c:c + DOWN_COLS])

    @pl.when(f == n_f - 1)
    def _():
        half_gate = 0.5 * mod_ref[2:3, :]
        for rows in _row_chunks(o_ref.shape[0], 8):
            y = z_ref[rows, :] + half_gate * o_ref[rows, :]
            if final:
                y = _rms(y) * fn_ref[...]
            o_ref[rows, :] = y


def _ffn(z, mod, l, k, seg0, g, wg, wu, wd, *, final_g=None, tm=1024, tf=512):
    n_rows = z.shape[0]
    n_f = D_FF // tf
    final = final_g is not None
    in_specs = [
        pl.BlockSpec((tm, D_MODEL), lambda i, f: (i, 0)),
        _mod_spec(l, seg0, tm, 2 * k, 2),
        pl.BlockSpec((1, D_MODEL), lambda i, f: (0, 0)),
        pl.BlockSpec((D_MODEL, tf), lambda i, f: (0, f)),
        pl.BlockSpec((D_MODEL, tf), lambda i, f: (0, f)),
        pl.BlockSpec((tf, D_MODEL), lambda i, f: (f, 0)),
    ]
    args = [z, mod, g.reshape(1, D_MODEL), wg, wu, wd]
    if final:
        in_specs.append(pl.BlockSpec((1, D_MODEL), lambda i, f: (0, 0)))
        args.append(final_g.reshape(1, D_MODEL))
    return pl.pallas_call(
        functools.partial(_ffn_kernel, n_f=n_f, final=final),
        out_shape=jax.ShapeDtypeStruct((n_rows, D_MODEL), F32),
        grid=(n_rows // tm, n_f),
        in_specs=in_specs,
        out_specs=pl.BlockSpec((tm, D_MODEL), lambda i, f: (i, 0)),
        scratch_shapes=[pltpu.VMEM((tm, D_MODEL), BF16)],
        compiler_params=_params("parallel", "arbitrary", vmem=VMEM_LIMIT_FFN),
        name="ffn_final" if final else "ffn",
    )(*args)


def _rope(v, cos, sin, lo_half):
    swapped = jnp.where(lo_half, pltpu.roll(v, 112, 1), pltpu.roll(v, 16, 1))
    return v * cos + swapped * sin


def _proj_kernel(*refs, rope, kv_only):
    refs = list(refs)
    x_ref, mod_ref, g_ref, wckv_ref, wkr_ref, wrest_ref, kvn_ref, wukv_ref = refs[:8]
    del refs[:8]
    if not kv_only:
        qn_ref, wuq_ref = refs[:2]
        del refs[:2]
    if rope:
        cos_ref, sin_ref = refs[:2]
        del refs[:2]
    h_scr = refs.pop()
    if kv_only:
        kmla_ref, vmla_ref, dk_ref, dv_ref = refs
    else:
        kmla_ref, vmla_ref, dk_ref, dv_ref, qmla_ref, dq_ref, u_ref = refs

    _modulate_rows(x_ref, g_ref, mod_ref, h_scr)
    hb = h_scr[...]
    if rope:
        lane = lax.broadcasted_iota(jnp.int32, (1, 128), 1)
        rotate = functools.partial(_rope, cos=cos_ref[...], sin=sin_ref[...], lo_half=(lane % 32) < 16)
    else:
        rotate = lambda v: v
    q_scale = MLA_SCALE * LOG2_E
    dq_scale = DIFF_SCALE * LOG2_E

    ckv = _rms(_dot(hb, wckv_ref[...])) * kvn_ref[...]
    kv = _dot(ckv.astype(BF16), wukv_ref[...])
    k_rope = rotate(_dot(hb, wkr_ref[...])).astype(BF16)
    for h in range(MLA_HEADS):
        c0 = h * (MLA_NOPE + MLA_V)
        kmla_ref[:, h * MLA_QK_PAD:h * MLA_QK_PAD + MLA_NOPE] = kv[:, c0:c0 + MLA_NOPE].astype(BF16)
        kmla_ref[:, h * MLA_QK_PAD + MLA_NOPE:(h + 1) * MLA_QK_PAD] = k_rope
        vmla_ref[:, h * MLA_V_PAD:h * MLA_V_PAD + MLA_V] = kv[:, c0 + MLA_NOPE:c0 + MLA_NOPE + MLA_V].astype(BF16)
        vmla_ref[:, h * MLA_V_PAD + MLA_V:(h + 1) * MLA_V_PAD] = jnp.ones((kv.shape[0], MLA_V_PAD - MLA_V), BF16)
    dk = _dot(hb, wrest_ref[:, R_DK:R_DV])
    for j in range(DIFF_HEADS):
        dk_ref[:, j * 128:(j + 1) * 128] = rotate(dk[:, j * 128:(j + 1) * 128]).astype(BF16)
    dv_ref[...] = _dot(hb, wrest_ref[:, R_DV:R_CQ]).astype(BF16)
    if kv_only:
        return

    cq = _rms(_dot(hb, wrest_ref[:, R_CQ:R_DQ])) * qn_ref[...]
    q = _dot(cq.astype(BF16), wuq_ref[...])
    for h in range(MLA_HEADS):
        c0 = h * MLA_QK_PAD
        qmla_ref[:, c0:c0 + 128] = (q[:, c0:c0 + 128] * q_scale).astype(BF16)
        qmla_ref[:, c0 + 128:c0 + 256] = (rotate(q[:, c0 + 128:c0 + 256]) * q_scale).astype(BF16)
    dq = _dot(hb, wrest_ref[:, R_DQ:R_U])
    for j in range(DIFF_HEADS):
        dq_ref[:, j * 128:(j + 1) * 128] = (rotate(dq[:, j * 128:(j + 1) * 128]) * dq_scale).astype(BF16)
    u_ref[...] = _dot(hb, wrest_ref[:, R_U:R_COLS]).astype(BF16)


def _proj(x, mod, l, seg0, g, w_in, kv_norm, w_ukv, q_norm, w_uq, rope_tabs=None, *, kv_only=False, tr=512):
    n_rows = x.shape[0]
    row = lambda w: pl.BlockSpec((tr, w), lambda i: (i, 0))
    vec = lambda w: _resident((None, 1, w), lambda i: (l, 0, 0))
    mat = lambda r, c: _resident((None, r, c), lambda i: (l, 0, 0))
    in_specs = [row(D_MODEL), _mod_spec(l, seg0, tr, 1, 1), _resident((1, D_MODEL), lambda i: (0, 0)),
                mat(D_MODEL, MLA_KV_RANK), mat(D_MODEL, ROPE_PAD), mat(D_MODEL, R_COLS),
                vec(MLA_KV_RANK), mat(MLA_KV_RANK, MLA_HEADS * (MLA_NOPE + MLA_V))]
    args = [x, mod, g.reshape(1, D_MODEL), *w_in, kv_norm.reshape(DEPTH, 1, -1), w_ukv]
    widths = [MLA_HEADS * MLA_QK_PAD, MLA_HEADS * MLA_V_PAD, DIFF_WIDTH, DIFF_WIDTH]
    if not kv_only:
        in_specs += [vec(MLA_Q_RANK), mat(MLA_Q_RANK, MLA_HEADS * MLA_QK_PAD)]
        args += [q_norm.reshape(DEPTH, 1, -1), w_uq]
        widths += [MLA_HEADS * MLA_QK_PAD, DIFF_WIDTH, FOURIER_WIDTH]
    if rope_tabs is not None:
        per_seg = SEG // tr
        in_specs += [pl.BlockSpec((tr, 128), lambda i: (i % per_seg, 0))] * 2
        args += list(rope_tabs)
    return pl.pallas_call(
        functools.partial(_proj_kernel, rope=rope_tabs is not None, kv_only=kv_only),
        out_shape=[jax.ShapeDtypeStruct((n_rows, w), BF16) for w in widths],
        grid=(n_rows // tr,),
        in_specs=in_specs,
        out_specs=[row(w) for w in widths],
        scratch_shapes=[pltpu.VMEM((tr, D_MODEL), BF16)],
        compiler_params=_params("parallel"),
        name="in_proj",
    )(*args)


SUB = 256
CHUNK_VREGS = 36


def _half_rows(n_keys):
    half = 8
    while 2 * half * n_keys <= CHUNK_VREGS * 1024 and 4 * half <= SUB:
        half *= 2
    return half


def _attn_refs(refs, lat):
    if not lat:
        q_ref, kc_ref, vc_ref, o_ref = refs
        return q_ref, kc_ref, vc_ref, None, None, o_ref
    n_cast = (len(refs) - 6) // 2
    for src, dst in zip(refs[5:5 + n_cast], refs[6 + n_cast:]):
        dst[...] = src[...].astype(BF16)
    return (*refs[:5], refs[5 + n_cast])


def _cast_rows(n_rows, n_steps):
    return min(r for r in range(16, n_rows + 1, 16) if n_rows % r == 0 and n_rows // r <= n_steps)


MLA_AHEAD = 1
DIFF_AHEAD = 2


def _pipelined(items, scores, softmax, values, n_slots):
    ahead = n_slots - 1
    for k in range(min(ahead, len(items))):
        scores(k % n_slots, *items[k])
    for i, item in enumerate(items):
        if i + ahead < len(items):
            scores((i + ahead) % n_slots, *items[i + ahead])
        softmax(i % n_slots, i % 2, *item)
        if i > 0:
            values((i - 1) % 2, *items[i - 1])
    values((len(items) - 1) % 2, *items[-1])


def _store_scores(s_scr, slot, q, kc, kl):
    s_scr[slot, :, :CTX_LEN] = _dot_nt(q, kc)
    if kl is not None:
        s_scr[slot, :, CTX_LEN:] = _dot_nt(q, kl)


def _weighted_values(p_scr, slot, vc, vl):
    o = _dot(p_scr[slot, :, :CTX_LEN], vc)
    if vl is not None:
        o = o + _dot(p_scr[slot, :, CTX_LEN:], vl)
    return o


def _chunk_numerators(s):
    return jnp.exp2(s - jnp.max(s, axis=-1, keepdims=True))


def _mla_kernel(*refs, heads, n_sub, lat):
    *io, s_scr, p_scr = refs
    q_ref, kc_ref, vc_ref, kl_ref, vl_ref, o_ref = _attn_refs(io, lat)

    def scores(slot, hh, j):
        qk = slice(hh * MLA_QK_PAD, (hh + 1) * MLA_QK_PAD)
        _store_scores(s_scr, slot, q_ref[j * SUB:(j + 1) * SUB, qk], kc_ref[:, qk], kl_ref[:, qk] if lat else None)

    def softmax(slot, pslot, hh, j):
        half = _half_rows(s_scr.shape[-1])
        for r in range(0, SUB, 2 * half):
            e = [_chunk_numerators(s_scr[slot, r0:r0 + half, :]) for r0 in (r, r + half)]
            p_scr[pslot, r:r + 2 * half, :] = jnp.concatenate(e, axis=0).astype(BF16)

    def values(pslot, hh, j):
        vv = slice(hh * MLA_V_PAD, (hh + 1) * MLA_V_PAD)
        o = _weighted_values(p_scr, pslot, vc_ref[:, vv], vl_ref[:, vv] if lat else None)
        o_ref[j * SUB:(j + 1) * SUB, hh * MLA_V:(hh + 1) * MLA_V] = (o[:, :MLA_V] / o[:, MLA_V:]).astype(BF16)

    _pipelined([(hh, j) for hh in range(heads) for j in range(n_sub)], scores, softmax, values, s_scr.shape[0])


def _mla_scratch(n_keys):
    return [pltpu.VMEM((MLA_AHEAD + 1, SUB, n_keys), F32), pltpu.VMEM((2, SUB, n_keys), BF16)]


def _diff_kernel(lam_ref, sub_ref, *refs, heads, n_sub, lat, lam_init):
    *io, s1_scr, s2_scr, p_scr, d_scr = refs
    q_ref, kc_ref, vc_ref, kl_ref, vl_ref, o_ref = _attn_refs(io, lat)
    lf = lam_ref[...]
    lam = (jnp.exp(jnp.sum(lf[0:1] * lf[1:2], axis=-1, keepdims=True))
           - jnp.exp(jnp.sum(lf[2:3] * lf[3:4], axis=-1, keepdims=True)) + lam_init)
    first = lax.broadcasted_iota(jnp.int32, (1, 2 * DIFF_QK), 1) < DIFF_QK

    def scores(slot, hh, j):
        cols = slice(hh * DIFF_V, (hh + 1) * DIFF_V)
        q = q_ref[j * SUB:(j + 1) * SUB, cols]
        kc = kc_ref[:, cols]
        kl = kl_ref[:, cols] if lat else None
        _store_scores(s1_scr, slot, jnp.where(first, q, jnp.zeros_like(q)), kc, kl)
        _store_scores(s2_scr, slot, jnp.where(first, jnp.zeros_like(q), q), kc, kl)

    def softmax(slot, pslot, hh, j):
        half = _half_rows(s1_scr.shape[-1])
        for r in range(0, SUB, 2 * half):
            p = []
            for r0 in (r, r + half):
                e1 = _chunk_numerators(s1_scr[slot, r0:r0 + half, :])
                e2 = _chunk_numerators(s2_scr[slot, r0:r0 + half, :])
                d1 = jnp.sum(e1, axis=-1, keepdims=True)
                d2 = jnp.sum(e2, axis=-1, keepdims=True)
                p.append(e1 - e2 * (lam * d1 / d2))
                d_scr[pslot, r0:r0 + half, :] = jnp.broadcast_to(d1, (half, DIFF_V))
            p_scr[pslot, r:r + 2 * half, :] = jnp.concatenate(p, axis=0).astype(BF16)

    def values(pslot, hh, j):
        cols = slice(hh * DIFF_V, (hh + 1) * DIFF_V)
        o = _weighted_values(p_scr, pslot, vc_ref[:, cols], vl_ref[:, cols] if lat else None) / d_scr[pslot]
        o_ref[j * SUB:(j + 1) * SUB, cols] = (_rms(o) * sub_ref[...] * (1.0 - lam_init)).astype(BF16)

    _pipelined([(hh, j) for hh in range(heads) for j in range(n_sub)], scores, softmax, values, s1_scr.shape[0])


def _diff_scratch(n_keys):
    return [pltpu.VMEM((DIFF_AHEAD + 1, SUB, n_keys), F32), pltpu.VMEM((DIFF_AHEAD + 1, SUB, n_keys), F32),
            pltpu.VMEM((2, SUB, n_keys), BF16), pltpu.VMEM((2, SUB, DIFF_V), F32)]


def _attention(kernel_fn, scratch_fn, name, q, k_ctx, v_ctx, k_lat=None, v_lat=None, *, n_heads, w_qk, w_v, w_o,
               tq, extra_args=(), extra_specs=(), cast=()):
    if k_lat is not None:
        per_seg = SEQ // tq
        n_steps = BATCH * n_heads * per_seg
        in_specs = list(extra_specs) + [
            pl.BlockSpec((tq, w_qk), lambda b, h, t: (b * per_seg + t, h)),
            pl.BlockSpec((CTX_LEN, w_qk), lambda b, h, t: (b, h)),
            pl.BlockSpec((CTX_LEN, w_v), lambda b, h, t: (b, h)),
            pl.BlockSpec((SEQ, w_qk), lambda b, h, t: (b, h)),
            pl.BlockSpec((SEQ, w_v), lambda b, h, t: (b, h)),
        ]
        out_shape = [jax.ShapeDtypeStruct((LAT_ROWS, n_heads * w_o), BF16)]
        out_specs = [pl.BlockSpec((tq, w_o), lambda b, h, t: (b * per_seg + t, h))]
        for w, l, k in cast:
            n_r, n_c = w.shape[2:]
            rows = _cast_rows(n_r, n_steps)
            blk = lambda b, h, t, last=n_r // rows - 1: jnp.minimum((b * n_heads + h) * per_seg + t, last)
            in_specs.append(pl.BlockSpec((None, None, rows, n_c), lambda b, h, t, l=l, k=k, blk=blk: (l, k, blk(b, h, t), 0)))
            out_shape.append(jax.ShapeDtypeStruct((n_r, n_c), BF16))
            out_specs.append(pl.BlockSpec((rows, n_c), lambda b, h, t, blk=blk: (blk(b, h, t), 0)))
        out = pl.pallas_call(
            functools.partial(kernel_fn, heads=1, n_sub=tq // SUB, lat=True),
            out_shape=out_shape,
            grid=(BATCH, n_heads, per_seg),
            in_specs=in_specs,
            out_specs=out_specs,
            scratch_shapes=scratch_fn(CTX_LEN + SEQ),
            compiler_params=_params(*(("arbitrary",) * 3 if cast else ("parallel", "parallel", "arbitrary"))),
            name=name,
        )(*extra_args, q, k_ctx, v_ctx, k_lat, v_lat, *[w for w, _, _ in cast])
        return out if cast else out[0]
    in_specs = list(extra_specs) + [
        pl.BlockSpec((CTX_LEN, n_heads * w_qk), lambda b: (b, 0)),
        pl.BlockSpec((CTX_LEN, n_heads * w_qk), lambda b: (b, 0)),
        pl.BlockSpec((CTX_LEN, n_heads * w_v), lambda b: (b, 0)),
    ]
    return pl.pallas_call(
        functools.partial(kernel_fn, heads=n_heads, n_sub=CTX_LEN // SUB, lat=False),
        out_shape=jax.ShapeDtypeStruct((CTX_ROWS, n_heads * w_o), BF16),
        grid=(BATCH,),
        in_specs=in_specs,
        out_specs=pl.BlockSpec((CTX_LEN, n_heads * w_o), lambda b: (b, 0)),
        scratch_shapes=scratch_fn(CTX_LEN),
        compiler_params=_params("parallel"),
        name=name + "_ctx",
    )(*extra_args, q, k_ctx, v_ctx)


def _mla_attention(q, k_ctx, v_ctx, k_lat=None, v_lat=None, *, tq=2048, cast=()):
    return _attention(_mla_kernel, _mla_scratch, "mla_attention", q, k_ctx, v_ctx, k_lat, v_lat,
                      n_heads=MLA_HEADS, w_qk=MLA_QK_PAD, w_v=MLA_V_PAD, w_o=MLA_V, tq=tq, cast=cast)


def _diff_attention(diff_lambda, subln, l, lam_init, q, k_ctx, v_ctx, k_lat=None, v_lat=None, *, tq=2048, cast=()):
    layer = lambda *_: (l, 0, 0)
    extra_specs = (pl.BlockSpec((None, 4, DIFF_QK), layer), pl.BlockSpec((None, 1, DIFF_V), layer))
    return _attention(functools.partial(_diff_kernel, lam_init=lam_init), _diff_scratch, "diff_attention",
                      q, k_ctx, v_ctx, k_lat, v_lat,
                      n_heads=DIFF_HEADS, w_qk=2 * DIFF_QK, w_v=DIFF_V, w_o=DIFF_V, tq=tq,
                      extra_args=(diff_lambda, subln.reshape(DEPTH, 1, DIFF_V)), extra_specs=extra_specs, cast=cast)


def _dft_tables(n_pos):
    k = np.arange(n_pos, dtype=np.int64)
    ang = 2.0 * np.pi * ((k[:, None] * k[None, :]) % n_pos) / n_pos
    pos = np.concatenate([np.cos(ang), -np.sin(ang)], axis=1) / math.sqrt(n_pos)
    c = np.arange(FOURIER_CH, dtype=np.int64)
    angc = 2.0 * np.pi * ((c[:, None] * c[None, :]) % FOURIER_CH) / FOURIER_CH
    ch = np.concatenate([np.cos(angc), np.sin(angc)], axis=1) / math.sqrt(FOURIER_CH)
    return jnp.asarray(pos, dtype=BF16), jnp.asarray(ch, dtype=BF16)


def _fourier_kernel(u_ref, ch_ref, pos_ref, o_ref, rhs_scr, *, n_pos):
    @pl.when(pl.program_id(1) == 0)
    def _():
        for g in range(FOURIER_GROUPS):
            a = _dot(u_ref[:, g * FOURIER_CH:(g + 1) * FOURIER_CH], ch_ref[...])
            rhs_scr[0:n_pos, g * FOURIER_CH:(g + 1) * FOURIER_CH] = a[:, :FOURIER_CH].astype(BF16)
            rhs_scr[n_pos:2 * n_pos, g * FOURIER_CH:(g + 1) * FOURIER_CH] = a[:, FOURIER_CH:].astype(BF16)

    o_ref[...] = _dot(pos_ref[...], rhs_scr[...]).astype(BF16)


def _fourier(u, n_pos, *, tr=1024):
    pos_t, ch_t = _dft_tables(n_pos)
    tr = min(tr, n_pos)
    n_j = n_pos // tr
    return pl.pallas_call(
        functools.partial(_fourier_kernel, n_pos=n_pos),
        out_shape=jax.ShapeDtypeStruct((BATCH * n_pos, FOURIER_WIDTH), BF16),
        grid=(BATCH, n_j),
        in_specs=[
            pl.BlockSpec((n_pos, FOURIER_WIDTH), lambda b, j: (b, 0)),
            pl.BlockSpec((FOURIER_CH, 2 * FOURIER_CH), lambda b, j: (0, 0)),
            pl.BlockSpec((tr, 2 * n_pos), lambda b, j: (j, 0)),
        ],
        out_specs=pl.BlockSpec((tr, FOURIER_WIDTH), lambda b, j: (b * n_j + j, 0)),
        scratch_shapes=[pltpu.VMEM((2 * n_pos, FOURIER_WIDTH), BF16)],
        compiler_params=_params("parallel", "arbitrary"),
        name="fourier_mix",
    )(u, ch_t, pos_t)


def _out_kernel(x_ref, mod_ref, om_ref, od_ref, of_ref, w_ref, o_ref):
    y = (_dot(om_ref[...], w_ref[0:MLA_WIDTH, :])
         + _dot(od_ref[...], w_ref[MLA_WIDTH:MLA_WIDTH + DIFF_WIDTH, :])
         + _dot(of_ref[...], w_ref[MLA_WIDTH + DIFF_WIDTH:, :]))
    o_ref[...] = x_ref[...] + mod_ref[2:3, :] * y


def _out_proj(x, mod, l, seg0, o_mla, o_diff, o_four, w_out, *, tr=512):
    n_rows = x.shape[0]
    row = lambda w: pl.BlockSpec((tr, w), lambda i: (i, 0))
    return pl.pallas_call(
        _out_kernel,
        out_shape=jax.ShapeDtypeStruct((n_rows, D_MODEL), F32),
        grid=(n_rows // tr,),
        in_specs=[
            row(D_MODEL), _mod_spec(l, seg0, tr, 1, 1),
            row(MLA_WIDTH), row(DIFF_WIDTH), row(FOURIER_WIDTH),
            _resident((None, D_MODEL, D_MODEL), lambda i: (l, 0, 0)),
        ],
        out_specs=row(D_MODEL),
        compiler_params=_params("parallel"),
        name="out_proj",
    )(x, mod, o_mla, o_diff, o_four, w_out)


def _rope_tables():
    rows = SEQ // GRID_W
    pos_r = jnp.repeat(jnp.arange(rows), GRID_W)
    pos_c = jnp.tile(jnp.arange(GRID_W), rows)
    d = MLA_ROPE // 2
    half = d // 2
    inv = ROPE_THETA ** (-2.0 * jnp.arange(half, dtype=F32) / d)

    def tabs(pos):
        ang = pos.astype(F32)[:, None] * inv[None, :]
        return jnp.cos(ang), jnp.sin(ang)

    cr, sr = tabs(pos_r)
    cc, sc = tabs(pos_c)
    cos64 = jnp.concatenate([cr, cr, cc, cc], axis=-1)
    sin64 = jnp.concatenate([-sr, sr, -sc, sc], axis=-1)
    return jnp.tile(cos64, (1, 2)), jnp.tile(sin64, (1, 2))


def kernel(x, c, ctx, c_ctx, ada_w, ada_b, norm_g, ffn_wg, ffn_wu, ffn_wd, w_in, mla_q_norm, mla_kv_norm,
           mla_w_uq, mla_w_ukv, diff_lambda, diff_subln, w_out, final_norm):
    assert DEPTH == 2
    s_in = jnp.concatenate([c_ctx[None, :], c, jnp.zeros((MOD_ROWS - 1 - BATCH, D_MODEL), F32)], axis=0)
    w_in_b = w_in.astype(BF16)
    w_in_p = (w_in_b[..., :IN_KROPE],
              jnp.pad(w_in_b[..., IN_KROPE:IN_REST], ((0, 0), (0, 0), (0, ROPE_PAD - MLA_ROPE))),
              w_in_b[..., IN_REST:])
    w_uq_p = jnp.pad(mla_w_uq.astype(BF16).reshape(DEPTH, MLA_Q_RANK, MLA_HEADS, MLA_NOPE + MLA_ROPE),
                     ((0, 0), (0, 0), (0, 0), (0, MLA_QK_PAD - MLA_NOPE - MLA_ROPE))
                     ).reshape(DEPTH, MLA_Q_RANK, MLA_HEADS * MLA_QK_PAD)
    w_ukv = mla_w_ukv.astype(BF16)
    w_o = w_out.astype(BF16)
    rope_tabs = _rope_tables()
    ffn_w32 = (ffn_wg, ffn_wu, ffn_wd)
    ffn_w = {(0, 0): tuple(w[0, 0].astype(BF16) for w in ffn_w32)}

    mod = _ada_table(s_in, ada_w, ada_b).reshape(DEPTH, MOD_ROWS, 3, 3, D_MODEL)

    zl = x.reshape(LAT_ROWS, D_MODEL)
    zc = ctx.reshape(CTX_ROWS, D_MODEL)
    for l in range(DEPTH):
        last = l == DEPTH - 1
        lam_init = 0.8 - 0.6 * math.exp(-0.3 * l)
        proj = functools.partial(_proj, mod=mod, l=l, g=norm_g[l, 1], w_in=w_in_p, kv_norm=mla_kv_norm,
                                 w_ukv=w_ukv, q_norm=mla_q_norm, w_uq=w_uq_p)
        diff_attn = functools.partial(_diff_attention, diff_lambda, diff_subln, l, lam_init)

        def ffn(z, k, seg0, **kw):
            wg, wu, wd = ffn_w[l, k]
            return _ffn(z, mod, l, k, seg0, norm_g[l, 2 * k], wg, wu, wd, **kw)

        zl = ffn(zl, 0, 1)
        zc = ffn(zc, 0, 0)
        kl, vl, dkl, dvl, ql, dql, ul = proj(zl, seg0=1, rope_tabs=rope_tabs)
        if last:
            kc, vc, dkc, dvc = proj(zc, seg0=0, kv_only=True)
            o_mla = _mla_attention(ql, kc, vc, kl, vl)
            o_diff = diff_attn(dql, dkc, dvc, dkl, dvl)
        else:
            kc, vc, dkc, dvc, qc, dqc, uc = proj(zc, seg0=0)
            o_mla, *w_a = _mla_attention(ql, kc, vc, kl, vl,
                                         cast=[(w, 0, 1) for w in ffn_w32] + [(w, 1, 0) for w in ffn_w32])
            o_diff, *w_b = diff_attn(dql, dkc, dvc, dkl, dvl, cast=[(w, 1, 1) for w in ffn_w32])
            ffn_w.update({(0, 1): tuple(w_a[:3]), (1, 0): tuple(w_a[3:]), (1, 1): tuple(w_b)})
        zl = _out_proj(zl, mod, l, 1, o_mla, o_diff, _fourier(ul, SEQ), w_o)
        if last:
            zl = ffn(zl, 1, 1, final_g=final_norm)
        else:
            zc = _out_proj(zc, mod, l, 0, _mla_attention(qc, kc, vc), diff_attn(dqc, dkc, dvc),
                           _fourier(uc, CTX_LEN), w_o)
            zl = ffn(zl, 1, 1)
            zc = ffn(zc, 1, 0)
    return zl.reshape(BATCH, SEQ, D_MODEL)
```

```python
import functools
import math

import numpy as np
import jax
import jax.numpy as jnp
from jax import lax
from jax.experimental import pallas as pl
from jax.experimental.pallas import tpu as pltpu

D_MODEL = 2048
BATCH = 8
SEQ = 2048
DEPTH = 2
GRID_W = 64
CTX_LEN = 256
EPS = 1e-6
ROPE_THETA = 10000.0
D_FF = 5632
N_MOD = 9

MLA_HEADS = 8
MLA_Q_RANK = 512
MLA_KV_RANK = 256
MLA_NOPE = 128
MLA_ROPE = 64
MLA_V = 128
MLA_SCALE = (MLA_NOPE + MLA_ROPE) ** -0.5
MLA_QK_PAD = 256
MLA_V_PAD = 256

DIFF_HEADS = 4
DIFF_QK = 64
DIFF_V = 2 * DIFF_QK
DIFF_SCALE = DIFF_QK ** -0.5

FOURIER_GROUPS = 4
FOURIER_CH = 128
MLA_WIDTH = MLA_HEADS * MLA_V
DIFF_WIDTH = DIFF_HEADS * DIFF_V
FOURIER_WIDTH = FOURIER_GROUPS * FOURIER_CH

SEG = SEQ
LAT_ROWS = BATCH * SEQ
CTX_ROWS = BATCH * CTX_LEN
MOD_ROWS = 16

IN_KROPE = MLA_KV_RANK
IN_REST = MLA_KV_RANK + MLA_ROPE
ROPE_PAD = 128
R_DK = 0
R_DV = 512
R_CQ = 1024
R_DQ = 1536
R_U = 2048
R_COLS = 2560

VMEM_LIMIT = 56 * 1024 * 1024
VMEM_LIMIT_FFN = 63 * 1024 * 1024 + 512 * 1024
LOG2_E = math.log2(math.e)

F32 = jnp.float32
BF16 = jnp.bfloat16


def _silu(v):
    return v / (1.0 + jnp.exp(-v))


def _dot(a, b):
    return jnp.dot(a, b, preferred_element_type=F32)


def _dot_nt(a, b):
    return lax.dot_general(a, b, (((1,), (1,)), ((), ())), preferred_element_type=F32)


def _rms(v):
    return v * lax.rsqrt(jnp.mean(v * v, axis=-1, keepdims=True) + EPS)


def _row_chunks(n_rows, chunk):
    return [slice(r, r + chunk) for r in range(0, n_rows, chunk)]


def _modulate_rows(z_ref, g_ref, mod_ref, h_ref):
    gain = g_ref[...] * (1.0 + mod_ref[1:2, :])
    shift = mod_ref[0:1, :]
    for rows in _row_chunks(z_ref.shape[0], 16):
        h_ref[rows, :] = (_rms(z_ref[rows, :]) * gain + shift).astype(BF16)


def _params(*sem, vmem=VMEM_LIMIT):
    return pltpu.CompilerParams(dimension_semantics=sem, vmem_limit_bytes=vmem)


def _resident(shape, index_map):
    return pl.BlockSpec(shape, index_map, pipeline_mode=pl.Buffered(1))


def _mod_spec(l, seg0, tile, grp, n_grid):
    if n_grid == 1:
        return pl.BlockSpec((None, None, None, 3, D_MODEL), lambda i: (l, seg0 + i * tile // SEG, grp, 0, 0))
    return pl.BlockSpec((None, None, None, 3, D_MODEL), lambda i, f: (l, seg0 + i * tile // SEG, grp, 0, 0))


def _ada_kernel(s_ref, w_ref, b_ref, o_ref):
    s = _silu(s_ref[...])
    o_ref[...] = _dot(s.astype(BF16), w_ref[...].astype(BF16)) + b_ref[...]


def _ada_table(s_in, ada_w, ada_b):
    tn = 1024
    n_cols = N_MOD * D_MODEL
    return pl.pallas_call(
        _ada_kernel,
        out_shape=jax.ShapeDtypeStruct((DEPTH, MOD_ROWS, n_cols), F32),
        grid=(DEPTH, n_cols // tn),
        in_specs=[
            pl.BlockSpec((MOD_ROWS, D_MODEL), lambda l, n: (0, 0)),
            pl.BlockSpec((None, D_MODEL, tn), lambda l, n: (l, 0, n)),
            pl.BlockSpec((None, 1, tn), lambda l, n: (l, 0, n)),
        ],
        out_specs=pl.BlockSpec((None, MOD_ROWS, tn), lambda l, n: (l, 0, n)),
        compiler_params=_params("parallel", "parallel"),
        name="ada_table",
    )(s_in, ada_w, ada_b.reshape(DEPTH, 1, n_cols))


DOWN_COLS = 512


def _ffn_kernel(z_ref, mod_ref, g_ref, wg_ref, wu_ref, wd_ref, *rest, n_f, final):
    if final:
        fn_ref, o_ref, h_scr = rest
    else:
        o_ref, h_scr = rest
    f = pl.program_id(1)

    @pl.when(f == 0)
    def _():
        _modulate_rows(z_ref, g_ref, mod_ref, h_scr)
        o_ref[...] = jnp.zeros_like(o_ref)

    h = h_scr[...]
    g = _dot(h, wg_ref[...])
    u = _dot(h, wu_ref[...])
    half = g.shape[1] // 2
    a0 = (_silu(g[:, :half]) * u[:, :half]).astype(BF16)
    a1 = (_silu(g[:, half:]) * u[:, half:]).astype(BF16)
    for c in range(0, D_MODEL, DOWN_COLS):
        o_ref[:, c:c + DOWN_COLS] += (_dot(a0, wd_ref[:half, c:c + DOWN_COLS])
                                      + _dot(a1, wd_ref[half:, c:c + DOWN_COLS]))

    @pl.when(f == n_f - 1)
    def _():
        half_gate = 0.5 * mod_ref[2:3, :]
        for rows in _row_chunks(o_ref.shape[0], 8):
            y = z_ref[rows, :] + half_gate * o_ref[rows, :]
            if final:
                y = _rms(y) * fn_ref[...]
            o_ref[rows, :] = y


def _ffn(z, mod, l, k, seg0, g, wg, wu, wd, *, final_g=None, tm=1024, tf=512):
    n_rows = z.shape[0]
    n_f = D_FF // tf
    final = final_g is not None
    in_specs = [
        pl.BlockSpec((tm, D_MODEL), lambda i, f: (i, 0)),
        _mod_spec(l, seg0, tm, 2 * k, 2),
        pl.BlockSpec((1, D_MODEL), lambda i, f: (0, 0)),
        pl.BlockSpec((D_MODEL, tf), lambda i, f: (0, f)),
        pl.BlockSpec((D_MODEL, tf), lambda i, f: (0, f)),
        pl.BlockSpec((tf, D_MODEL), lambda i, f: (f, 0)),
    ]
    args = [z, mod, g.reshape(1, D_MODEL), wg, wu, wd]
    if final:
        in_specs.append(pl.BlockSpec((1, D_MODEL), lambda i, f: (0, 0)))
        args.append(final_g.reshape(1, D_MODEL))
    return pl.pallas_call(
        functools.partial(_ffn_kernel, n_f=n_f, final=final),
        out_shape=jax.ShapeDtypeStruct((n_rows, D_MODEL), F32),
        grid=(n_rows // tm, n_f),
        in_specs=in_specs,
        out_specs=pl.BlockSpec((tm, D_MODEL), lambda i, f: (i, 0)),
        scratch_shapes=[pltpu.VMEM((tm, D_MODEL), BF16)],
        compiler_params=_params("parallel", "arbitrary", vmem=VMEM_LIMIT_FFN),
        name="ffn_final" if final else "ffn",
    )(*args)


def _rope(v, cos, sin, lo_half):
    swapped = jnp.where(lo_half, pltpu.roll(v, 112, 1), pltpu.roll(v, 16, 1))
    return v * cos + swapped * sin


def _proj_kernel(*refs, rope, kv_only):
    refs = list(refs)
    x_ref, mod_ref, g_ref, wckv_ref, wkr_ref, wrest_ref, kvn_ref, wukv_ref = refs[:8]
    del refs[:8]
    if not kv_only:
        qn_ref, wuq_ref = refs[:2]
        del refs[:2]
    if rope:
        cos_ref, sin_ref = refs[:2]
        del refs[:2]
    h_scr = refs.pop()
    if kv_only:
        kmla_ref, vmla_ref, dk_ref, dv_ref = refs
    else:
        kmla_ref, vmla_ref, dk_ref, dv_ref, qmla_ref, dq_ref, u_ref = refs

    _modulate_rows(x_ref, g_ref, mod_ref, h_scr)
    hb = h_scr[...]
    if rope:
        lane = lax.broadcasted_iota(jnp.int32, (1, 128), 1)
        rotate = functools.partial(_rope, cos=cos_ref[...], sin=sin_ref[...], lo_half=(lane % 32) < 16)
    else:
        rotate = lambda v: v
    q_scale = MLA_SCALE * LOG2_E
    dq_scale = DIFF_SCALE * LOG2_E

    ckv = _dot(hb, wckv_ref[...])
    cq = None if kv_only else _dot(hb, wrest_ref[:, R_CQ:R_DQ])
    k_rope = _dot(hb, wkr_ref[...])
    dk = _dot(hb, wrest_ref[:, R_DK:R_DV])
    dv = _dot(hb, wrest_ref[:, R_DV:R_CQ])
    if not kv_only:
        dq = _dot(hb, wrest_ref[:, R_DQ:R_U])
        u = _dot(hb, wrest_ref[:, R_U:R_COLS])
    kv = _dot((_rms(ckv) * kvn_ref[...]).astype(BF16), wukv_ref[...])
    if not kv_only:
        q = _dot((_rms(cq) * qn_ref[...]).astype(BF16), wuq_ref[...])

    k_rope = rotate(k_rope).astype(BF16)
    for j in range(DIFF_HEADS):
        dk_ref[:, j * 128:(j + 1) * 128] = rotate(dk[:, j * 128:(j + 1) * 128]).astype(BF16)
    dv_ref[...] = dv.astype(BF16)
    for h in range(MLA_HEADS):
        c0 = h * (MLA_NOPE + MLA_V)
        kmla_ref[:, h * MLA_QK_PAD:h * MLA_QK_PAD + MLA_NOPE] = kv[:, c0:c0 + MLA_NOPE].astype(BF16)
        kmla_ref[:, h * MLA_QK_PAD + MLA_NOPE:(h + 1) * MLA_QK_PAD] = k_rope
        vmla_ref[:, h * MLA_V_PAD:h * MLA_V_PAD + MLA_V] = kv[:, c0 + MLA_NOPE:c0 + MLA_NOPE + MLA_V].astype(BF16)
        vmla_ref[:, h * MLA_V_PAD + MLA_V:(h + 1) * MLA_V_PAD] = jnp.ones((kv.shape[0], MLA_V_PAD - MLA_V), BF16)
    if kv_only:
        return

    for j in range(DIFF_HEADS):
        dq_ref[:, j * 128:(j + 1) * 128] = (rotate(dq[:, j * 128:(j + 1) * 128]) * dq_scale).astype(BF16)
    u_ref[...] = u.astype(BF16)
    for h in range(MLA_HEADS):
        c0 = h * MLA_QK_PAD
        qmla_ref[:, c0:c0 + 128] = (q[:, c0:c0 + 128] * q_scale).astype(BF16)
        qmla_ref[:, c0 + 128:c0 + 256] = (rotate(q[:, c0 + 128:c0 + 256]) * q_scale).astype(BF16)


def _proj(x, mod, l, seg0, g, w_in, kv_norm, w_ukv, q_norm, w_uq, rope_tabs=None, *, kv_only=False, tr=512):
    n_rows = x.shape[0]
    row = lambda w: pl.BlockSpec((tr, w), lambda i: (i, 0))
    vec = lambda w: _resident((None, 1, w), lambda i: (l, 0, 0))
    mat = lambda r, c: _resident((None, r, c), lambda i: (l, 0, 0))
    in_specs = [row(D_MODEL), _mod_spec(l, seg0, tr, 1, 1), _resident((1, D_MODEL), lambda i: (0, 0)),
                mat(D_MODEL, MLA_KV_RANK), mat(D_MODEL, ROPE_PAD), mat(D_MODEL, R_COLS),
                vec(MLA_KV_RANK), mat(MLA_KV_RANK, MLA_HEADS * (MLA_NOPE + MLA_V))]
    args = [x, mod, g.reshape(1, D_MODEL), *w_in, kv_norm.reshape(DEPTH, 1, -1), w_ukv]
    widths = [MLA_HEADS * MLA_QK_PAD, MLA_HEADS * MLA_V_PAD, DIFF_WIDTH, DIFF_WIDTH]
    if not kv_only:
        in_specs += [vec(MLA_Q_RANK), mat(MLA_Q_RANK, MLA_HEADS * MLA_QK_PAD)]
        args += [q_norm.reshape(DEPTH, 1, -1), w_uq]
        widths += [MLA_HEADS * MLA_QK_PAD, DIFF_WIDTH, FOURIER_WIDTH]
    if rope_tabs is not None:
        per_seg = SEG // tr
        in_specs += [pl.BlockSpec((tr, 128), lambda i: (i % per_seg, 0))] * 2
        args += list(rope_tabs)
    return pl.pallas_call(
        functools.partial(_proj_kernel, rope=rope_tabs is not None, kv_only=kv_only),
        out_shape=[jax.ShapeDtypeStruct((n_rows, w), BF16) for w in widths],
        grid=(n_rows // tr,),
        in_specs=in_specs,
        out_specs=[row(w) for w in widths],
        scratch_shapes=[pltpu.VMEM((tr, D_MODEL), BF16)],
        compiler_params=_params("parallel"),
        name="in_proj",
    )(*args)


SUB = 256
CHUNK_VREGS = 36


def _half_rows(n_keys):
    half = 8
    while 2 * half * n_keys <= CHUNK_VREGS * 1024 and 4 * half <= SUB:
        half *= 2
    return half


def _attn_refs(refs, lat):
    if not lat:
        q_ref, kc_ref, vc_ref, o_ref = refs
        return q_ref, kc_ref, vc_ref, None, None, o_ref
    n_cast = (len(refs) - 6) // 2
    for src, dst in zip(refs[5:5 + n_cast], refs[6 + n_cast:]):
        dst[...] = src[...].astype(BF16)
    return (*refs[:5], refs[5 + n_cast])


def _cast_rows(n_rows, n_steps):
    return min(r for r in range(16, n_rows + 1, 16) if n_rows % r == 0 and n_rows // r <= n_steps)


MLA_AHEAD = 1
DIFF_AHEAD = 2


def _pipelined(items, scores, softmax, values, n_slots):
    ahead = n_slots - 1
    for k in range(min(ahead, len(items))):
        scores(k % n_slots, *items[k])
    for i, item in enumerate(items):
        if i + ahead < len(items):
            scores((i + ahead) % n_slots, *items[i + ahead])
        softmax(i % n_slots, i % 2, *item)
        if i > 0:
            values((i - 1) % 2, *items[i - 1])
    values((len(items) - 1) % 2, *items[-1])


def _store_scores(s_scr, slot, q, kc, kl):
    s_scr[slot, :, :CTX_LEN] = _dot_nt(q, kc)
    if kl is not None:
        s_scr[slot, :, CTX_LEN:] = _dot_nt(q, kl)


def _weighted_values(p_scr, slot, vc, vl):
    o = _dot(p_scr[slot, :, :CTX_LEN], vc)
    if vl is not None:
        o = o + _dot(p_scr[slot, :, CTX_LEN:], vl)
    return o


def _chunk_numerators(s):
    return jnp.exp2(s - jnp.max(s, axis=-1, keepdims=True))


def _mla_kernel(*refs, heads, n_sub, lat):
    *io, s_scr, p_scr = refs
    q_ref, kc_ref, vc_ref, kl_ref, vl_ref, o_ref = _attn_refs(io, lat)

    def scores(slot, hh, j):
        qk = slice(hh * MLA_QK_PAD, (hh + 1) * MLA_QK_PAD)
        _store_scores(s_scr, slot, q_ref[j * SUB:(j + 1) * SUB, qk], kc_ref[:, qk], kl_ref[:, qk] if lat else None)

    def softmax(slot, pslot, hh, j):
        half = _half_rows(s_scr.shape[-1])
        for r in range(0, SUB, 2 * half):
            e = [_chunk_numerators(s_scr[slot, r0:r0 + half, :]) for r0 in (r, r + half)]
            p_scr[pslot, r:r + 2 * half, :] = jnp.concatenate(e, axis=0).astype(BF16)

    def values(pslot, hh, j):
        vv = slice(hh * MLA_V_PAD, (hh + 1) * MLA_V_PAD)
        o = _weighted_values(p_scr, pslot, vc_ref[:, vv], vl_ref[:, vv] if lat else None)
        o_ref[j * SUB:(j + 1) * SUB, hh * MLA_V:(hh + 1) * MLA_V] = (o[:, :MLA_V] / o[:, MLA_V:]).astype(BF16)

    _pipelined([(hh, j) for hh in range(heads) for j in range(n_sub)], scores, softmax, values, s_scr.shape[0])


def _mla_scratch(n_keys):
    return [pltpu.VMEM((MLA_AHEAD + 1, SUB, n_keys), F32), pltpu.VMEM((2, SUB, n_keys), BF16)]


def _diff_kernel(lam_ref, sub_ref, *refs, heads, n_sub, lat, lam_init):
    *io, s1_scr, s2_scr, p_scr, d_scr = refs
    q_ref, kc_ref, vc_ref, kl_ref, vl_ref, o_ref = _attn_refs(io, lat)
    lf = lam_ref[...]
    lam = (jnp.exp(jnp.sum(lf[0:1] * lf[1:2], axis=-1, keepdims=True))
           - jnp.exp(jnp.sum(lf[2:3] * lf[3:4], axis=-1, keepdims=True)) + lam_init)
    first = lax.broadcasted_iota(jnp.int32, (1, 2 * DIFF_QK), 1) < DIFF_QK

    def scores(slot, hh, j):
        cols = slice(hh * DIFF_V, (hh + 1) * DIFF_V)
        q = q_ref[j * SUB:(j + 1) * SUB, cols]
        kc = kc_ref[:, cols]
        kl = kl_ref[:, cols] if lat else None
        _store_scores(s1_scr, slot, jnp.where(first, q, jnp.zeros_like(q)), kc, kl)
        _store_scores(s2_scr, slot, jnp.where(first, jnp.zeros_like(q), q), kc, kl)

    def softmax(slot, pslot, hh, j):
        half = _half_rows(s1_scr.shape[-1])
        for r in range(0, SUB, 2 * half):
            p = []
            for r0 in (r, r + half):
                e1 = _chunk_numerators(s1_scr[slot, r0:r0 + half, :])
                e2 = _chunk_numerators(s2_scr[slot, r0:r0 + half, :])
                d1 = jnp.sum(e1, axis=-1, keepdims=True)
                d2 = jnp.sum(e2, axis=-1, keepdims=True)
                p.append(e1 - e2 * (lam * d1 / d2))
                d_scr[pslot, r0:r0 + half, :] = jnp.broadcast_to(d1, (half, DIFF_V))
            p_scr[pslot, r:r + 2 * half, :] = jnp.concatenate(p, axis=0).astype(BF16)

    def values(pslot, hh, j):
        cols = slice(hh * DIFF_V, (hh + 1) * DIFF_V)
        o = _weighted_values(p_scr, pslot, vc_ref[:, cols], vl_ref[:, cols] if lat else None) / d_scr[pslot]
        o_ref[j * SUB:(j + 1) * SUB, cols] = (_rms(o) * sub_ref[...] * (1.0 - lam_init)).astype(BF16)

    _pipelined([(hh, j) for hh in range(heads) for j in range(n_sub)], scores, softmax, values, s1_scr.shape[0])


def _diff_scratch(n_keys):
    return [pltpu.VMEM((DIFF_AHEAD + 1, SUB, n_keys), F32), pltpu.VMEM((DIFF_AHEAD + 1, SUB, n_keys), F32),
            pltpu.VMEM((2, SUB, n_keys), BF16), pltpu.VMEM((2, SUB, DIFF_V), F32)]


def _attention(kernel_fn, scratch_fn, name, q, k_ctx, v_ctx, k_lat=None, v_lat=None, *, n_heads, w_qk, w_v, w_o,
               tq, extra_args=(), extra_specs=(), cast=()):
    if k_lat is not None:
        per_seg = SEQ // tq
        n_steps = BATCH * n_heads * per_seg
        in_specs = list(extra_specs) + [
            pl.BlockSpec((tq, w_qk), lambda b, h, t: (b * per_seg + t, h)),
            pl.BlockSpec((CTX_LEN, w_qk), lambda b, h, t: (b, h)),
            pl.BlockSpec((CTX_LEN, w_v), lambda b, h, t: (b, h)),
            pl.BlockSpec((SEQ, w_qk), lambda b, h, t: (b, h)),
            pl.BlockSpec((SEQ, w_v), lambda b, h, t: (b, h)),
        ]
        out_shape = [jax.ShapeDtypeStruct((LAT_ROWS, n_heads * w_o), BF16)]
        out_specs = [pl.BlockSpec((tq, w_o), lambda b, h, t: (b * per_seg + t, h))]
        for w, l, k in cast:
            n_r, n_c = w.shape[2:]
            rows = _cast_rows(n_r, n_steps)
            blk = lambda b, h, t, last=n_r // rows - 1: jnp.minimum((b * n_heads + h) * per_seg + t, last)
            in_specs.append(pl.BlockSpec((None, None, rows, n_c), lambda b, h, t, l=l, k=k, blk=blk: (l, k, blk(b, h, t), 0)))
            out_shape.append(jax.ShapeDtypeStruct((n_r, n_c), BF16))
            out_specs.append(pl.BlockSpec((rows, n_c), lambda b, h, t, blk=blk: (blk(b, h, t), 0)))
        out = pl.pallas_call(
            functools.partial(kernel_fn, heads=1, n_sub=tq // SUB, lat=True),
            out_shape=out_shape,
            grid=(BATCH, n_heads, per_seg),
            in_specs=in_specs,
            out_specs=out_specs,
            scratch_shapes=scratch_fn(CTX_LEN + SEQ),
            compiler_params=_params(*(("arbitrary",) * 3 if cast else ("parallel", "parallel", "arbitrary"))),
            name=name,
        )(*extra_args, q, k_ctx, v_ctx, k_lat, v_lat, *[w for w, _, _ in cast])
        return out if cast else out[0]
    in_specs = list(extra_specs) + [
        pl.BlockSpec((CTX_LEN, n_heads * w_qk), lambda b: (b, 0)),
        pl.BlockSpec((CTX_LEN, n_heads * w_qk), lambda b: (b, 0)),
        pl.BlockSpec((CTX_LEN, n_heads * w_v), lambda b: (b, 0)),
    ]
    return pl.pallas_call(
        functools.partial(kernel_fn, heads=n_heads, n_sub=CTX_LEN // SUB, lat=False),
        out_shape=jax.ShapeDtypeStruct((CTX_ROWS, n_heads * w_o), BF16),
        grid=(BATCH,),
        in_specs=in_specs,
        out_specs=pl.BlockSpec((CTX_LEN, n_heads * w_o), lambda b: (b, 0)),
        scratch_shapes=scratch_fn(CTX_LEN),
        compiler_params=_params("parallel"),
        name=name + "_ctx",
    )(*extra_args, q, k_ctx, v_ctx)


def _mla_attention(q, k_ctx, v_ctx, k_lat=None, v_lat=None, *, tq=2048, cast=()):
    return _attention(_mla_kernel, _mla_scratch, "mla_attention", q, k_ctx, v_ctx, k_lat, v_lat,
                      n_heads=MLA_HEADS, w_qk=MLA_QK_PAD, w_v=MLA_V_PAD, w_o=MLA_V, tq=tq, cast=cast)


def _diff_attention(diff_lambda, subln, l, lam_init, q, k_ctx, v_ctx, k_lat=None, v_lat=None, *, tq=2048, cast=()):
    layer = lambda *_: (l, 0, 0)
    extra_specs = (pl.BlockSpec((None, 4, DIFF_QK), layer), pl.BlockSpec((None, 1, DIFF_V), layer))
    return _attention(functools.partial(_diff_kernel, lam_init=lam_init), _diff_scratch, "diff_attention",
                      q, k_ctx, v_ctx, k_lat, v_lat,
                      n_heads=DIFF_HEADS, w_qk=2 * DIFF_QK, w_v=DIFF_V, w_o=DIFF_V, tq=tq,
                      extra_args=(diff_lambda, subln.reshape(DEPTH, 1, DIFF_V)), extra_specs=extra_specs, cast=cast)


def _dft_tables(n_pos):
    k = np.arange(n_pos, dtype=np.int64)
    ang = 2.0 * np.pi * ((k[:, None] * k[None, :]) % n_pos) / n_pos
    pos = np.concatenate([np.cos(ang), -np.sin(ang)], axis=1) / math.sqrt(n_pos)
    c = np.arange(FOURIER_CH, dtype=np.int64)
    angc = 2.0 * np.pi * ((c[:, None] * c[None, :]) % FOURIER_CH) / FOURIER_CH
    ch = np.concatenate([np.cos(angc), np.sin(angc)], axis=1) / math.sqrt(FOURIER_CH)
    return jnp.asarray(pos, dtype=BF16), jnp.asarray(ch, dtype=BF16)


def _fourier_kernel(u_ref, ch_ref, pos_ref, o_ref, rhs_scr, *, n_pos):
    @pl.when(pl.program_id(1) == 0)
    def _():
        for g in range(FOURIER_GROUPS):
            a = _dot(u_ref[:, g * FOURIER_CH:(g + 1) * FOURIER_CH], ch_ref[...])
            rhs_scr[0:n_pos, g * FOURIER_CH:(g + 1) * FOURIER_CH] = a[:, :FOURIER_CH].astype(BF16)
            rhs_scr[n_pos:2 * n_pos, g * FOURIER_CH:(g + 1) * FOURIER_CH] = a[:, FOURIER_CH:].astype(BF16)

    o_ref[...] = _dot(pos_ref[...], rhs_scr[...]).astype(BF16)


def _fourier(u, n_pos, *, tr=1024):
    pos_t, ch_t = _dft_tables(n_pos)
    tr = min(tr, n_pos)
    n_j = n_pos // tr
    return pl.pallas_call(
        functools.partial(_fourier_kernel, n_pos=n_pos),
        out_shape=jax.ShapeDtypeStruct((BATCH * n_pos, FOURIER_WIDTH), BF16),
        grid=(BATCH, n_j),
        in_specs=[
            pl.BlockSpec((n_pos, FOURIER_WIDTH), lambda b, j: (b, 0)),
            pl.BlockSpec((FOURIER_CH, 2 * FOURIER_CH), lambda b, j: (0, 0)),
            pl.BlockSpec((tr, 2 * n_pos), lambda b, j: (j, 0)),
        ],
        out_specs=pl.BlockSpec((tr, FOURIER_WIDTH), lambda b, j: (b * n_j + j, 0)),
        scratch_shapes=[pltpu.VMEM((2 * n_pos, FOURIER_WIDTH), BF16)],
        compiler_params=_params("parallel", "arbitrary"),
        name="fourier_mix",
    )(u, ch_t, pos_t)


def _out_kernel(x_ref, mod_ref, om_ref, od_ref, of_ref, w_ref, o_ref):
    y = (_dot(om_ref[...], w_ref[0:MLA_WIDTH, :])
         + _dot(od_ref[...], w_ref[MLA_WIDTH:MLA_WIDTH + DIFF_WIDTH, :])
         + _dot(of_ref[...], w_ref[MLA_WIDTH + DIFF_WIDTH:, :]))
    o_ref[...] = x_ref[...] + mod_ref[2:3, :] * y


def _out_proj(x, mod, l, seg0, o_mla, o_diff, o_four, w_out, *, tr=512):
    n_rows = x.shape[0]
    row = lambda w: pl.BlockSpec((tr, w), lambda i: (i, 0))
    return pl.pallas_call(
        _out_kernel,
        out_shape=jax.ShapeDtypeStruct((n_rows, D_MODEL), F32),
        grid=(n_rows // tr,),
        in_specs=[
            row(D_MODEL), _mod_spec(l, seg0, tr, 1, 1),
            row(MLA_WIDTH), row(DIFF_WIDTH), row(FOURIER_WIDTH),
            _resident((None, D_MODEL, D_MODEL), lambda i: (l, 0, 0)),
        ],
        out_specs=row(D_MODEL),
        compiler_params=_params("parallel"),
        name="out_proj",
    )(x, mod, o_mla, o_diff, o_four, w_out)


def _rope_tables():
    rows = SEQ // GRID_W
    pos_r = jnp.repeat(jnp.arange(rows), GRID_W)
    pos_c = jnp.tile(jnp.arange(GRID_W), rows)
    d = MLA_ROPE // 2
    half = d // 2
    inv = ROPE_THETA ** (-2.0 * jnp.arange(half, dtype=F32) / d)

    def tabs(pos):
        ang = pos.astype(F32)[:, None] * inv[None, :]
        return jnp.cos(ang), jnp.sin(ang)

    cr, sr = tabs(pos_r)
    cc, sc = tabs(pos_c)
    cos64 = jnp.concatenate([cr, cr, cc, cc], axis=-1)
    sin64 = jnp.concatenate([-sr, sr, -sc, sc], axis=-1)
    return jnp.tile(cos64, (1, 2)), jnp.tile(sin64, (1, 2))


def kernel(x, c, ctx, c_ctx, ada_w, ada_b, norm_g, ffn_wg, ffn_wu, ffn_wd, w_in, mla_q_norm, mla_kv_norm,
           mla_w_uq, mla_w_ukv, diff_lambda, diff_subln, w_out, final_norm):
    assert DEPTH == 2
    s_in = jnp.concatenate([c_ctx[None, :], c, jnp.zeros((MOD_ROWS - 1 - BATCH, D_MODEL), F32)], axis=0)
    w_in_b = w_in.astype(BF16)
    w_in_p = (w_in_b[..., :IN_KROPE],
              jnp.pad(w_in_b[..., IN_KROPE:IN_REST], ((0, 0), (0, 0), (0, ROPE_PAD - MLA_ROPE))),
              w_in_b[..., IN_REST:])
    w_uq_p = jnp.pad(mla_w_uq.astype(BF16).reshape(DEPTH, MLA_Q_RANK, MLA_HEADS, MLA_NOPE + MLA_ROPE),
                     ((0, 0), (0, 0), (0, 0), (0, MLA_QK_PAD - MLA_NOPE - MLA_ROPE))
                     ).reshape(DEPTH, MLA_Q_RANK, MLA_HEADS * MLA_QK_PAD)
    w_ukv = mla_w_ukv.astype(BF16)
    w_o = w_out.astype(BF16)
    rope_tabs = _rope_tables()
    ffn_w32 = (ffn_wg, ffn_wu, ffn_wd)
    ffn_w = {(0, 0): tuple(w[0, 0].astype(BF16) for w in ffn_w32)}

    mod = _ada_table(s_in, ada_w, ada_b).reshape(DEPTH, MOD_ROWS, 3, 3, D_MODEL)

    zl = x.reshape(LAT_ROWS, D_MODEL)
    zc = ctx.reshape(CTX_ROWS, D_MODEL)
    for l in range(DEPTH):
        last = l == DEPTH - 1
        lam_init = 0.8 - 0.6 * math.exp(-0.3 * l)
        proj = functools.partial(_proj, mod=mod, l=l, g=norm_g[l, 1], w_in=w_in_p, kv_norm=mla_kv_norm,
                                 w_ukv=w_ukv, q_norm=mla_q_norm, w_uq=w_uq_p)
        diff_attn = functools.partial(_diff_attention, diff_lambda, diff_subln, l, lam_init)

        def ffn(z, k, seg0, **kw):
            wg, wu, wd = ffn_w[l, k]
            return _ffn(z, mod, l, k, seg0, norm_g[l, 2 * k], wg, wu, wd, **kw)

        zl = ffn(zl, 0, 1)
        zc = ffn(zc, 0, 0)
        kl, vl, dkl, dvl, ql, dql, ul = proj(zl, seg0=1, rope_tabs=rope_tabs)
        if last:
            kc, vc, dkc, dvc = proj(zc, seg0=0, kv_only=True)
            o_mla = _mla_attention(ql, kc, vc, kl, vl)
            o_diff = diff_attn(dql, dkc, dvc, dkl, dvl)
        else:
            kc, vc, dkc, dvc, qc, dqc, uc = proj(zc, seg0=0)
            o_mla, *w_a = _mla_attention(ql, kc, vc, kl, vl,
                                         cast=[(w, 0, 1) for w in ffn_w32] + [(w, 1, 0) for w in ffn_w32])
            o_diff, *w_b = diff_attn(dql, dkc, dvc, dkl, dvl, cast=[(w, 1, 1) for w in ffn_w32])
            ffn_w.update({(0, 1): tuple(w_a[:3]), (1, 0): tuple(w_a[3:]), (1, 1): tuple(w_b)})
        zl = _out_proj(zl, mod, l, 1, o_mla, o_diff, _fourier(ul, SEQ), w_o)
        if last:
            zl = ffn(zl, 1, 1, final_g=final_norm)
        else:
            zc = _out_proj(zc, mod, l, 0, _mla_attention(qc, kc, vc), diff_attn(dqc, dkc, dvc),
                           _fourier(uc, CTX_LEN), w_o)
            zl = ffn(zl, 1, 1)
            zc = ffn(zc, 1, 0)
    return zl.reshape(BATCH, SEQ, D_MODEL)
```

```python
import functools
import math

import numpy as np
import jax
import jax.numpy as jnp
from jax import lax
from jax.experimental import pallas as pl
from jax.experimental.pallas import tpu as pltpu

D_MODEL = 2048
BATCH = 8
SEQ = 2048
DEPTH = 2
GRID_W = 64
CTX_LEN = 256
EPS = 1e-6
ROPE_THETA = 10000.0
D_FF = 5632
N_MOD = 9

MLA_HEADS = 8
MLA_Q_RANK = 512
MLA_KV_RANK = 256
MLA_NOPE = 128
MLA_ROPE = 64
MLA_V = 128
MLA_SCALE = (MLA_NOPE + MLA_ROPE) ** -0.5
MLA_QK_PAD = 256
MLA_V_PAD = 256

DIFF_HEADS = 4
DIFF_QK = 64
DIFF_V = 2 * DIFF_QK
DIFF_SCALE = DIFF_QK ** -0.5

FOURIER_GROUPS = 4
FOURIER_CH = 128
MLA_WIDTH = MLA_HEADS * MLA_V
DIFF_WIDTH = DIFF_HEADS * DIFF_V
FOURIER_WIDTH = FOURIER_GROUPS * FOURIER_CH

SEG = SEQ
LAT_ROWS = BATCH * SEQ
CTX_ROWS = BATCH * CTX_LEN
MOD_ROWS = 16

IN_KROPE = MLA_KV_RANK
IN_REST = MLA_KV_RANK + MLA_ROPE
ROPE_PAD = 128
R_DK = 0
R_DV = 512
R_CQ = 1024
R_DQ = 1536
R_U = 2048
R_COLS = 2560

VMEM_LIMIT = 56 * 1024 * 1024
VMEM_LIMIT_FFN = 63 * 1024 * 1024 + 512 * 1024
LOG2_E = math.log2(math.e)

F32 = jnp.float32
BF16 = jnp.bfloat16


def _silu(v):
    return v / (1.0 + jnp.exp(-v))


def _dot(a, b):
    return jnp.dot(a, b, preferred_element_type=F32)


def _dot_nt(a, b):
    return lax.dot_general(a, b, (((1,), (1,)), ((), ())), preferred_element_type=F32)


def _rms(v):
    return v * lax.rsqrt(jnp.mean(v * v, axis=-1, keepdims=True) + EPS)


def _row_chunks(n_rows, chunk):
    return [slice(r, r + chunk) for r in range(0, n_rows, chunk)]


def _modulate_rows(z_ref, g_ref, mod_ref, h_ref):
    gain = g_ref[...] * (1.0 + mod_ref[1:2, :])
    shift = mod_ref[0:1, :]
    for rows in _row_chunks(z_ref.shape[0], 16):
        h_ref[rows, :] = (_rms(z_ref[rows, :]) * gain + shift).astype(BF16)


def _params(*sem, vmem=VMEM_LIMIT):
    return pltpu.CompilerParams(dimension_semantics=sem, vmem_limit_bytes=vmem)


def _resident(shape, index_map):
    return pl.BlockSpec(shape, index_map, pipeline_mode=pl.Buffered(1))


def _mod_spec(l, seg0, tile, grp, n_grid):
    if n_grid == 1:
        return pl.BlockSpec((None, None, None, 3, D_MODEL), lambda i: (l, seg0 + i * tile // SEG, grp, 0, 0))
    return pl.BlockSpec((None, None, None, 3, D_MODEL), lambda i, f: (l, seg0 + i * tile // SEG, grp, 0, 0))


def _ada_kernel(s_ref, w_ref, b_ref, o_ref):
    s = _silu(s_ref[...])
    o_ref[...] = _dot(s.astype(BF16), w_ref[...].astype(BF16)) + b_ref[...]


def _ada_table(s_in, ada_w, ada_b):
    tn = 1024
    n_cols = N_MOD * D_MODEL
    return pl.pallas_call(
        _ada_kernel,
        out_shape=jax.ShapeDtypeStruct((DEPTH, MOD_ROWS, n_cols), F32),
        grid=(DEPTH, n_cols // tn),
        in_specs=[
            pl.BlockSpec((MOD_ROWS, D_MODEL), lambda l, n: (0, 0)),
            pl.BlockSpec((None, D_MODEL, tn), lambda l, n: (l, 0, n)),
            pl.BlockSpec((None, 1, tn), lambda l, n: (l, 0, n)),
        ],
        out_specs=pl.BlockSpec((None, MOD_ROWS, tn), lambda l, n: (l, 0, n)),
        compiler_params=_params("parallel", "parallel"),
        name="ada_table",
    )(s_in, ada_w, ada_b.reshape(DEPTH, 1, n_cols))


DOWN_COLS = 512


def _ffn_kernel(z_ref, mod_ref, g_ref, wg_ref, wu_ref, wd_ref, *rest, n_f, final):
    if final:
        fn_ref, o_ref, h_scr = rest
    else:
        o_ref, h_scr = rest
    f = pl.program_id(1)
    tm = o_ref.shape[0]

    def gate_up(rows):
        h = h_scr[rows, :]
        return _dot(h, wg_ref[...]), _dot(h, wu_ref[...])

    def sweep_step(first, last):
        if first:
            parts = []
            for rows in _row_chunks(tm, tm // 2):
                _modulate_rows(z_ref.at[rows], g_ref, mod_ref, h_scr.at[rows])
                parts.append(gate_up(rows))
            g = jnp.concatenate([p[0] for p in parts], axis=0)
            u = jnp.concatenate([p[1] for p in parts], axis=0)
        else:
            g, u = gate_up(slice(None))
        half = g.shape[1] // 2
        a0 = (_silu(g[:, :half]) * u[:, :half]).astype(BF16)
        a1 = (_silu(g[:, half:]) * u[:, half:]).astype(BF16)
        for c in range(0, D_MODEL, DOWN_COLS):
            cols = slice(c, c + DOWN_COLS)
            d = _dot(a0, wd_ref[:half, cols]) + _dot(a1, wd_ref[half:, cols])
            if first:
                o_ref[:, cols] = d
            elif not last:
                o_ref[:, cols] += d
            else:
                half_gate = 0.5 * mod_ref[2:3, cols]
                for rows in _row_chunks(tm, 32):
                    o_ref[rows, cols] = z_ref[rows, cols] + half_gate * (o_ref[rows, cols] + d[rows, :])
        if last and final:
            for rows in _row_chunks(tm, 8):
                o_ref[rows, :] = _rms(o_ref[rows, :]) * fn_ref[...]

    pl.when(f == 0)(functools.partial(sweep_step, True, False))
    pl.when(jnp.logical_and(f > 0, f < n_f - 1))(functools.partial(sweep_step, False, False))
    pl.when(f == n_f - 1)(functools.partial(sweep_step, False, True))


def _ffn(z, mod, l, k, seg0, g, wg, wu, wd, *, final_g=None, tm=1024, tf=512):
    n_rows = z.shape[0]
    n_f = D_FF // tf
    final = final_g is not None
    in_specs = [
        pl.BlockSpec((tm, D_MODEL), lambda i, f: (i, 0)),
        _mod_spec(l, seg0, tm, 2 * k, 2),
        pl.BlockSpec((1, D_MODEL), lambda i, f: (0, 0)),
        pl.BlockSpec((D_MODEL, tf), lambda i, f: (0, f)),
        pl.BlockSpec((D_MODEL, tf), lambda i, f: (0, f)),
        pl.BlockSpec((tf, D_MODEL), lambda i, f: (f, 0)),
    ]
    args = [z, mod, g.reshape(1, D_MODEL), wg, wu, wd]
    if final:
        in_specs.append(pl.BlockSpec((1, D_MODEL), lambda i, f: (0, 0)))
        args.append(final_g.reshape(1, D_MODEL))
    return pl.pallas_call(
        functools.partial(_ffn_kernel, n_f=n_f, final=final),
        out_shape=jax.ShapeDtypeStruct((n_rows, D_MODEL), F32),
        grid=(n_rows // tm, n_f),
        in_specs=in_specs,
        out_specs=pl.BlockSpec((tm, D_MODEL), lambda i, f: (i, 0)),
        scratch_shapes=[pltpu.VMEM((tm, D_MODEL), BF16)],
        compiler_params=_params("parallel", "arbitrary", vmem=VMEM_LIMIT_FFN),
        name="ffn_final" if final else "ffn",
    )(*args)


def _rope(v, cos, sin, lo_half):
    swapped = jnp.where(lo_half, pltpu.roll(v, 112, 1), pltpu.roll(v, 16, 1))
    return v * cos + swapped * sin


def _proj_kernel(*refs, rope, kv_only):
    refs = list(refs)
    x_ref, mod_ref, g_ref, wckv_ref, wkr_ref, wrest_ref, kvn_ref, wukv_ref = refs[:8]
    del refs[:8]
    if not kv_only:
        qn_ref, wuq_ref = refs[:2]
        del refs[:2]
    if rope:
        cos_ref, sin_ref = refs[:2]
        del refs[:2]
    h_scr = refs.pop()
    if kv_only:
        kmla_ref, vmla_ref, dk_ref, dv_ref = refs
    else:
        kmla_ref, vmla_ref, dk_ref, dv_ref, qmla_ref, dq_ref, u_ref = refs

    _modulate_rows(x_ref, g_ref, mod_ref, h_scr)
    hb = h_scr[...]
    if rope:
        lane = lax.broadcasted_iota(jnp.int32, (1, 128), 1)
        rotate = functools.partial(_rope, cos=cos_ref[...], sin=sin_ref[...], lo_half=(lane % 32) < 16)
    else:
        rotate = lambda v: v
    q_scale = MLA_SCALE * LOG2_E
    dq_scale = DIFF_SCALE * LOG2_E

    ckv = _dot(hb, wckv_ref[...])
    cq = None if kv_only else _dot(hb, wrest_ref[:, R_CQ:R_DQ])
    k_rope = _dot(hb, wkr_ref[...])
    dk = _dot(hb, wrest_ref[:, R_DK:R_DV])
    dv = _dot(hb, wrest_ref[:, R_DV:R_CQ])
    if not kv_only:
        dq = _dot(hb, wrest_ref[:, R_DQ:R_U])
        u = _dot(hb, wrest_ref[:, R_U:R_COLS])
    kv = _dot((_rms(ckv) * kvn_ref[...]).astype(BF16), wukv_ref[...])
    if not kv_only:
        q = _dot((_rms(cq) * qn_ref[...]).astype(BF16), wuq_ref[...])

    k_rope = rotate(k_rope).astype(BF16)
    for j in range(DIFF_HEADS):
        dk_ref[:, j * 128:(j + 1) * 128] = rotate(dk[:, j * 128:(j + 1) * 128]).astype(BF16)
    dv_ref[...] = dv.astype(BF16)
    for h in range(MLA_HEADS):
        c0 = h * (MLA_NOPE + MLA_V)
        kmla_ref[:, h * MLA_QK_PAD:h * MLA_QK_PAD + MLA_NOPE] = kv[:, c0:c0 + MLA_NOPE].astype(BF16)
        kmla_ref[:, h * MLA_QK_PAD + MLA_NOPE:(h + 1) * MLA_QK_PAD] = k_rope
        vmla_ref[:, h * MLA_V_PAD:h * MLA_V_PAD + MLA_V] = kv[:, c0 + MLA_NOPE:c0 + MLA_NOPE + MLA_V].astype(BF16)
        vmla_ref[:, h * MLA_V_PAD + MLA_V:(h + 1) * MLA_V_PAD] = jnp.ones((kv.shape[0], MLA_V_PAD - MLA_V), BF16)
    if kv_only:
        return

    for j in range(DIFF_HEADS):
        dq_ref[:, j * 128:(j + 1) * 128] = (rotate(dq[:, j * 128:(j + 1) * 128]) * dq_scale).astype(BF16)
    u_ref[...] = u.astype(BF16)
    for h in range(MLA_HEADS):
        c0 = h * MLA_QK_PAD
        qmla_ref[:, c0:c0 + 128] = (q[:, c0:c0 + 128] * q_scale).astype(BF16)
        qmla_ref[:, c0 + 128:c0 + 256] = (rotate(q[:, c0 + 128:c0 + 256]) * q_scale).astype(BF16)


def _proj(x, mod, l, seg0, g, w_in, kv_norm, w_ukv, q_norm, w_uq, rope_tabs=None, *, kv_only=False, tr=512):
    n_rows = x.shape[0]
    row = lambda w: pl.BlockSpec((tr, w), lambda i: (i, 0))
    vec = lambda w: _resident((None, 1, w), lambda i: (l, 0, 0))
    mat = lambda r, c: _resident((None, r, c), lambda i: (l, 0, 0))
    in_specs = [row(D_MODEL), _mod_spec(l, seg0, tr, 1, 1), _resident((1, D_MODEL), lambda i: (0, 0)),
                mat(D_MODEL, MLA_KV_RANK), mat(D_MODEL, ROPE_PAD), mat(D_MODEL, R_COLS),
                vec(MLA_KV_RANK), mat(MLA_KV_RANK, MLA_HEADS * (MLA_NOPE + MLA_V))]
    args = [x, mod, g.reshape(1, D_MODEL), *w_in, kv_norm.reshape(DEPTH, 1, -1), w_ukv]
    widths = [MLA_HEADS * MLA_QK_PAD, MLA_HEADS * MLA_V_PAD, DIFF_WIDTH, DIFF_WIDTH]
    if not kv_only:
        in_specs += [vec(MLA_Q_RANK), mat(MLA_Q_RANK, MLA_HEADS * MLA_QK_PAD)]
        args += [q_norm.reshape(DEPTH, 1, -1), w_uq]
        widths += [MLA_HEADS * MLA_QK_PAD, DIFF_WIDTH, FOURIER_WIDTH]
    if rope_tabs is not None:
        per_seg = SEG // tr
        in_specs += [pl.BlockSpec((tr, 128), lambda i: (i % per_seg, 0))] * 2
        args += list(rope_tabs)
    return pl.pallas_call(
        functools.partial(_proj_kernel, rope=rope_tabs is not None, kv_only=kv_only),
        out_shape=[jax.ShapeDtypeStruct((n_rows, w), BF16) for w in widths],
        grid=(n_rows // tr,),
        in_specs=in_specs,
        out_specs=[row(w) for w in widths],
        scratch_shapes=[pltpu.VMEM((tr, D_MODEL), BF16)],
        compiler_params=_params("parallel"),
        name="in_proj",
    )(*args)


SUB = 256
CHUNK_VREGS = 36


def _half_rows(n_keys):
    half = 8
    while 2 * half * n_keys <= CHUNK_VREGS * 1024 and 4 * half <= SUB:
        half *= 2
    return half


def _attn_refs(refs, lat):
    if not lat:
        q_ref, kc_ref, vc_ref, o_ref = refs
        return q_ref, kc_ref, vc_ref, None, None, o_ref
    n_cast = (len(refs) - 6) // 2
    for src, dst in zip(refs[5:5 + n_cast], refs[6 + n_cast:]):
        dst[...] = src[...].astype(BF16)
    return (*refs[:5], refs[5 + n_cast])


def _cast_rows(n_rows, n_steps):
    return min(r for r in range(16, n_rows + 1, 16) if n_rows % r == 0 and n_rows // r <= n_steps)


MLA_AHEAD = 1
DIFF_AHEAD = 2


def _pipelined(items, scores, softmax, values, n_slots):
    ahead = n_slots - 1
    for k in range(min(ahead, len(items))):
        scores(k % n_slots, *items[k])
    for i, item in enumerate(items):
        if i + ahead < len(items):
            scores((i + ahead) % n_slots, *items[i + ahead])
        softmax(i % n_slots, i % 2, *item)
        if i > 0:
            values((i - 1) % 2, *items[i - 1])
    values((len(items) - 1) % 2, *items[-1])


def _store_scores(s_scr, slot, q, kc, kl):
    s_scr[slot, :, :CTX_LEN] = _dot_nt(q, kc)
    if kl is not None:
        s_scr[slot, :, CTX_LEN:] = _dot_nt(q, kl)


def _weighted_values(p_scr, slot, vc, vl):
    o = _dot(p_scr[slot, :, :CTX_LEN], vc)
    if vl is not None:
        o = o + _dot(p_scr[slot, :, CTX_LEN:], vl)
    return o


def _chunk_numerators(s):
    return jnp.exp2(s - jnp.max(s, axis=-1, keepdims=True))


def _mla_kernel(*refs, heads, n_sub, lat):
    *io, s_scr, p_scr = refs
    q_ref, kc_ref, vc_ref, kl_ref, vl_ref, o_ref = _attn_refs(io, lat)

    def scores(slot, hh, j):
        qk = slice(hh * MLA_QK_PAD, (hh + 1) * MLA_QK_PAD)
        _store_scores(s_scr, slot, q_ref[j * SUB:(j + 1) * SUB, qk], kc_ref[:, qk], kl_ref[:, qk] if lat else None)

    def softmax(slot, pslot, hh, j):
        half = _half_rows(s_scr.shape[-1])
        for r in range(0, SUB, 2 * half):
            e = [_chunk_numerators(s_scr[slot, r0:r0 + half, :]) for r0 in (r, r + half)]
            p_scr[pslot, r:r + 2 * half, :] = jnp.concatenate(e, axis=0).astype(BF16)

    def values(pslot, hh, j):
        vv = slice(hh * MLA_V_PAD, (hh + 1) * MLA_V_PAD)
        o = _weighted_values(p_scr, pslot, vc_ref[:, vv], vl_ref[:, vv] if lat else None)
        o_ref[j * SUB:(j + 1) * SUB, hh * MLA_V:(hh + 1) * MLA_V] = (o[:, :MLA_V] / o[:, MLA_V:]).astype(BF16)

    _pipelined([(hh, j) for hh in range(heads) for j in range(n_sub)], scores, softmax, values, s_scr.shape[0])


def _mla_scratch(n_keys):
    return [pltpu.VMEM((MLA_AHEAD + 1, SUB, n_keys), F32), pltpu.VMEM((2, SUB, n_keys), BF16)]


def _diff_kernel(lam_ref, sub_ref, *refs, heads, n_sub, lat, lam_init):
    *io, s1_scr, s2_scr, p_scr, d_scr = refs
    q_ref, kc_ref, vc_ref, kl_ref, vl_ref, o_ref = _attn_refs(io, lat)
    lf = lam_ref[...]
    lam = (jnp.exp(jnp.sum(lf[0:1] * lf[1:2], axis=-1, keepdims=True))
           - jnp.exp(jnp.sum(lf[2:3] * lf[3:4], axis=-1, keepdims=True)) + lam_init)
    first = lax.broadcasted_iota(jnp.int32, (1, 2 * DIFF_QK), 1) < DIFF_QK

    def scores(slot, hh, j):
        cols = slice(hh * DIFF_V, (hh + 1) * DIFF_V)
        q = q_ref[j * SUB:(j + 1) * SUB, cols]
        kc = kc_ref[:, cols]
        kl = kl_ref[:, cols] if lat else None
        _store_scores(s1_scr, slot, jnp.where(first, q, jnp.zeros_like(q)), kc, kl)
        _store_scores(s2_scr, slot, jnp.where(first, jnp.zeros_like(q), q), kc, kl)

    def softmax(slot, pslot, hh, j):
        half = _half_rows(s1_scr.shape[-1])
        for r in range(0, SUB, 2 * half):
            p = []
            for r0 in (r, r + half):
                e1 = _chunk_numerators(s1_scr[slot, r0:r0 + half, :])
                e2 = _chunk_numerators(s2_scr[slot, r0:r0 + half, :])
                d1 = jnp.sum(e1, axis=-1, keepdims=True)
                d2 = jnp.sum(e2, axis=-1, keepdims=True)
                p.append(e1 - e2 * (lam * d1 / d2))
                d_scr[pslot, r0:r0 + half, :] = jnp.broadcast_to(d1, (half, DIFF_V))
            p_scr[pslot, r:r + 2 * half, :] = jnp.concatenate(p, axis=0).astype(BF16)

    def values(pslot, hh, j):
        cols = slice(hh * DIFF_V, (hh + 1) * DIFF_V)
        o = _weighted_values(p_scr, pslot, vc_ref[:, cols], vl_ref[:, cols] if lat else None) / d_scr[pslot]
        o_ref[j * SUB:(j + 1) * SUB, cols] = (_rms(o) * sub_ref[...] * (1.0 - lam_init)).astype(BF16)

    _pipelined([(hh, j) for hh in range(heads) for j in range(n_sub)], scores, softmax, values, s1_scr.shape[0])


def _diff_scratch(n_keys):
    return [pltpu.VMEM((DIFF_AHEAD + 1, SUB, n_keys), F32), pltpu.VMEM((DIFF_AHEAD + 1, SUB, n_keys), F32),
            pltpu.VMEM((2, SUB, n_keys), BF16), pltpu.VMEM((2, SUB, DIFF_V), F32)]


def _attention(kernel_fn, scratch_fn, name, q, k_ctx, v_ctx, k_lat=None, v_lat=None, *, n_heads, w_qk, w_v, w_o,
               tq, extra_args=(), extra_specs=(), cast=()):
    if k_lat is not None:
        per_seg = SEQ // tq
        n_steps = BATCH * n_heads * per_seg
        in_specs = list(extra_specs) + [
            pl.BlockSpec((tq, w_qk), lambda b, h, t: (b * per_seg + t, h)),
            pl.BlockSpec((CTX_LEN, w_qk), lambda b, h, t: (b, h)),
            pl.BlockSpec((CTX_LEN, w_v), lambda b, h, t: (b, h)),
            pl.BlockSpec((SEQ, w_qk), lambda b, h, t: (b, h)),
            pl.BlockSpec((SEQ, w_v), lambda b, h, t: (b, h)),
        ]
        out_shape = [jax.ShapeDtypeStruct((LAT_ROWS, n_heads * w_o), BF16)]
        out_specs = [pl.BlockSpec((tq, w_o), lambda b, h, t: (b * per_seg + t, h))]
        for w, l, k in cast:
            n_r, n_c = w.shape[2:]
            rows = _cast_rows(n_r, n_steps)
            blk = lambda b, h, t, last=n_r // rows - 1: jnp.minimum((b * n_heads + h) * per_seg + t, last)
            in_specs.append(pl.BlockSpec((None, None, rows, n_c), lambda b, h, t, l=l, k=k, blk=blk: (l, k, blk(b, h, t), 0)))
            out_shape.append(jax.ShapeDtypeStruct((n_r, n_c), BF16))
            out_specs.append(pl.BlockSpec((rows, n_c), lambda b, h, t, blk=blk: (blk(b, h, t), 0)))
        out = pl.pallas_call(
            functools.partial(kernel_fn, heads=1, n_sub=tq // SUB, lat=True),
            out_shape=out_shape,
            grid=(BATCH, n_heads, per_seg),
            in_specs=in_specs,
            out_specs=out_specs,
            scratch_shapes=scratch_fn(CTX_LEN + SEQ),
            compiler_params=_params(*(("arbitrary",) * 3 if cast else ("parallel", "parallel", "arbitrary"))),
            name=name,
        )(*extra_args, q, k_ctx, v_ctx, k_lat, v_lat, *[w for w, _, _ in cast])
        return out if cast else out[0]
    in_specs = list(extra_specs) + [
        pl.BlockSpec((CTX_LEN, n_heads * w_qk), lambda b: (b, 0)),
        pl.BlockSpec((CTX_LEN, n_heads * w_qk), lambda b: (b, 0)),
        pl.BlockSpec((CTX_LEN, n_heads * w_v), lambda b: (b, 0)),
    ]
    return pl.pallas_call(
        functools.partial(kernel_fn, heads=n_heads, n_sub=CTX_LEN // SUB, lat=False),
        out_shape=jax.ShapeDtypeStruct((CTX_ROWS, n_heads * w_o), BF16),
        grid=(BATCH,),
        in_specs=in_specs,
        out_specs=pl.BlockSpec((CTX_LEN, n_heads * w_o), lambda b: (b, 0)),
        scratch_shapes=scratch_fn(CTX_LEN),
        compiler_params=_params("parallel"),
        name=name + "_ctx",
    )(*extra_args, q, k_ctx, v_ctx)


def _mla_attention(q, k_ctx, v_ctx, k_lat=None, v_lat=None, *, tq=2048, cast=()):
    return _attention(_mla_kernel, _mla_scratch, "mla_attention", q, k_ctx, v_ctx, k_lat, v_lat,
                      n_heads=MLA_HEADS, w_qk=MLA_QK_PAD, w_v=MLA_V_PAD, w_o=MLA_V, tq=tq, cast=cast)


def _diff_attention(diff_lambda, subln, l, lam_init, q, k_ctx, v_ctx, k_lat=None, v_lat=None, *, tq=2048, cast=()):
    layer = lambda *_: (l, 0, 0)
    extra_specs = (pl.BlockSpec((None, 4, DIFF_QK), layer), pl.BlockSpec((None, 1, DIFF_V), layer))
    return _attention(functools.partial(_diff_kernel, lam_init=lam_init), _diff_scratch, "diff_attention",
                      q, k_ctx, v_ctx, k_lat, v_lat,
                      n_heads=DIFF_HEADS, w_qk=2 * DIFF_QK, w_v=DIFF_V, w_o=DIFF_V, tq=tq,
                      extra_args=(diff_lambda, subln.reshape(DEPTH, 1, DIFF_V)), extra_specs=extra_specs, cast=cast)


def _dft_tables(n_pos):
    k = np.arange(n_pos, dtype=np.int64)
    ang = 2.0 * np.pi * ((k[:, None] * k[None, :]) % n_pos) / n_pos
    pos = np.concatenate([np.cos(ang), -np.sin(ang)], axis=1) / math.sqrt(n_pos)
    c = np.arange(FOURIER_CH, dtype=np.int64)
    angc = 2.0 * np.pi * ((c[:, None] * c[None, :]) % FOURIER_CH) / FOURIER_CH
    ch = np.concatenate([np.cos(angc), np.sin(angc)], axis=1) / math.sqrt(FOURIER_CH)
    return jnp.asarray(pos, dtype=BF16), jnp.asarray(ch, dtype=BF16)


def _fourier_kernel(u_ref, ch_ref, pos_ref, o_ref, rhs_scr, *, n_pos):
    @pl.when(pl.program_id(1) == 0)
    def _():
        for g in range(FOURIER_GROUPS):
            a = _dot(u_ref[:, g * FOURIER_CH:(g + 1) * FOURIER_CH], ch_ref[...])
            rhs_scr[0:n_pos, g * FOURIER_CH:(g + 1) * FOURIER_CH] = a[:, :FOURIER_CH].astype(BF16)
            rhs_scr[n_pos:2 * n_pos, g * FOURIER_CH:(g + 1) * FOURIER_CH] = a[:, FOURIER_CH:].astype(BF16)

    o_ref[...] = _dot(pos_ref[...], rhs_scr[...]).astype(BF16)


def _fourier(u, n_pos, *, tr=1024):
    pos_t, ch_t = _dft_tables(n_pos)
    tr = min(tr, n_pos)
    n_j = n_pos // tr
    return pl.pallas_call(
        functools.partial(_fourier_kernel, n_pos=n_pos),
        out_shape=jax.ShapeDtypeStruct((BATCH * n_pos, FOURIER_WIDTH), BF16),
        grid=(BATCH, n_j),
        in_specs=[
            pl.BlockSpec((n_pos, FOURIER_WIDTH), lambda b, j: (b, 0)),
            pl.BlockSpec((FOURIER_CH, 2 * FOURIER_CH), lambda b, j: (0, 0)),
            pl.BlockSpec((tr, 2 * n_pos), lambda b, j: (j, 0)),
        ],
        out_specs=pl.BlockSpec((tr, FOURIER_WIDTH), lambda b, j: (b * n_j + j, 0)),
        scratch_shapes=[pltpu.VMEM((2 * n_pos, FOURIER_WIDTH), BF16)],
        compiler_params=_params("parallel", "arbitrary"),
        name="fourier_mix",
    )(u, ch_t, pos_t)


def _out_kernel(x_ref, mod_ref, om_ref, od_ref, of_ref, w_ref, o_ref):
    y = (_dot(om_ref[...], w_ref[0:MLA_WIDTH, :])
         + _dot(od_ref[...], w_ref[MLA_WIDTH:MLA_WIDTH + DIFF_WIDTH, :])
         + _dot(of_ref[...], w_ref[MLA_WIDTH + DIFF_WIDTH:, :]))
    o_ref[...] = x_ref[...] + mod_ref[2:3, :] * y


def _out_proj(x, mod, l, seg0, o_mla, o_diff, o_four, w_out, *, tr=512):
    n_rows = x.shape[0]
    row = lambda w: pl.BlockSpec((tr, w), lambda i: (i, 0))
    return pl.pallas_call(
        _out_kernel,
        out_shape=jax.ShapeDtypeStruct((n_rows, D_MODEL), F32),
        grid=(n_rows // tr,),
        in_specs=[
            row(D_MODEL), _mod_spec(l, seg0, tr, 1, 1),
            row(MLA_WIDTH), row(DIFF_WIDTH), row(FOURIER_WIDTH),
            _resident((None, D_MODEL, D_MODEL), lambda i: (l, 0, 0)),
        ],
        out_specs=row(D_MODEL),
        compiler_params=_params("parallel"),
        name="out_proj",
    )(x, mod, o_mla, o_diff, o_four, w_out)


def _rope_tables():
    rows = SEQ // GRID_W
    pos_r = jnp.repeat(jnp.arange(rows), GRID_W)
    pos_c = jnp.tile(jnp.arange(GRID_W), rows)
    d = MLA_ROPE // 2
    half = d // 2
    inv = ROPE_THETA ** (-2.0 * jnp.arange(half, dtype=F32) / d)

    def tabs(pos):
        ang = pos.astype(F32)[:, None] * inv[None, :]
        return jnp.cos(ang), jnp.sin(ang)

    cr, sr = tabs(pos_r)
    cc, sc = tabs(pos_c)
    cos64 = jnp.concatenate([cr, cr, cc, cc], axis=-1)
    sin64 = jnp.concatenate([-sr, sr, -sc, sc], axis=-1)
    return jnp.tile(cos64, (1, 2)), jnp.tile(sin64, (1, 2))


def kernel(x, c, ctx, c_ctx, ada_w, ada_b, norm_g, ffn_wg, ffn_wu, ffn_wd, w_in, mla_q_norm, mla_kv_norm,
           mla_w_uq, mla_w_ukv, diff_lambda, diff_subln, w_out, final_norm):
    assert DEPTH == 2
    s_in = jnp.concatenate([c_ctx[None, :], c, jnp.zeros((MOD_ROWS - 1 - BATCH, D_MODEL), F32)], axis=0)
    w_in_b = w_in.astype(BF16)
    w_in_p = (w_in_b[..., :IN_KROPE],
              jnp.pad(w_in_b[..., IN_KROPE:IN_REST], ((0, 0), (0, 0), (0, ROPE_PAD - MLA_ROPE))),
              w_in_b[..., IN_REST:])
    w_uq_p = jnp.pad(mla_w_uq.astype(BF16).reshape(DEPTH, MLA_Q_RANK, MLA_HEADS, MLA_NOPE + MLA_ROPE),
                     ((0, 0), (0, 0), (0, 0), (0, MLA_QK_PAD - MLA_NOPE - MLA_ROPE))
                     ).reshape(DEPTH, MLA_Q_RANK, MLA_HEADS * MLA_QK_PAD)
    w_ukv = mla_w_ukv.astype(BF16)
    w_o = w_out.astype(BF16)
    rope_tabs = _rope_tables()
    ffn_w32 = (ffn_wg, ffn_wu, ffn_wd)
    ffn_w = {(0, 0): tuple(w[0, 0].astype(BF16) for w in ffn_w32)}

    mod = _ada_table(s_in, ada_w, ada_b).reshape(DEPTH, MOD_ROWS, 3, 3, D_MODEL)

    zl = x.reshape(LAT_ROWS, D_MODEL)
    zc = ctx.reshape(CTX_ROWS, D_MODEL)
    for l in range(DEPTH):
        last = l == DEPTH - 1
        lam_init = 0.8 - 0.6 * math.exp(-0.3 * l)
        proj = functools.partial(_proj, mod=mod, l=l, g=norm_g[l, 1], w_in=w_in_p, kv_norm=mla_kv_norm,
                                 w_ukv=w_ukv, q_norm=mla_q_norm, w_uq=w_uq_p)
        diff_attn = functools.partial(_diff_attention, diff_lambda, diff_subln, l, lam_init)

        def ffn(z, k, seg0, **kw):
            wg, wu, wd = ffn_w[l, k]
            return _ffn(z, mod, l, k, seg0, norm_g[l, 2 * k], wg, wu, wd, **kw)

        zl = ffn(zl, 0, 1)
        zc = ffn(zc, 0, 0)
        kl, vl, dkl, dvl, ql, dql, ul = proj(zl, seg0=1, rope_tabs=rope_tabs)
        if last:
            kc, vc, dkc, dvc = proj(zc, seg0=0, kv_only=True)
            o_mla = _mla_attention(ql, kc, vc, kl, vl)
            o_diff = diff_attn(dql, dkc, dvc, dkl, dvl)
        else:
            kc, vc, dkc, dvc, qc, dqc, uc = proj(zc, seg0=0)
            o_mla, *w_a = _mla_attention(ql, kc, vc, kl, vl,
                                         cast=[(w, 0, 1) for w in ffn_w32] + [(w, 1, 0) for w in ffn_w32])
            o_diff, *w_b = diff_attn(dql, dkc, dvc, dkl, dvl, cast=[(w, 1, 1) for w in ffn_w32])
            ffn_w.update({(0, 1): tuple(w_a[:3]), (1, 0): tuple(w_a[3:]), (1, 1): tuple(w_b)})
        zl = _out_proj(zl, mod, l, 1, o_mla, o_diff, _fourier(ul, SEQ), w_o)
        if last:
            zl = ffn(zl, 1, 1, final_g=final_norm)
        else:
            zc = _out_proj(zc, mod, l, 0, _mla_attention(qc, kc, vc), diff_attn(dqc, dkc, dvc),
                           _fourier(uc, CTX_LEN), w_o)
            zl = ffn(zl, 1, 1)
            zc = ffn(zc, 1, 0)
    return zl.reshape(BATCH, SEQ, D_MODEL)
```

```python
import functools
import math

import numpy as np
import jax
import jax.numpy as jnp
from jax import lax
from jax.experimental import pallas as pl
from jax.experimental.pallas import tpu as pltpu

D_MODEL = 2048
BATCH = 8
SEQ = 2048
DEPTH = 2
GRID_W = 64
CTX_LEN = 256
EPS = 1e-6
ROPE_THETA = 10000.0
D_FF = 5632
N_MOD = 9

MLA_HEADS = 8
MLA_Q_RANK = 512
MLA_KV_RANK = 256
MLA_NOPE = 128
MLA_ROPE = 64
MLA_V = 128
MLA_SCALE = (MLA_NOPE + MLA_ROPE) ** -0.5
MLA_QK_PAD = 256
MLA_V_PAD = 256

DIFF_HEADS = 4
DIFF_QK = 64
DIFF_V = 2 * DIFF_QK
DIFF_SCALE = DIFF_QK ** -0.5

FOURIER_GROUPS = 4
FOURIER_CH = 128
MLA_WIDTH = MLA_HEADS * MLA_V
DIFF_WIDTH = DIFF_HEADS * DIFF_V
FOURIER_WIDTH = FOURIER_GROUPS * FOURIER_CH

SEG = SEQ
LAT_ROWS = BATCH * SEQ
CTX_ROWS = BATCH * CTX_LEN
MOD_ROWS = 16

IN_KROPE = MLA_KV_RANK
IN_REST = MLA_KV_RANK + MLA_ROPE
ROPE_PAD = 128
R_DK = 0
R_DV = 512
R_CQ = 1024
R_DQ = 1536
R_U = 2048
R_COLS = 2560

VMEM_LIMIT = 56 * 1024 * 1024
VMEM_LIMIT_FFN = 63 * 1024 * 1024 + 512 * 1024
LOG2_E = math.log2(math.e)

F32 = jnp.float32
BF16 = jnp.bfloat16


def _silu(v):
    return v / (1.0 + jnp.exp(-v))


def _dot(a, b):
    return jnp.dot(a, b, preferred_element_type=F32)


def _dot_nt(a, b):
    return lax.dot_general(a, b, (((1,), (1,)), ((), ())), preferred_element_type=F32)


def _rms(v):
    return v * lax.rsqrt(jnp.mean(v * v, axis=-1, keepdims=True) + EPS)


def _row_chunks(n_rows, chunk):
    return [slice(r, r + chunk) for r in range(0, n_rows, chunk)]


def _modulate_rows(z_ref, g_ref, mod_ref, h_ref):
    gain = g_ref[...] * (1.0 + mod_ref[1:2, :])
    shift = mod_ref[0:1, :]
    for rows in _row_chunks(z_ref.shape[0], 16):
        h_ref[rows, :] = (_rms(z_ref[rows, :]) * gain + shift).astype(BF16)


def _params(*sem, vmem=VMEM_LIMIT):
    return pltpu.CompilerParams(dimension_semantics=sem, vmem_limit_bytes=vmem)


def _resident(shape, index_map):
    return pl.BlockSpec(shape, index_map, pipeline_mode=pl.Buffered(1))


def _mod_spec(l, seg0, tile, grp, n_grid):
    if n_grid == 1:
        return pl.BlockSpec((None, None, None, 3, D_MODEL), lambda i: (l, seg0 + i * tile // SEG, grp, 0, 0))
    return pl.BlockSpec((None, None, None, 3, D_MODEL), lambda i, f: (l, seg0 + i * tile // SEG, grp, 0, 0))


def _ada_kernel(s_ref, w_ref, b_ref, o_ref):
    s = _silu(s_ref[...])
    o_ref[...] = _dot(s.astype(BF16), w_ref[...].astype(BF16)) + b_ref[...]


def _ada_table(s_in, ada_w, ada_b):
    tn = 1024
    n_cols = N_MOD * D_MODEL
    return pl.pallas_call(
        _ada_kernel,
        out_shape=jax.ShapeDtypeStruct((DEPTH, MOD_ROWS, n_cols), F32),
        grid=(DEPTH, n_cols // tn),
        in_specs=[
            pl.BlockSpec((MOD_ROWS, D_MODEL), lambda l, n: (0, 0)),
            pl.BlockSpec((None, D_MODEL, tn), lambda l, n: (l, 0, n)),
            pl.BlockSpec((None, 1, tn), lambda l, n: (l, 0, n)),
        ],
        out_specs=pl.BlockSpec((None, MOD_ROWS, tn), lambda l, n: (l, 0, n)),
        compiler_params=_params("parallel", "parallel"),
        name="ada_table",
    )(s_in, ada_w, ada_b.reshape(DEPTH, 1, n_cols))


DOWN_COLS = 512


def _ffn_kernel(z_ref, mod_ref, g_ref, wg_ref, wu_ref, wd_ref, *rest, n_f, final):
    if final:
        fn_ref, o_ref, h_scr = rest
    else:
        o_ref, h_scr = rest
    f = pl.program_id(1)
    tm = o_ref.shape[0]

    half = wg_ref.shape[1] // 2

    def gate_up(rows):
        h = h_scr[rows, :]
        return [(_dot(h, wg_ref[:, c:c + half]), _dot(h, wu_ref[:, c:c + half])) for c in (0, half)]

    def activation(gu):
        return (_silu(gu[0]) * gu[1]).astype(BF16)

    def sweep_step(first, last):
        if first:
            parts = []
            for rows in _row_chunks(tm, tm // 4):
                _modulate_rows(z_ref.at[rows], g_ref, mod_ref, h_scr.at[rows])
                parts.append(gate_up(rows))
            gu = [tuple(jnp.concatenate([p[k][i] for p in parts], axis=0) for i in (0, 1)) for k in (0, 1)]
        else:
            gu = gate_up(slice(None))
        a0 = activation(gu[0])
        a1 = activation(gu[1])
        for c in range(0, D_MODEL, DOWN_COLS):
            cols = slice(c, c + DOWN_COLS)
            d = _dot(a0, wd_ref[:half, cols]) + _dot(a1, wd_ref[half:, cols])
            if first:
                o_ref[:, cols] = d
            elif not last:
                o_ref[:, cols] += d
            else:
                half_gate = 0.5 * mod_ref[2:3, cols]
                for rows in _row_chunks(tm, 32):
                    o_ref[rows, cols] = z_ref[rows, cols] + half_gate * (o_ref[rows, cols] + d[rows, :])
        if last and final:
            for rows in _row_chunks(tm, 8):
                o_ref[rows, :] = _rms(o_ref[rows, :]) * fn_ref[...]

    pl.when(f == 0)(functools.partial(sweep_step, True, False))
    pl.when(jnp.logical_and(f > 0, f < n_f - 1))(functools.partial(sweep_step, False, False))
    pl.when(f == n_f - 1)(functools.partial(sweep_step, False, True))


def _ffn(z, mod, l, k, seg0, g, wg, wu, wd, *, final_g=None, tm=1024, tf=512):
    n_rows = z.shape[0]
    n_f = D_FF // tf
    final = final_g is not None
    in_specs = [
        pl.BlockSpec((tm, D_MODEL), lambda i, f: (i, 0)),
        _mod_spec(l, seg0, tm, 2 * k, 2),
        pl.BlockSpec((1, D_MODEL), lambda i, f: (0, 0)),
        pl.BlockSpec((D_MODEL, tf), lambda i, f: (0, f)),
        pl.BlockSpec((D_MODEL, tf), lambda i, f: (0, f)),
        pl.BlockSpec((tf, D_MODEL), lambda i, f: (f, 0)),
    ]
    args = [z, mod, g.reshape(1, D_MODEL), wg, wu, wd]
    if final:
        in_specs.append(pl.BlockSpec((1, D_MODEL), lambda i, f: (0, 0)))
        args.append(final_g.reshape(1, D_MODEL))
    return pl.pallas_call(
        functools.partial(_ffn_kernel, n_f=n_f, final=final),
        out_shape=jax.ShapeDtypeStruct((n_rows, D_MODEL), F32),
        grid=(n_rows // tm, n_f),
        in_specs=in_specs,
        out_specs=pl.BlockSpec((tm, D_MODEL), lambda i, f: (i, 0)),
        scratch_shapes=[pltpu.VMEM((tm, D_MODEL), BF16)],
        compiler_params=_params("parallel", "arbitrary", vmem=VMEM_LIMIT_FFN),
        name="ffn_final" if final else "ffn",
    )(*args)


def _rope(v, cos, sin, lo_half):
    swapped = jnp.where(lo_half, pltpu.roll(v, 112, 1), pltpu.roll(v, 16, 1))
    return v * cos + swapped * sin


def _proj_kernel(*refs, rope, kv_only):
    refs = list(refs)
    x_ref, mod_ref, g_ref, wckv_ref, wkr_ref, wrest_ref, kvn_ref, wukv_ref = refs[:8]
    del refs[:8]
    if not kv_only:
        qn_ref, wuq_ref = refs[:2]
        del refs[:2]
    if rope:
        cos_ref, sin_ref = refs[:2]
        del refs[:2]
    h_scr = refs.pop()
    if kv_only:
        kmla_ref, vmla_ref, dk_ref, dv_ref = refs
    else:
        kmla_ref, vmla_ref, dk_ref, dv_ref, qmla_ref, dq_ref, u_ref = refs

    _modulate_rows(x_ref, g_ref, mod_ref, h_scr)
    hb = h_scr[...]
    if rope:
        lane = lax.broadcasted_iota(jnp.int32, (1, 128), 1)
        rotate = functools.partial(_rope, cos=cos_ref[...], sin=sin_ref[...], lo_half=(lane % 32) < 16)
    else:
        rotate = lambda v: v
    q_scale = MLA_SCALE * LOG2_E
    dq_scale = DIFF_SCALE * LOG2_E

    ckv = _dot(hb, wckv_ref[...])
    cq = None if kv_only else _dot(hb, wrest_ref[:, R_CQ:R_DQ])
    k_rope = _dot(hb, wkr_ref[...])
    dk = _dot(hb, wrest_ref[:, R_DK:R_DV])
    dv = _dot(hb, wrest_ref[:, R_DV:R_CQ])
    if not kv_only:
        dq = _dot(hb, wrest_ref[:, R_DQ:R_U])
        u = _dot(hb, wrest_ref[:, R_U:R_COLS])
    kv = _dot((_rms(ckv) * kvn_ref[...]).astype(BF16), wukv_ref[...])
    if not kv_only:
        q = _dot((_rms(cq) * qn_ref[...]).astype(BF16), wuq_ref[...])

    k_rope = rotate(k_rope).astype(BF16)
    for j in range(DIFF_HEADS):
        dk_ref[:, j * 128:(j + 1) * 128] = rotate(dk[:, j * 128:(j + 1) * 128]).astype(BF16)
    dv_ref[...] = dv.astype(BF16)
    for h in range(MLA_HEADS):
        c0 = h * (MLA_NOPE + MLA_V)
        kmla_ref[:, h * MLA_QK_PAD:h * MLA_QK_PAD + MLA_NOPE] = kv[:, c0:c0 + MLA_NOPE].astype(BF16)
        kmla_ref[:, h * MLA_QK_PAD + MLA_NOPE:(h + 1) * MLA_QK_PAD] = k_rope
        vmla_ref[:, h * MLA_V_PAD:h * MLA_V_PAD + MLA_V] = kv[:, c0 + MLA_NOPE:c0 + MLA_NOPE + MLA_V].astype(BF16)
        vmla_ref[:, h * MLA_V_PAD + MLA_V:(h + 1) * MLA_V_PAD] = jnp.ones((kv.shape[0], MLA_V_PAD - MLA_V), BF16)
    if kv_only:
        return

    for j in range(DIFF_HEADS):
        dq_ref[:, j * 128:(j + 1) * 128] = (rotate(dq[:, j * 128:(j + 1) * 128]) * dq_scale).astype(BF16)
    u_ref[...] = u.astype(BF16)
    for h in range(MLA_HEADS):
        c0 = h * MLA_QK_PAD
        qmla_ref[:, c0:c0 + 128] = (q[:, c0:c0 + 128] * q_scale).astype(BF16)
        qmla_ref[:, c0 + 128:c0 + 256] = (rotate(q[:, c0 + 128:c0 + 256]) * q_scale).astype(BF16)


def _proj(x, mod, l, seg0, g, w_in, kv_norm, w_ukv, q_norm, w_uq, rope_tabs=None, *, kv_only=False, tr=512):
    n_rows = x.shape[0]
    row = lambda w: pl.BlockSpec((tr, w), lambda i: (i, 0))
    vec = lambda w: _resident((None, 1, w), lambda i: (l, 0, 0))
    mat = lambda r, c: _resident((None, r, c), lambda i: (l, 0, 0))
    in_specs = [row(D_MODEL), _mod_spec(l, seg0, tr, 1, 1), _resident((1, D_MODEL), lambda i: (0, 0)),
                mat(D_MODEL, MLA_KV_RANK), mat(D_MODEL, ROPE_PAD), mat(D_MODEL, R_COLS),
                vec(MLA_KV_RANK), mat(MLA_KV_RANK, MLA_HEADS * (MLA_NOPE + MLA_V))]
    args = [x, mod, g.reshape(1, D_MODEL), *w_in, kv_norm.reshape(DEPTH, 1, -1), w_ukv]
    widths = [MLA_HEADS * MLA_QK_PAD, MLA_HEADS * MLA_V_PAD, DIFF_WIDTH, DIFF_WIDTH]
    if not kv_only:
        in_specs += [vec(MLA_Q_RANK), mat(MLA_Q_RANK, MLA_HEADS * MLA_QK_PAD)]
        args += [q_norm.reshape(DEPTH, 1, -1), w_uq]
        widths += [MLA_HEADS * MLA_QK_PAD, DIFF_WIDTH, FOURIER_WIDTH]
    if rope_tabs is not None:
        per_seg = SEG // tr
        in_specs += [pl.BlockSpec((tr, 128), lambda i: (i % per_seg, 0))] * 2
        args += list(rope_tabs)
    return pl.pallas_call(
        functools.partial(_proj_kernel, rope=rope_tabs is not None, kv_only=kv_only),
        out_shape=[jax.ShapeDtypeStruct((n_rows, w), BF16) for w in widths],
        grid=(n_rows // tr,),
        in_specs=in_specs,
        out_specs=[row(w) for w in widths],
        scratch_shapes=[pltpu.VMEM((tr, D_MODEL), BF16)],
        compiler_params=_params("parallel"),
        name="in_proj",
    )(*args)


SUB = 256
CHUNK_VREGS = 36


def _half_rows(n_keys):
    half = 8
    while 2 * half * n_keys <= CHUNK_VREGS * 1024 and 4 * half <= SUB:
        half *= 2
    return half


def _attn_refs(refs, lat):
    if not lat:
        q_ref, kc_ref, vc_ref, o_ref = refs
        return q_ref, kc_ref, vc_ref, None, None, o_ref
    n_cast = (len(refs) - 6) // 2
    for src, dst in zip(refs[5:5 + n_cast], refs[6 + n_cast:]):
        dst[...] = src[...].astype(BF16)
    return (*refs[:5], refs[5 + n_cast])


def _cast_rows(n_rows, n_steps):
    return min(r for r in range(16, n_rows + 1, 16) if n_rows % r == 0 and n_rows // r <= n_steps)


MLA_AHEAD = 1
DIFF_AHEAD = 2


def _pipelined(items, scores, softmax, values, n_slots):
    ahead = n_slots - 1
    for k in range(min(ahead, len(items))):
        scores(k % n_slots, *items[k])
    for i, item in enumerate(items):
        if i + ahead < len(items):
            scores((i + ahead) % n_slots, *items[i + ahead])
        softmax(i % n_slots, i % 2, *item)
        if i > 0:
            values((i - 1) % 2, *items[i - 1])
    values((len(items) - 1) % 2, *items[-1])


def _store_scores(s_scr, slot, q, kc, kl):
    s_scr[slot, :, :CTX_LEN] = _dot_nt(q, kc)
    if kl is not None:
        s_scr[slot, :, CTX_LEN:] = _dot_nt(q, kl)


def _weighted_values(p_scr, slot, vc, vl):
    o = _dot(p_scr[slot, :, :CTX_LEN], vc)
    if vl is not None:
        o = o + _dot(p_scr[slot, :, CTX_LEN:], vl)
    return o


def _chunk_numerators(s):
    return jnp.exp2(s - jnp.max(s, axis=-1, keepdims=True))


def _mla_kernel(*refs, heads, n_sub, lat):
    *io, s_scr, p_scr = refs
    q_ref, kc_ref, vc_ref, kl_ref, vl_ref, o_ref = _attn_refs(io, lat)

    def scores(slot, hh, j):
        qk = slice(hh * MLA_QK_PAD, (hh + 1) * MLA_QK_PAD)
        _store_scores(s_scr, slot, q_ref[j * SUB:(j + 1) * SUB, qk], kc_ref[:, qk], kl_ref[:, qk] if lat else None)

    def softmax(slot, pslot, hh, j):
        half = _half_rows(s_scr.shape[-1])
        for r in range(0, SUB, 2 * half):
            e = [_chunk_numerators(s_scr[slot, r0:r0 + half, :]) for r0 in (r, r + half)]
            p_scr[pslot, r:r + 2 * half, :] = jnp.concatenate(e, axis=0).astype(BF16)

    def values(pslot, hh, j):
        vv = slice(hh * MLA_V_PAD, (hh + 1) * MLA_V_PAD)
        o = _weighted_values(p_scr, pslot, vc_ref[:, vv], vl_ref[:, vv] if lat else None)
        o_ref[j * SUB:(j + 1) * SUB, hh * MLA_V:(hh + 1) * MLA_V] = (o[:, :MLA_V] / o[:, MLA_V:]).astype(BF16)

    _pipelined([(hh, j) for hh in range(heads) for j in range(n_sub)], scores, softmax, values, s_scr.shape[0])


def _mla_scratch(n_keys):
    return [pltpu.VMEM((MLA_AHEAD + 1, SUB, n_keys), F32), pltpu.VMEM((2, SUB, n_keys), BF16)]


def _diff_kernel(lam_ref, sub_ref, *refs, heads, n_sub, lat, lam_init):
    *io, s1_scr, s2_scr, p_scr, d_scr = refs
    q_ref, kc_ref, vc_ref, kl_ref, vl_ref, o_ref = _attn_refs(io, lat)
    lf = lam_ref[...]
    lam = (jnp.exp(jnp.sum(lf[0:1] * lf[1:2], axis=-1, keepdims=True))
           - jnp.exp(jnp.sum(lf[2:3] * lf[3:4], axis=-1, keepdims=True)) + lam_init)
    first = lax.broadcasted_iota(jnp.int32, (1, 2 * DIFF_QK), 1) < DIFF_QK

    def scores(slot, hh, j):
        cols = slice(hh * DIFF_V, (hh + 1) * DIFF_V)
        q = q_ref[j * SUB:(j + 1) * SUB, cols]
        kc = kc_ref[:, cols]
        kl = kl_ref[:, cols] if lat else None
        _store_scores(s1_scr, slot, jnp.where(first, q, jnp.zeros_like(q)), kc, kl)
        _store_scores(s2_scr, slot, jnp.where(first, jnp.zeros_like(q), q), kc, kl)

    def softmax(slot, pslot, hh, j):
        half = _half_rows(s1_scr.shape[-1])
        for r in range(0, SUB, 2 * half):
            p = []
            for r0 in (r, r + half):
                e1 = _chunk_numerators(s1_scr[slot, r0:r0 + half, :])
                e2 = _chunk_numerators(s2_scr[slot, r0:r0 + half, :])
                d1 = jnp.sum(e1, axis=-1, keepdims=True)
                d2 = jnp.sum(e2, axis=-1, keepdims=True)
                p.append(e1 - e2 * (lam * d1 / d2))
                d_scr[pslot, r0:r0 + half, :] = jnp.broadcast_to(d1, (half, DIFF_V))
            p_scr[pslot, r:r + 2 * half, :] = jnp.concatenate(p, axis=0).astype(BF16)

    def values(pslot, hh, j):
        cols = slice(hh * DIFF_V, (hh + 1) * DIFF_V)
        o = _weighted_values(p_scr, pslot, vc_ref[:, cols], vl_ref[:, cols] if lat else None) / d_scr[pslot]
        o_ref[j * SUB:(j + 1) * SUB, cols] = (_rms(o) * sub_ref[...] * (1.0 - lam_init)).astype(BF16)

    _pipelined([(hh, j) for hh in range(heads) for j in range(n_sub)], scores, softmax, values, s1_scr.shape[0])


def _diff_scratch(n_keys):
    return [pltpu.VMEM((DIFF_AHEAD + 1, SUB, n_keys), F32), pltpu.VMEM((DIFF_AHEAD + 1, SUB, n_keys), F32),
            pltpu.VMEM((2, SUB, n_keys), BF16), pltpu.VMEM((2, SUB, DIFF_V), F32)]


def _attention(kernel_fn, scratch_fn, name, q, k_ctx, v_ctx, k_lat=None, v_lat=None, *, n_heads, w_qk, w_v, w_o,
               tq, extra_args=(), extra_specs=(), cast=()):
    if k_lat is not None:
        per_seg = SEQ // tq
        n_steps = BATCH * n_heads * per_seg
        in_specs = list(extra_specs) + [
            pl.BlockSpec((tq, w_qk), lambda b, h, t: (b * per_seg + t, h)),
            pl.BlockSpec((CTX_LEN, w_qk), lambda b, h, t: (b, h)),
            pl.BlockSpec((CTX_LEN, w_v), lambda b, h, t: (b, h)),
            pl.BlockSpec((SEQ, w_qk), lambda b, h, t: (b, h)),
            pl.BlockSpec((SEQ, w_v), lambda b, h, t: (b, h)),
        ]
        out_shape = [jax.ShapeDtypeStruct((LAT_ROWS, n_heads * w_o), BF16)]
        out_specs = [pl.BlockSpec((tq, w_o), lambda b, h, t: (b * per_seg + t, h))]
        for w, l, k in cast:
            n_r, n_c = w.shape[2:]
            rows = _cast_rows(n_r, n_steps)
            blk = lambda b, h, t, last=n_r // rows - 1: jnp.minimum((b * n_heads + h) * per_seg + t, last)
            in_specs.append(pl.BlockSpec((None, None, rows, n_c), lambda b, h, t, l=l, k=k, blk=blk: (l, k, blk(b, h, t), 0)))
            out_shape.append(jax.ShapeDtypeStruct((n_r, n_c), BF16))
            out_specs.append(pl.BlockSpec((rows, n_c), lambda b, h, t, blk=blk: (blk(b, h, t), 0)))
        out = pl.pallas_call(
            functools.partial(kernel_fn, heads=1, n_sub=tq // SUB, lat=True),
            out_shape=out_shape,
            grid=(BATCH, n_heads, per_seg),
            in_specs=in_specs,
            out_specs=out_specs,
            scratch_shapes=scratch_fn(CTX_LEN + SEQ),
            compiler_params=_params(*(("arbitrary",) * 3 if cast else ("parallel", "parallel", "arbitrary"))),
            name=name,
        )(*extra_args, q, k_ctx, v_ctx, k_lat, v_lat, *[w for w, _, _ in cast])
        return out if cast else out[0]
    in_specs = list(extra_specs) + [
        pl.BlockSpec((CTX_LEN, n_heads * w_qk), lambda b: (b, 0)),
        pl.BlockSpec((CTX_LEN, n_heads * w_qk), lambda b: (b, 0)),
        pl.BlockSpec((CTX_LEN, n_heads * w_v), lambda b: (b, 0)),
    ]
    return pl.pallas_call(
        functools.partial(kernel_fn, heads=n_heads, n_sub=CTX_LEN // SUB, lat=False),
        out_shape=jax.ShapeDtypeStruct((CTX_ROWS, n_heads * w_o), BF16),
        grid=(BATCH,),
        in_specs=in_specs,
        out_specs=pl.BlockSpec((CTX_LEN, n_heads * w_o), lambda b: (b, 0)),
        scratch_shapes=scratch_fn(CTX_LEN),
        compiler_params=_params("parallel"),
        name=name + "_ctx",
    )(*extra_args, q, k_ctx, v_ctx)


def _mla_attention(q, k_ctx, v_ctx, k_lat=None, v_lat=None, *, tq=2048, cast=()):
    return _attention(_mla_kernel, _mla_scratch, "mla_attention", q, k_ctx, v_ctx, k_lat, v_lat,
                      n_heads=MLA_HEADS, w_qk=MLA_QK_PAD, w_v=MLA_V_PAD, w_o=MLA_V, tq=tq, cast=cast)


def _diff_attention(diff_lambda, subln, l, lam_init, q, k_ctx, v_ctx, k_lat=None, v_lat=None, *, tq=2048, cast=()):
    layer = lambda *_: (l, 0, 0)
    extra_specs = (pl.BlockSpec((None, 4, DIFF_QK), layer), pl.BlockSpec((None, 1, DIFF_V), layer))
    return _attention(functools.partial(_diff_kernel, lam_init=lam_init), _diff_scratch, "diff_attention",
                      q, k_ctx, v_ctx, k_lat, v_lat,
                      n_heads=DIFF_HEADS, w_qk=2 * DIFF_QK, w_v=DIFF_V, w_o=DIFF_V, tq=tq,
                      extra_args=(diff_lambda, subln.reshape(DEPTH, 1, DIFF_V)), extra_specs=extra_specs, cast=cast)


def _dft_tables(n_pos):
    k = np.arange(n_pos, dtype=np.int64)
    ang = 2.0 * np.pi * ((k[:, None] * k[None, :]) % n_pos) / n_pos
    pos = np.concatenate([np.cos(ang), -np.sin(ang)], axis=1) / math.sqrt(n_pos)
    c = np.arange(FOURIER_CH, dtype=np.int64)
    angc = 2.0 * np.pi * ((c[:, None] * c[None, :]) % FOURIER_CH) / FOURIER_CH
    ch = np.concatenate([np.cos(angc), np.sin(angc)], axis=1) / math.sqrt(FOURIER_CH)
    return jnp.asarray(pos, dtype=BF16), jnp.asarray(ch, dtype=BF16)


def _fourier_kernel(u_ref, ch_ref, pos_ref, o_ref, rhs_scr, *, n_pos):
    @pl.when(pl.program_id(1) == 0)
    def _():
        for g in range(FOURIER_GROUPS):
            a = _dot(u_ref[:, g * FOURIER_CH:(g + 1) * FOURIER_CH], ch_ref[...])
            rhs_scr[0:n_pos, g * FOURIER_CH:(g + 1) * FOURIER_CH] = a[:, :FOURIER_CH].astype(BF16)
            rhs_scr[n_pos:2 * n_pos, g * FOURIER_CH:(g + 1) * FOURIER_CH] = a[:, FOURIER_CH:].astype(BF16)

    o_ref[...] = _dot(pos_ref[...], rhs_scr[...]).astype(BF16)


def _fourier(u, n_pos, *, tr=1024):
    pos_t, ch_t = _dft_tables(n_pos)
    tr = min(tr, n_pos)
    n_j = n_pos // tr
    return pl.pallas_call(
        functools.partial(_fourier_kernel, n_pos=n_pos),
        out_shape=jax.ShapeDtypeStruct((BATCH * n_pos, FOURIER_WIDTH), BF16),
        grid=(BATCH, n_j),
        in_specs=[
            pl.BlockSpec((n_pos, FOURIER_WIDTH), lambda b, j: (b, 0)),
            pl.BlockSpec((FOURIER_CH, 2 * FOURIER_CH), lambda b, j: (0, 0)),
            pl.BlockSpec((tr, 2 * n_pos), lambda b, j: (j, 0)),
        ],
        out_specs=pl.BlockSpec((tr, FOURIER_WIDTH), lambda b, j: (b * n_j + j, 0)),
        scratch_shapes=[pltpu.VMEM((2 * n_pos, FOURIER_WIDTH), BF16)],
        compiler_params=_params("parallel", "arbitrary"),
        name="fourier_mix",
    )(u, ch_t, pos_t)


def _out_kernel(x_ref, mod_ref, om_ref, od_ref, of_ref, w_ref, o_ref):
    for c in range(0, D_MODEL, DOWN_COLS):
        cols = slice(c, c + DOWN_COLS)
        y = (_dot(om_ref[...], w_ref[0:MLA_WIDTH, cols])
             + _dot(od_ref[...], w_ref[MLA_WIDTH:MLA_WIDTH + DIFF_WIDTH, cols])
             + _dot(of_ref[...], w_ref[MLA_WIDTH + DIFF_WIDTH:, cols]))
        o_ref[:, cols] = x_ref[:, cols] + mod_ref[2:3, cols] * y


def _out_proj(x, mod, l, seg0, o_mla, o_diff, o_four, w_out, *, tr=512):
    n_rows = x.shape[0]
    row = lambda w: pl.BlockSpec((tr, w), lambda i: (i, 0))
    return pl.pallas_call(
        _out_kernel,
        out_shape=jax.ShapeDtypeStruct((n_rows, D_MODEL), F32),
        grid=(n_rows // tr,),
        in_specs=[
            row(D_MODEL), _mod_spec(l, seg0, tr, 1, 1),
            row(MLA_WIDTH), row(DIFF_WIDTH), row(FOURIER_WIDTH),
            _resident((None, D_MODEL, D_MODEL), lambda i: (l, 0, 0)),
        ],
        out_specs=row(D_MODEL),
        compiler_params=_params("parallel"),
        name="out_proj",
    )(x, mod, o_mla, o_diff, o_four, w_out)


def _rope_tables():
    rows = SEQ // GRID_W
    pos_r = jnp.repeat(jnp.arange(rows), GRID_W)
    pos_c = jnp.tile(jnp.arange(GRID_W), rows)
    d = MLA_ROPE // 2
    half = d // 2
    inv = ROPE_THETA ** (-2.0 * jnp.arange(half, dtype=F32) / d)

    def tabs(pos):
        ang = pos.astype(F32)[:, None] * inv[None, :]
        return jnp.cos(ang), jnp.sin(ang)

    cr, sr = tabs(pos_r)
    cc, sc = tabs(pos_c)
    cos64 = jnp.concatenate([cr, cr, cc, cc], axis=-1)
    sin64 = jnp.concatenate([-sr, sr, -sc, sc], axis=-1)
    return jnp.tile(cos64, (1, 2)), jnp.tile(sin64, (1, 2))


def kernel(x, c, ctx, c_ctx, ada_w, ada_b, norm_g, ffn_wg, ffn_wu, ffn_wd, w_in, mla_q_norm, mla_kv_norm,
           mla_w_uq, mla_w_ukv, diff_lambda, diff_subln, w_out, final_norm):
    assert DEPTH == 2
    s_in = jnp.concatenate([c_ctx[None, :], c, jnp.zeros((MOD_ROWS - 1 - BATCH, D_MODEL), F32)], axis=0)
    w_in_b = w_in.astype(BF16)
    w_in_p = (w_in_b[..., :IN_KROPE],
              jnp.pad(w_in_b[..., IN_KROPE:IN_REST], ((0, 0), (0, 0), (0, ROPE_PAD - MLA_ROPE))),
              w_in_b[..., IN_REST:])
    w_uq_p = jnp.pad(mla_w_uq.astype(BF16).reshape(DEPTH, MLA_Q_RANK, MLA_HEADS, MLA_NOPE + MLA_ROPE),
                     ((0, 0), (0, 0), (0, 0), (0, MLA_QK_PAD - MLA_NOPE - MLA_ROPE))
                     ).reshape(DEPTH, MLA_Q_RANK, MLA_HEADS * MLA_QK_PAD)
    w_ukv = mla_w_ukv.astype(BF16)
    w_o = w_out.astype(BF16)
    rope_tabs = _rope_tables()
    ffn_w32 = (ffn_wg, ffn_wu, ffn_wd)
    ffn_w = {(0, 0): tuple(w[0, 0].astype(BF16) for w in ffn_w32)}

    mod = _ada_table(s_in, ada_w, ada_b).reshape(DEPTH, MOD_ROWS, 3, 3, D_MODEL)

    zl = x.reshape(LAT_ROWS, D_MODEL)
    zc = ctx.reshape(CTX_ROWS, D_MODEL)
    for l in range(DEPTH):
        last = l == DEPTH - 1
        lam_init = 0.8 - 0.6 * math.exp(-0.3 * l)
        proj = functools.partial(_proj, mod=mod, l=l, g=norm_g[l, 1], w_in=w_in_p, kv_norm=mla_kv_norm,
                                 w_ukv=w_ukv, q_norm=mla_q_norm, w_uq=w_uq_p)
        diff_attn = functools.partial(_diff_attention, diff_lambda, diff_subln, l, lam_init)

        def ffn(z, k, seg0, **kw):
            wg, wu, wd = ffn_w[l, k]
            return _ffn(z, mod, l, k, seg0, norm_g[l, 2 * k], wg, wu, wd, **kw)

        zl = ffn(zl, 0, 1)
        zc = ffn(zc, 0, 0)
        kl, vl, dkl, dvl, ql, dql, ul = proj(zl, seg0=1, rope_tabs=rope_tabs)
        if last:
            kc, vc, dkc, dvc = proj(zc, seg0=0, kv_only=True)
            o_mla = _mla_attention(ql, kc, vc, kl, vl)
            o_diff = diff_attn(dql, dkc, dvc, dkl, dvl)
        else:
            kc, vc, dkc, dvc, qc, dqc, uc = proj(zc, seg0=0)
            o_mla, *w_a = _mla_attention(ql, kc, vc, kl, vl,
                                         cast=[(w, 0, 1) for w in ffn_w32] + [(w, 1, 0) for w in ffn_w32])
            o_diff, *w_b = diff_attn(dql, dkc, dvc, dkl, dvl, cast=[(w, 1, 1) for w in ffn_w32])
            ffn_w.update({(0, 1): tuple(w_a[:3]), (1, 0): tuple(w_a[3:]), (1, 1): tuple(w_b)})
        zl = _out_proj(zl, mod, l, 1, o_mla, o_diff, _fourier(ul, SEQ), w_o)
        if last:
            zl = ffn(zl, 1, 1, final_g=final_norm)
        else:
            zc = _out_proj(zc, mod, l, 0, _mla_attention(qc, kc, vc), diff_attn(dqc, dkc, dvc),
                           _fourier(uc, CTX_LEN), w_o)
            zl = ffn(zl, 1, 1)
            zc = ffn(zc, 1, 0)
    return zl.reshape(BATCH, SEQ, D_MODEL)
```

```python
import functools
import math

import numpy as np
import jax
import jax.numpy as jnp
from jax import lax
from jax.experimental import pallas as pl
from jax.experimental.pallas import tpu as pltpu

D_MODEL = 2048
BATCH = 8
SEQ = 2048
DEPTH = 2
GRID_W = 64
CTX_LEN = 256
EPS = 1e-6
ROPE_THETA = 10000.0
D_FF = 5632
N_MOD = 9

MLA_HEADS = 8
MLA_Q_RANK = 512
MLA_KV_RANK = 256
MLA_NOPE = 128
MLA_ROPE = 64
MLA_V = 128
MLA_SCALE = (MLA_NOPE + MLA_ROPE) ** -0.5
MLA_QK_PAD = 256
MLA_V_PAD = 256

DIFF_HEADS = 4
DIFF_QK = 64
DIFF_V = 2 * DIFF_QK
DIFF_SCALE = DIFF_QK ** -0.5

FOURIER_GROUPS = 4
FOURIER_CH = 128
MLA_WIDTH = MLA_HEADS * MLA_V
DIFF_WIDTH = DIFF_HEADS * DIFF_V
FOURIER_WIDTH = FOURIER_GROUPS * FOURIER_CH

SEG = SEQ
LAT_ROWS = BATCH * SEQ
CTX_ROWS = BATCH * CTX_LEN
MOD_ROWS = 16

IN_KROPE = MLA_KV_RANK
IN_REST = MLA_KV_RANK + MLA_ROPE
ROPE_PAD = 128
R_DK = 0
R_DV = 512
R_CQ = 1024
R_DQ = 1536
R_U = 2048
R_COLS = 2560

VMEM_LIMIT = 56 * 1024 * 1024
VMEM_LIMIT_FFN = 63 * 1024 * 1024 + 512 * 1024
LOG2_E = math.log2(math.e)

F32 = jnp.float32
BF16 = jnp.bfloat16


def _silu(v):
    return v / (1.0 + jnp.exp(-v))


def _dot(a, b):
    return jnp.dot(a, b, preferred_element_type=F32)


def _dot_nt(a, b):
    return lax.dot_general(a, b, (((1,), (1,)), ((), ())), preferred_element_type=F32)


def _rms(v):
    return v * lax.rsqrt(jnp.mean(v * v, axis=-1, keepdims=True) + EPS)


def _row_chunks(n_rows, chunk):
    return [slice(r, r + chunk) for r in range(0, n_rows, chunk)]


def _modulate_rows(z_ref, g_ref, mod_ref, h_ref):
    gain = g_ref[...] * (1.0 + mod_ref[1:2, :])
    shift = mod_ref[0:1, :]
    for rows in _row_chunks(z_ref.shape[0], 16):
        h_ref[rows, :] = (_rms(z_ref[rows, :]) * gain + shift).astype(BF16)


def _params(*sem, vmem=VMEM_LIMIT):
    return pltpu.CompilerParams(dimension_semantics=sem, vmem_limit_bytes=vmem)


def _resident(shape, index_map):
    return pl.BlockSpec(shape, index_map, pipeline_mode=pl.Buffered(1))


def _mod_spec(l, seg0, tile, grp, n_grid):
    if n_grid == 1:
        return pl.BlockSpec((None, None, None, 3, D_MODEL), lambda i: (l, seg0 + i * tile // SEG, grp, 0, 0))
    return pl.BlockSpec((None, None, None, 3, D_MODEL), lambda i, f: (l, seg0 + i * tile // SEG, grp, 0, 0))


def _ada_kernel(s_ref, w_ref, b_ref, o_ref):
    s = _silu(s_ref[...])
    o_ref[...] = _dot(s.astype(BF16), w_ref[...].astype(BF16)) + b_ref[...]


def _ada_table(s_in, ada_w, ada_b):
    tn = 1024
    n_cols = N_MOD * D_MODEL
    return pl.pallas_call(
        _ada_kernel,
        out_shape=jax.ShapeDtypeStruct((DEPTH, MOD_ROWS, n_cols), F32),
        grid=(DEPTH, n_cols // tn),
        in_specs=[
            pl.BlockSpec((MOD_ROWS, D_MODEL), lambda l, n: (0, 0)),
            pl.BlockSpec((None, D_MODEL, tn), lambda l, n: (l, 0, n)),
            pl.BlockSpec((None, 1, tn), lambda l, n: (l, 0, n)),
        ],
        out_specs=pl.BlockSpec((None, MOD_ROWS, tn), lambda l, n: (l, 0, n)),
        compiler_params=_params("parallel", "parallel"),
        name="ada_table",
    )(s_in, ada_w, ada_b.reshape(DEPTH, 1, n_cols))


DOWN_COLS = 512


def _ffn_kernel(z_ref, mod_ref, g_ref, wg_ref, wu_ref, wd_ref, *rest, n_f, final):
    if final:
        fn_ref, o_ref, h_scr = rest
    else:
        o_ref, h_scr = rest
    f = pl.program_id(1)
    tm = o_ref.shape[0]

    half = wg_ref.shape[1] // 2

    def gate_up(rows):
        h = h_scr[rows, :]
        return [(_dot(h, wg_ref[:, c:c + half]), _dot(h, wu_ref[:, c:c + half])) for c in (0, half)]

    def activation(gu):
        return (_silu(gu[0]) * gu[1]).astype(BF16)

    def sweep_step(first, last):
        if first:
            parts = []
            for rows in _row_chunks(tm, tm // 4):
                _modulate_rows(z_ref.at[rows], g_ref, mod_ref, h_scr.at[rows])
                parts.append(gate_up(rows))
            gu = [tuple(jnp.concatenate([p[k][i] for p in parts], axis=0) for i in (0, 1)) for k in (0, 1)]
        else:
            gu = gate_up(slice(None))
        a0 = activation(gu[0])
        a1 = activation(gu[1])
        for c in range(0, D_MODEL, DOWN_COLS):
            cols = slice(c, c + DOWN_COLS)
            d = _dot(a0, wd_ref[:half, cols]) + _dot(a1, wd_ref[half:, cols])
            if first:
                o_ref[:, cols] = d
            elif not last:
                o_ref[:, cols] += d
            else:
                half_gate = 0.5 * mod_ref[2:3, cols]
                for rows in _row_chunks(tm, 32):
                    o_ref[rows, cols] = z_ref[rows, cols] + half_gate * (o_ref[rows, cols] + d[rows, :])
        if last and final:
            for rows in _row_chunks(tm, 8):
                o_ref[rows, :] = _rms(o_ref[rows, :]) * fn_ref[...]

    pl.when(f == 0)(functools.partial(sweep_step, True, False))
    pl.when(jnp.logical_and(f > 0, f < n_f - 1))(functools.partial(sweep_step, False, False))
    pl.when(f == n_f - 1)(functools.partial(sweep_step, False, True))


def _ffn(z, mod, l, k, seg0, g, wg, wu, wd, *, final_g=None, tm=1024, tf=512):
    n_rows = z.shape[0]
    n_f = D_FF // tf
    final = final_g is not None
    in_specs = [
        pl.BlockSpec((tm, D_MODEL), lambda i, f: (i, 0)),
        _mod_spec(l, seg0, tm, 2 * k, 2),
        pl.BlockSpec((1, D_MODEL), lambda i, f: (0, 0)),
        pl.BlockSpec((D_MODEL, tf), lambda i, f: (0, f)),
        pl.BlockSpec((D_MODEL, tf), lambda i, f: (0, f)),
        pl.BlockSpec((tf, D_MODEL), lambda i, f: (f, 0)),
    ]
    args = [z, mod, g.reshape(1, D_MODEL), wg, wu, wd]
    if final:
        in_specs.append(pl.BlockSpec((1, D_MODEL), lambda i, f: (0, 0)))
        args.append(final_g.reshape(1, D_MODEL))
    return pl.pallas_call(
        functools.partial(_ffn_kernel, n_f=n_f, final=final),
        out_shape=jax.ShapeDtypeStruct((n_rows, D_MODEL), F32),
        grid=(n_rows // tm, n_f),
        in_specs=in_specs,
        out_specs=pl.BlockSpec((tm, D_MODEL), lambda i, f: (i, 0)),
        scratch_shapes=[pltpu.VMEM((tm, D_MODEL), BF16)],
        compiler_params=_params("parallel", "arbitrary", vmem=VMEM_LIMIT_FFN),
        name="ffn_final" if final else "ffn",
    )(*args)


def _rope(v, cos, sin, lo_half):
    swapped = jnp.where(lo_half, pltpu.roll(v, 112, 1), pltpu.roll(v, 16, 1))
    return v * cos + swapped * sin


def _proj_kernel(*refs, rope, kv_only):
    refs = list(refs)
    x_ref, mod_ref, g_ref, wckv_ref, wkr_ref, wrest_ref, kvn_ref, wukv_ref = refs[:8]
    del refs[:8]
    if not kv_only:
        qn_ref, wuq_ref = refs[:2]
        del refs[:2]
    if rope:
        cos_ref, sin_ref = refs[:2]
        del refs[:2]
    h_scr = refs.pop()
    if kv_only:
        kmla_ref, vmla_ref, dk_ref, dv_ref = refs
    else:
        kmla_ref, vmla_ref, dk_ref, dv_ref, qmla_ref, dq_ref, u_ref = refs

    _modulate_rows(x_ref, g_ref, mod_ref, h_scr)
    hb = h_scr[...]
    if rope:
        lane = lax.broadcasted_iota(jnp.int32, (1, 128), 1)
        rotate = functools.partial(_rope, cos=cos_ref[...], sin=sin_ref[...], lo_half=(lane % 32) < 16)
    else:
        rotate = lambda v: v
    q_scale = MLA_SCALE * LOG2_E
    dq_scale = DIFF_SCALE * LOG2_E

    ckv = _dot(hb, wckv_ref[...])
    cq = None if kv_only else _dot(hb, wrest_ref[:, R_CQ:R_DQ])
    k_rope = _dot(hb, wkr_ref[...])
    dk = _dot(hb, wrest_ref[:, R_DK:R_DV])
    dv = _dot(hb, wrest_ref[:, R_DV:R_CQ])
    if not kv_only:
        dq = _dot(hb, wrest_ref[:, R_DQ:R_U])
        u = _dot(hb, wrest_ref[:, R_U:R_COLS])
    kv = _dot((_rms(ckv) * kvn_ref[...]).astype(BF16), wukv_ref[...])
    if not kv_only:
        q = _dot((_rms(cq) * qn_ref[...]).astype(BF16), wuq_ref[...])

    k_rope = rotate(k_rope).astype(BF16)
    for j in range(DIFF_HEADS):
        dk_ref[:, j * 128:(j + 1) * 128] = rotate(dk[:, j * 128:(j + 1) * 128]).astype(BF16)
    dv_ref[...] = dv.astype(BF16)
    for h in range(MLA_HEADS):
        c0 = h * (MLA_NOPE + MLA_V)
        kmla_ref[:, h * MLA_QK_PAD:h * MLA_QK_PAD + MLA_NOPE] = kv[:, c0:c0 + MLA_NOPE].astype(BF16)
        kmla_ref[:, h * MLA_QK_PAD + MLA_NOPE:(h + 1) * MLA_QK_PAD] = k_rope
        vmla_ref[:, h * MLA_V_PAD:h * MLA_V_PAD + MLA_V] = kv[:, c0 + MLA_NOPE:c0 + MLA_NOPE + MLA_V].astype(BF16)
        vmla_ref[:, h * MLA_V_PAD + MLA_V:(h + 1) * MLA_V_PAD] = jnp.ones((kv.shape[0], MLA_V_PAD - MLA_V), BF16)
    if kv_only:
        return

    for j in range(DIFF_HEADS):
        dq_ref[:, j * 128:(j + 1) * 128] = (rotate(dq[:, j * 128:(j + 1) * 128]) * dq_scale).astype(BF16)
    u_ref[...] = u.astype(BF16)
    for h in range(MLA_HEADS):
        c0 = h * MLA_QK_PAD
        qmla_ref[:, c0:c0 + 128] = (q[:, c0:c0 + 128] * q_scale).astype(BF16)
        qmla_ref[:, c0 + 128:c0 + 256] = (rotate(q[:, c0 + 128:c0 + 256]) * q_scale).astype(BF16)


def _proj(x, mod, l, seg0, g, w_in, kv_norm, w_ukv, q_norm, w_uq, rope_tabs=None, *, kv_only=False, tr=512):
    n_rows = x.shape[0]
    row = lambda w: pl.BlockSpec((tr, w), lambda i: (i, 0))
    vec = lambda w: _resident((None, 1, w), lambda i: (l, 0, 0))
    mat = lambda r, c: _resident((None, r, c), lambda i: (l, 0, 0))
    in_specs = [row(D_MODEL), _mod_spec(l, seg0, tr, 1, 1), _resident((1, D_MODEL), lambda i: (0, 0)),
                mat(D_MODEL, MLA_KV_RANK), mat(D_MODEL, ROPE_PAD), mat(D_MODEL, R_COLS),
                vec(MLA_KV_RANK), mat(MLA_KV_RANK, MLA_HEADS * (MLA_NOPE + MLA_V))]
    args = [x, mod, g.reshape(1, D_MODEL), *w_in, kv_norm.reshape(DEPTH, 1, -1), w_ukv]
    widths = [MLA_HEADS * MLA_QK_PAD, MLA_HEADS * MLA_V_PAD, DIFF_WIDTH, DIFF_WIDTH]
    if not kv_only:
        in_specs += [vec(MLA_Q_RANK), mat(MLA_Q_RANK, MLA_HEADS * MLA_QK_PAD)]
        args += [q_norm.reshape(DEPTH, 1, -1), w_uq]
        widths += [MLA_HEADS * MLA_QK_PAD, DIFF_WIDTH, FOURIER_WIDTH]
    if rope_tabs is not None:
        per_seg = SEG // tr
        in_specs += [pl.BlockSpec((tr, 128), lambda i: (i % per_seg, 0))] * 2
        args += list(rope_tabs)
    return pl.pallas_call(
        functools.partial(_proj_kernel, rope=rope_tabs is not None, kv_only=kv_only),
        out_shape=[jax.ShapeDtypeStruct((n_rows, w), BF16) for w in widths],
        grid=(n_rows // tr,),
        in_specs=in_specs,
        out_specs=[row(w) for w in widths],
        scratch_shapes=[pltpu.VMEM((tr, D_MODEL), BF16)],
        compiler_params=_params("parallel"),
        name="in_proj",
    )(*args)


SUB = 256
CHUNK_VREGS = 36


def _half_rows(n_keys):
    half = 8
    while 2 * half * n_keys <= CHUNK_VREGS * 1024 and 4 * half <= SUB:
        half *= 2
    return half


def _attn_refs(refs, lat):
    if not lat:
        q_ref, kc_ref, vc_ref, o_ref = refs
        return q_ref, kc_ref, vc_ref, None, None, o_ref
    n_cast = (len(refs) - 6) // 2
    for src, dst in zip(refs[5:5 + n_cast], refs[6 + n_cast:]):
        dst[...] = src[...].astype(BF16)
    return (*refs[:5], refs[5 + n_cast])


def _cast_rows(n_rows, n_steps):
    return min(r for r in range(16, n_rows + 1, 16) if n_rows % r == 0 and n_rows // r <= n_steps)


MLA_AHEAD = 1
DIFF_AHEAD = 2


def _pipelined(items, scores, softmax, values, n_slots):
    ahead = n_slots - 1
    for k in range(min(ahead, len(items))):
        scores(k % n_slots, *items[k])
    for i, item in enumerate(items):
        if i + ahead < len(items):
            scores((i + ahead) % n_slots, *items[i + ahead])
        softmax(i % n_slots, i % 2, *item)
        if i > 0:
            values((i - 1) % 2, *items[i - 1])
    values((len(items) - 1) % 2, *items[-1])


def _store_scores(s_scr, slot, q, kc, kl):
    s_scr[slot, :, :CTX_LEN] = _dot_nt(q, kc)
    if kl is not None:
        s_scr[slot, :, CTX_LEN:] = _dot_nt(q, kl)


def _weighted_values(p_scr, slot, vc, vl):
    o = _dot(p_scr[slot, :, :CTX_LEN], vc)
    if vl is not None:
        o = o + _dot(p_scr[slot, :, CTX_LEN:], vl)
    return o


def _chunk_numerators(s):
    return jnp.exp2(s - jnp.max(s, axis=-1, keepdims=True))


def _mla_kernel(*refs, heads, n_sub, lat):
    *io, s_scr, p_scr = refs
    q_ref, kc_ref, vc_ref, kl_ref, vl_ref, o_ref = _attn_refs(io, lat)

    def scores(slot, hh, j):
        qk = slice(hh * MLA_QK_PAD, (hh + 1) * MLA_QK_PAD)
        _store_scores(s_scr, slot, q_ref[j * SUB:(j + 1) * SUB, qk], kc_ref[:, qk], kl_ref[:, qk] if lat else None)

    def softmax(slot, pslot, hh, j):
        half = _half_rows(s_scr.shape[-1])
        for r in range(0, SUB, 2 * half):
            e = [_chunk_numerators(s_scr[slot, r0:r0 + half, :]) for r0 in (r, r + half)]
            p_scr[pslot, r:r + 2 * half, :] = jnp.concatenate(e, axis=0).astype(BF16)

    def values(pslot, hh, j):
        vv = slice(hh * MLA_V_PAD, (hh + 1) * MLA_V_PAD)
        o = _weighted_values(p_scr, pslot, vc_ref[:, vv], vl_ref[:, vv] if lat else None)
        o_ref[j * SUB:(j + 1) * SUB, hh * MLA_V:(hh + 1) * MLA_V] = (o[:, :MLA_V] / o[:, MLA_V:]).astype(BF16)

    _pipelined([(hh, j) for hh in range(heads) for j in range(n_sub)], scores, softmax, values, s_scr.shape[0])


def _mla_scratch(n_keys):
    return [pltpu.VMEM((MLA_AHEAD + 1, SUB, n_keys), F32), pltpu.VMEM((2, SUB, n_keys), BF16)]


def _diff_kernel(lam_ref, sub_ref, *refs, heads, n_sub, lat, lam_init):
    *io, s1_scr, s2_scr, p_scr, d_scr = refs
    q_ref, kc_ref, vc_ref, kl_ref, vl_ref, o_ref = _attn_refs(io, lat)
    lf = lam_ref[...]
    lam = (jnp.exp(jnp.sum(lf[0:1] * lf[1:2], axis=-1, keepdims=True))
           - jnp.exp(jnp.sum(lf[2:3] * lf[3:4], axis=-1, keepdims=True)) + lam_init)
    first = lax.broadcasted_iota(jnp.int32, (1, 2 * DIFF_QK), 1) < DIFF_QK

    def scores(slot, hh, j):
        cols = slice(hh * DIFF_V, (hh + 1) * DIFF_V)
        q = q_ref[j * SUB:(j + 1) * SUB, cols]
        kc = kc_ref[:, cols]
        kl = kl_ref[:, cols] if lat else None
        _store_scores(s1_scr, slot, jnp.where(first, q, jnp.zeros_like(q)), kc, kl)
        _store_scores(s2_scr, slot, jnp.where(first, jnp.zeros_like(q), q), kc, kl)

    def softmax(slot, pslot, hh, j):
        half = _half_rows(s1_scr.shape[-1])
        for r in range(0, SUB, 2 * half):
            p = []
            for r0 in (r, r + half):
                e1 = _chunk_numerators(s1_scr[slot, r0:r0 + half, :])
                e2 = _chunk_numerators(s2_scr[slot, r0:r0 + half, :])
                d1 = jnp.sum(e1, axis=-1, keepdims=True)
                d2 = jnp.sum(e2, axis=-1, keepdims=True)
                p.append(e1 - e2 * (lam * d1 / d2))
                d_scr[pslot, r0:r0 + half, :] = jnp.broadcast_to(d1, (half, DIFF_V))
            p_scr[pslot, r:r + 2 * half, :] = jnp.concatenate(p, axis=0).astype(BF16)

    def values(pslot, hh, j):
        cols = slice(hh * DIFF_V, (hh + 1) * DIFF_V)
        o = _weighted_values(p_scr, pslot, vc_ref[:, cols], vl_ref[:, cols] if lat else None) / d_scr[pslot]
        o_ref[j * SUB:(j + 1) * SUB, cols] = (_rms(o) * sub_ref[...] * (1.0 - lam_init)).astype(BF16)

    _pipelined([(hh, j) for hh in range(heads) for j in range(n_sub)], scores, softmax, values, s1_scr.shape[0])


def _diff_scratch(n_keys):
    return [pltpu.VMEM((DIFF_AHEAD + 1, SUB, n_keys), F32), pltpu.VMEM((DIFF_AHEAD + 1, SUB, n_keys), F32),
            pltpu.VMEM((2, SUB, n_keys), BF16), pltpu.VMEM((2, SUB, DIFF_V), F32)]


def _attention(kernel_fn, scratch_fn, name, q, k_ctx, v_ctx, k_lat=None, v_lat=None, *, n_heads, w_qk, w_v, w_o,
               tq, extra_args=(), extra_specs=(), cast=()):
    if k_lat is not None:
        per_seg = SEQ // tq
        n_steps = BATCH * n_heads * per_seg
        in_specs = list(extra_specs) + [
            pl.BlockSpec((tq, w_qk), lambda b, h, t: (b * per_seg + t, h)),
            pl.BlockSpec((CTX_LEN, w_qk), lambda b, h, t: (b, h)),
            pl.BlockSpec((CTX_LEN, w_v), lambda b, h, t: (b, h)),
            pl.BlockSpec((SEQ, w_qk), lambda b, h, t: (b, h)),
            pl.BlockSpec((SEQ, w_v), lambda b, h, t: (b, h)),
        ]
        out_shape = [jax.ShapeDtypeStruct((LAT_ROWS, n_heads * w_o), BF16)]
        out_specs = [pl.BlockSpec((tq, w_o), lambda b, h, t: (b * per_seg + t, h))]
        for w, l, k in cast:
            n_r, n_c = w.shape[2:]
            rows = _cast_rows(n_r, n_steps)
            blk = lambda b, h, t, last=n_r // rows - 1: jnp.minimum((b * n_heads + h) * per_seg + t, last)
            in_specs.append(pl.BlockSpec((None, None, rows, n_c), lambda b, h, t, l=l, k=k, blk=blk: (l, k, blk(b, h, t), 0)))
            out_shape.append(jax.ShapeDtypeStruct((n_r, n_c), BF16))
            out_specs.append(pl.BlockSpec((rows, n_c), lambda b, h, t, blk=blk: (blk(b, h, t), 0)))
        out = pl.pallas_call(
            functools.partial(kernel_fn, heads=1, n_sub=tq // SUB, lat=True),
            out_shape=out_shape,
            grid=(BATCH, n_heads, per_seg),
            in_specs=in_specs,
            out_specs=out_specs,
            scratch_shapes=scratch_fn(CTX_LEN + SEQ),
            compiler_params=_params(*(("arbitrary",) * 3 if cast else ("parallel", "parallel", "arbitrary"))),
            name=name,
        )(*extra_args, q, k_ctx, v_ctx, k_lat, v_lat, *[w for w, _, _ in cast])
        return out if cast else out[0]
    in_specs = list(extra_specs) + [
        pl.BlockSpec((CTX_LEN, n_heads * w_qk), lambda b: (b, 0)),
        pl.BlockSpec((CTX_LEN, n_heads * w_qk), lambda b: (b, 0)),
        pl.BlockSpec((CTX_LEN, n_heads * w_v), lambda b: (b, 0)),
    ]
    return pl.pallas_call(
        functools.partial(kernel_fn, heads=n_heads, n_sub=CTX_LEN // SUB, lat=False),
        out_shape=jax.ShapeDtypeStruct((CTX_ROWS, n_heads * w_o), BF16),
        grid=(BATCH,),
        in_specs=in_specs,
        out_specs=pl.BlockSpec((CTX_LEN, n_heads * w_o), lambda b: (b, 0)),
        scratch_shapes=scratch_fn(CTX_LEN),
        compiler_params=_params("parallel"),
        name=name + "_ctx",
    )(*extra_args, q, k_ctx, v_ctx)


def _mla_attention(q, k_ctx, v_ctx, k_lat=None, v_lat=None, *, tq=2048, cast=()):
    return _attention(_mla_kernel, _mla_scratch, "mla_attention", q, k_ctx, v_ctx, k_lat, v_lat,
                      n_heads=MLA_HEADS, w_qk=MLA_QK_PAD, w_v=MLA_V_PAD, w_o=MLA_V, tq=tq, cast=cast)


def _diff_attention(diff_lambda, subln, l, lam_init, q, k_ctx, v_ctx, k_lat=None, v_lat=None, *, tq=2048, cast=()):
    layer = lambda *_: (l, 0, 0)
    extra_specs = (pl.BlockSpec((None, 4, DIFF_QK), layer), pl.BlockSpec((None, 1, DIFF_V), layer))
    return _attention(functools.partial(_diff_kernel, lam_init=lam_init), _diff_scratch, "diff_attention",
                      q, k_ctx, v_ctx, k_lat, v_lat,
                      n_heads=DIFF_HEADS, w_qk=2 * DIFF_QK, w_v=DIFF_V, w_o=DIFF_V, tq=tq,
                      extra_args=(diff_lambda, subln.reshape(DEPTH, 1, DIFF_V)), extra_specs=extra_specs, cast=cast)


def _dft_tables(n_pos):
    half = n_pos // 2
    k = np.arange(half, dtype=np.int64)[:, None]
    m = np.arange(half, dtype=np.int64)[None, :]
    tabs = []
    for parity in (0, 1):
        ang = 2.0 * np.pi * ((k * (2 * m + parity)) % n_pos) / n_pos
        tabs.append(jnp.asarray(np.concatenate([np.cos(ang), -np.sin(ang)], axis=1) / math.sqrt(n_pos), dtype=BF16))
    c = np.arange(FOURIER_CH, dtype=np.int64)
    angc = 2.0 * np.pi * ((c[:, None] * c[None, :]) % FOURIER_CH) / FOURIER_CH
    ch = np.concatenate([np.cos(angc), np.sin(angc)], axis=1) / math.sqrt(FOURIER_CH)
    return tabs[0], tabs[1], jnp.asarray(ch, dtype=BF16)


def _fourier_kernel(u_ref, ch_ref, te_ref, to_ref, o_ref, a_scr, rhs_e, rhs_o, *, n_pos):
    half = n_pos // 2
    for g in range(FOURIER_GROUPS):
        cols = slice(g * FOURIER_CH, (g + 1) * FOURIER_CH)
        a = _dot(u_ref[:, cols], ch_ref[...])
        a_scr[2 * g] = a[:, :FOURIER_CH]
        a_scr[2 * g + 1] = a[:, FOURIER_CH:]
    for g in range(FOURIER_GROUPS):
        cols = slice(g * FOURIER_CH, (g + 1) * FOURIER_CH)
        for parity, rhs in ((0, rhs_e), (1, rhs_o)):
            picked = pl.ds(parity, half, stride=2)
            rhs[0:half, cols] = a_scr[2 * g, picked, :].astype(BF16)
            rhs[half:n_pos, cols] = a_scr[2 * g + 1, picked, :].astype(BF16)
    p = _dot(te_ref[...], rhs_e[...])
    q = _dot(to_ref[...], rhs_o[...])
    o_ref[0:half, :] = (p + q).astype(BF16)
    o_ref[half:n_pos, :] = (p - q).astype(BF16)


def _fourier(u, n_pos):
    t_even, t_odd, ch_t = _dft_tables(n_pos)
    half = n_pos // 2
    return pl.pallas_call(
        functools.partial(_fourier_kernel, n_pos=n_pos),
        out_shape=jax.ShapeDtypeStruct((BATCH * n_pos, FOURIER_WIDTH), BF16),
        grid=(BATCH,),
        in_specs=[
            pl.BlockSpec((n_pos, FOURIER_WIDTH), lambda b: (b, 0)),
            _resident((FOURIER_CH, 2 * FOURIER_CH), lambda b: (0, 0)),
            _resident((half, n_pos), lambda b: (0, 0)),
            _resident((half, n_pos), lambda b: (0, 0)),
        ],
        out_specs=pl.BlockSpec((n_pos, FOURIER_WIDTH), lambda b: (b, 0)),
        scratch_shapes=[pltpu.VMEM((2 * FOURIER_GROUPS, n_pos, FOURIER_CH), F32),
                        pltpu.VMEM((n_pos, FOURIER_WIDTH), BF16), pltpu.VMEM((n_pos, FOURIER_WIDTH), BF16)],
        compiler_params=_params("parallel"),
        name="fourier_mix",
    )(u, ch_t, t_even, t_odd)


def _out_kernel(x_ref, mod_ref, om_ref, od_ref, of_ref, w_ref, o_ref):
    for c in range(0, D_MODEL, DOWN_COLS):
        cols = slice(c, c + DOWN_COLS)
        y = (_dot(om_ref[...], w_ref[0:MLA_WIDTH, cols])
             + _dot(od_ref[...], w_ref[MLA_WIDTH:MLA_WIDTH + DIFF_WIDTH, cols])
             + _dot(of_ref[...], w_ref[MLA_WIDTH + DIFF_WIDTH:, cols]))
        o_ref[:, cols] = x_ref[:, cols] + mod_ref[2:3, cols] * y


def _out_proj(x, mod, l, seg0, o_mla, o_diff, o_four, w_out, *, tr=512):
    n_rows = x.shape[0]
    row = lambda w: pl.BlockSpec((tr, w), lambda i: (i, 0))
    return pl.pallas_call(
        _out_kernel,
        out_shape=jax.ShapeDtypeStruct((n_rows, D_MODEL), F32),
        grid=(n_rows // tr,),
        in_specs=[
            row(D_MODEL), _mod_spec(l, seg0, tr, 1, 1),
            row(MLA_WIDTH), row(DIFF_WIDTH), row(FOURIER_WIDTH),
            _resident((None, D_MODEL, D_MODEL), lambda i: (l, 0, 0)),
        ],
        out_specs=row(D_MODEL),
        compiler_params=_params("parallel"),
        name="out_proj",
    )(x, mod, o_mla, o_diff, o_four, w_out)


def _rope_tables():
    rows = SEQ // GRID_W
    pos_r = jnp.repeat(jnp.arange(rows), GRID_W)
    pos_c = jnp.tile(jnp.arange(GRID_W), rows)
    d = MLA_ROPE // 2
    half = d // 2
    inv = ROPE_THETA ** (-2.0 * jnp.arange(half, dtype=F32) / d)

    def tabs(pos):
        ang = pos.astype(F32)[:, None] * inv[None, :]
        return jnp.cos(ang), jnp.sin(ang)

    cr, sr = tabs(pos_r)
    cc, sc = tabs(pos_c)
    cos64 = jnp.concatenate([cr, cr, cc, cc], axis=-1)
    sin64 = jnp.concatenate([-sr, sr, -sc, sc], axis=-1)
    return jnp.tile(cos64, (1, 2)), jnp.tile(sin64, (1, 2))


def kernel(x, c, ctx, c_ctx, ada_w, ada_b, norm_g, ffn_wg, ffn_wu, ffn_wd, w_in, mla_q_norm, mla_kv_norm,
           mla_w_uq, mla_w_ukv, diff_lambda, diff_subln, w_out, final_norm):
    assert DEPTH == 2
    s_in = jnp.concatenate([c_ctx[None, :], c, jnp.zeros((MOD_ROWS - 1 - BATCH, D_MODEL), F32)], axis=0)
    w_in_b = w_in.astype(BF16)
    w_in_p = (w_in_b[..., :IN_KROPE],
              jnp.pad(w_in_b[..., IN_KROPE:IN_REST], ((0, 0), (0, 0), (0, ROPE_PAD - MLA_ROPE))),
              w_in_b[..., IN_REST:])
    w_uq_p = jnp.pad(mla_w_uq.astype(BF16).reshape(DEPTH, MLA_Q_RANK, MLA_HEADS, MLA_NOPE + MLA_ROPE),
                     ((0, 0), (0, 0), (0, 0), (0, MLA_QK_PAD - MLA_NOPE - MLA_ROPE))
                     ).reshape(DEPTH, MLA_Q_RANK, MLA_HEADS * MLA_QK_PAD)
    w_ukv = mla_w_ukv.astype(BF16)
    w_o = w_out.astype(BF16)
    rope_tabs = _rope_tables()
    ffn_w32 = (ffn_wg, ffn_wu, ffn_wd)
    ffn_w = {(0, 0): tuple(w[0, 0].astype(BF16) for w in ffn_w32)}

    mod = _ada_table(s_in, ada_w, ada_b).reshape(DEPTH, MOD_ROWS, 3, 3, D_MODEL)

    zl = x.reshape(LAT_ROWS, D_MODEL)
    zc = ctx.reshape(CTX_ROWS, D_MODEL)
    for l in range(DEPTH):
        last = l == DEPTH - 1
        lam_init = 0.8 - 0.6 * math.exp(-0.3 * l)
        proj = functools.partial(_proj, mod=mod, l=l, g=norm_g[l, 1], w_in=w_in_p, kv_norm=mla_kv_norm,
                                 w_ukv=w_ukv, q_norm=mla_q_norm, w_uq=w_uq_p)
        diff_attn = functools.partial(_diff_attention, diff_lambda, diff_subln, l, lam_init)

        def ffn(z, k, seg0, **kw):
            wg, wu, wd = ffn_w[l, k]
            return _ffn(z, mod, l, k, seg0, norm_g[l, 2 * k], wg, wu, wd, **kw)

        zl = ffn(zl, 0, 1)
        zc = ffn(zc, 0, 0)
        kl, vl, dkl, dvl, ql, dql, ul = proj(zl, seg0=1, rope_tabs=rope_tabs)
        if last:
            kc, vc, dkc, dvc = proj(zc, seg0=0, kv_only=True)
            o_mla = _mla_attention(ql, kc, vc, kl, vl)
            o_diff = diff_attn(dql, dkc, dvc, dkl, dvl)
        else:
            kc, vc, dkc, dvc, qc, dqc, uc = proj(zc, seg0=0)
            o_mla, *w_a = _mla_attention(ql, kc, vc, kl, vl,
                                         cast=[(w, 0, 1) for w in ffn_w32] + [(w, 1, 0) for w in ffn_w32])
            o_diff, *w_b = diff_attn(dql, dkc, dvc, dkl, dvl, cast=[(w, 1, 1) for w in ffn_w32])
            ffn_w.update({(0, 1): tuple(w_a[:3]), (1, 0): tuple(w_a[3:]), (1, 1): tuple(w_b)})
        zl = _out_proj(zl, mod, l, 1, o_mla, o_diff, _fourier(ul, SEQ), w_o)
        if last:
            zl = ffn(zl, 1, 1, final_g=final_norm)
        else:
            zc = _out_proj(zc, mod, l, 0, _mla_attention(qc, kc, vc), diff_attn(dqc, dkc, dvc),
                           _fourier(uc, CTX_LEN), w_o)
            zl = ffn(zl, 1, 1)
            zc = ffn(zc, 1, 0)
    return zl.reshape(BATCH, SEQ, D_MODEL)
```

```python
import functools
import math

import numpy as np
import jax
import jax.numpy as jnp
from jax import lax
from jax.experimental import pallas as pl
from jax.experimental.pallas import tpu as pltpu

D_MODEL = 2048
BATCH = 8
SEQ = 2048
DEPTH = 2
GRID_W = 64
CTX_LEN = 256
EPS = 1e-6
ROPE_THETA = 10000.0
D_FF = 5632
N_MOD = 9

MLA_HEADS = 8
MLA_Q_RANK = 512
MLA_KV_RANK = 256
MLA_NOPE = 128
MLA_ROPE = 64
MLA_V = 128
MLA_SCALE = (MLA_NOPE + MLA_ROPE) ** -0.5
MLA_QK_PAD = 256
MLA_V_PAD = 256

DIFF_HEADS = 4
DIFF_QK = 64
DIFF_V = 2 * DIFF_QK
DIFF_SCALE = DIFF_QK ** -0.5

FOURIER_GROUPS = 4
FOURIER_CH = 128
MLA_WIDTH = MLA_HEADS * MLA_V
DIFF_WIDTH = DIFF_HEADS * DIFF_V
FOURIER_WIDTH = FOURIER_GROUPS * FOURIER_CH

SEG = SEQ
LAT_ROWS = BATCH * SEQ
CTX_ROWS = BATCH * CTX_LEN
MOD_ROWS = 16

IN_KROPE = MLA_KV_RANK
IN_REST = MLA_KV_RANK + MLA_ROPE
ROPE_PAD = 128
R_DK = 0
R_DV = 512
R_CQ = 1024
R_DQ = 1536
R_U = 2048
R_COLS = 2560

VMEM_LIMIT = 56 * 1024 * 1024
VMEM_LIMIT_FFN = 63 * 1024 * 1024 + 512 * 1024
LOG2_E = math.log2(math.e)

F32 = jnp.float32
BF16 = jnp.bfloat16


def _silu(v):
    return v / (1.0 + jnp.exp(-v))


def _dot(a, b):
    return jnp.dot(a, b, preferred_element_type=F32)


def _dot_nt(a, b):
    return lax.dot_general(a, b, (((1,), (1,)), ((), ())), preferred_element_type=F32)


def _rms(v):
    return v * lax.rsqrt(jnp.mean(v * v, axis=-1, keepdims=True) + EPS)


def _row_chunks(n_rows, chunk):
    return [slice(r, r + chunk) for r in range(0, n_rows, chunk)]


def _modulate_rows(z_ref, g_ref, mod_ref, h_ref):
    gain = g_ref[...] * (1.0 + mod_ref[1:2, :])
    shift = mod_ref[0:1, :]
    for rows in _row_chunks(z_ref.shape[0], 16):
        h_ref[rows, :] = (_rms(z_ref[rows, :]) * gain + shift).astype(BF16)


def _params(*sem, vmem=VMEM_LIMIT):
    return pltpu.CompilerParams(dimension_semantics=sem, vmem_limit_bytes=vmem)


def _resident(shape, index_map):
    return pl.BlockSpec(shape, index_map, pipeline_mode=pl.Buffered(1))


def _mod_spec(l, seg0, tile, grp, n_grid):
    if n_grid == 1:
        return pl.BlockSpec((None, None, None, 3, D_MODEL), lambda i: (l, seg0 + i * tile // SEG, grp, 0, 0))
    return pl.BlockSpec((None, None, None, 3, D_MODEL), lambda i, f: (l, seg0 + i * tile // SEG, grp, 0, 0))


def _ada_kernel(s_ref, w_ref, b_ref, o_ref):
    s = _silu(s_ref[...])
    o_ref[...] = _dot(s.astype(BF16), w_ref[...].astype(BF16)) + b_ref[...]


def _ada_table(s_in, ada_w, ada_b):
    tn = 1024
    n_cols = N_MOD * D_MODEL
    return pl.pallas_call(
        _ada_kernel,
        out_shape=jax.ShapeDtypeStruct((DEPTH, MOD_ROWS, n_cols), F32),
        grid=(DEPTH, n_cols // tn),
        in_specs=[
            pl.BlockSpec((MOD_ROWS, D_MODEL), lambda l, n: (0, 0)),
            pl.BlockSpec((None, D_MODEL, tn), lambda l, n: (l, 0, n)),
            pl.BlockSpec((None, 1, tn), lambda l, n: (l, 0, n)),
        ],
        out_specs=pl.BlockSpec((None, MOD_ROWS, tn), lambda l, n: (l, 0, n)),
        compiler_params=_params("parallel", "parallel"),
        name="ada_table",
    )(s_in, ada_w, ada_b.reshape(DEPTH, 1, n_cols))


DOWN_COLS = 512


def _ffn_kernel(z_ref, mod_ref, g_ref, wg_ref, wu_ref, wd_ref, *rest, n_f, final):
    if final:
        fn_ref, o_ref, h_scr = rest
    else:
        o_ref, h_scr = rest
    f = pl.program_id(1)
    tm = o_ref.shape[0]

    half = wg_ref.shape[1] // 2

    def gate_up(rows):
        h = h_scr[rows, :]
        return [(_dot(h, wg_ref[:, c:c + half]), _dot(h, wu_ref[:, c:c + half])) for c in (0, half)]

    def activation(gu):
        return (_silu(gu[0]) * gu[1]).astype(BF16)

    def sweep_step(first, last):
        if first:
            parts = []
            for rows in _row_chunks(tm, tm // 4):
                _modulate_rows(z_ref.at[rows], g_ref, mod_ref, h_scr.at[rows])
                parts.append(gate_up(rows))
            gu = [tuple(jnp.concatenate([p[k][i] for p in parts], axis=0) for i in (0, 1)) for k in (0, 1)]
        else:
            gu = gate_up(slice(None))
        a0 = activation(gu[0])
        a1 = activation(gu[1])
        for c in range(0, D_MODEL, DOWN_COLS):
            cols = slice(c, c + DOWN_COLS)
            d = _dot(a0, wd_ref[:half, cols]) + _dot(a1, wd_ref[half:, cols])
            if first:
                o_ref[:, cols] = d
            elif not last:
                o_ref[:, cols] += d
            else:
                half_gate = 0.5 * mod_ref[2:3, cols]
                for rows in _row_chunks(tm, 32):
                    o_ref[rows, cols] = z_ref[rows, cols] + half_gate * (o_ref[rows, cols] + d[rows, :])
        if last and final:
            for rows in _row_chunks(tm, 8):
                o_ref[rows, :] = _rms(o_ref[rows, :]) * fn_ref[...]

    pl.when(f == 0)(functools.partial(sweep_step, True, False))
    pl.when(jnp.logical_and(f > 0, f < n_f - 1))(functools.partial(sweep_step, False, False))
    pl.when(f == n_f - 1)(functools.partial(sweep_step, False, True))


def _ffn(z, mod, l, k, seg0, g, wg, wu, wd, *, final_g=None, tm=1024, tf=512):
    n_rows = z.shape[0]
    n_f = D_FF // tf
    final = final_g is not None
    in_specs = [
        pl.BlockSpec((tm, D_MODEL), lambda i, f: (i, 0)),
        _mod_spec(l, seg0, tm, 2 * k, 2),
        pl.BlockSpec((1, D_MODEL), lambda i, f: (0, 0)),
        pl.BlockSpec((D_MODEL, tf), lambda i, f: (0, f)),
        pl.BlockSpec((D_MODEL, tf), lambda i, f: (0, f)),
        pl.BlockSpec((tf, D_MODEL), lambda i, f: (f, 0)),
    ]
    args = [z, mod, g.reshape(1, D_MODEL), wg, wu, wd]
    if final:
        in_specs.append(pl.BlockSpec((1, D_MODEL), lambda i, f: (0, 0)))
        args.append(final_g.reshape(1, D_MODEL))
    return pl.pallas_call(
        functools.partial(_ffn_kernel, n_f=n_f, final=final),
        out_shape=jax.ShapeDtypeStruct((n_rows, D_MODEL), F32),
        grid=(n_rows // tm, n_f),
        in_specs=in_specs,
        out_specs=pl.BlockSpec((tm, D_MODEL), lambda i, f: (i, 0)),
        scratch_shapes=[pltpu.VMEM((tm, D_MODEL), BF16)],
        compiler_params=_params("parallel", "arbitrary", vmem=VMEM_LIMIT_FFN),
        name="ffn_final" if final else "ffn",
    )(*args)


def _rope(v, cos, sin, lo_half):
    swapped = jnp.where(lo_half, pltpu.roll(v, 112, 1), pltpu.roll(v, 16, 1))
    return v * cos + swapped * sin


def _proj_kernel(*refs, rope, kv_only):
    refs = list(refs)
    x_ref, mod_ref, g_ref, wckv_ref, wkr_ref, wrest_ref, kvn_ref, wukv_ref = refs[:8]
    del refs[:8]
    if not kv_only:
        qn_ref, wuq_ref = refs[:2]
        del refs[:2]
    if rope:
        cos_ref, sin_ref = refs[:2]
        del refs[:2]
    h_scr = refs.pop()
    if kv_only:
        kmla_ref, vmla_ref, dk_ref, dv_ref = refs
    else:
        kmla_ref, vmla_ref, dk_ref, dv_ref, qmla_ref, dq_ref, u_ref = refs

    _modulate_rows(x_ref, g_ref, mod_ref, h_scr)
    hb = h_scr[...]
    if rope:
        lane = lax.broadcasted_iota(jnp.int32, (1, 128), 1)
        rotate = functools.partial(_rope, cos=cos_ref[...], sin=sin_ref[...], lo_half=(lane % 32) < 16)
    else:
        rotate = lambda v: v
    q_scale = MLA_SCALE * LOG2_E
    dq_scale = DIFF_SCALE * LOG2_E

    ckv = _dot(hb, wckv_ref[...])
    cq = None if kv_only else _dot(hb, wrest_ref[:, R_CQ:R_DQ])
    k_rope = _dot(hb, wkr_ref[...])
    dk = _dot(hb, wrest_ref[:, R_DK:R_DV])
    dv = _dot(hb, wrest_ref[:, R_DV:R_CQ])
    if not kv_only:
        dq = _dot(hb, wrest_ref[:, R_DQ:R_U])
        u = _dot(hb, wrest_ref[:, R_U:R_COLS])
    kv = _dot((_rms(ckv) * kvn_ref[...]).astype(BF16), wukv_ref[...])
    if not kv_only:
        q = _dot((_rms(cq) * qn_ref[...]).astype(BF16), wuq_ref[...])

    k_rope = rotate(k_rope).astype(BF16)
    for j in range(DIFF_HEADS):
        dk_ref[:, j * 128:(j + 1) * 128] = rotate(dk[:, j * 128:(j + 1) * 128]).astype(BF16)
    dv_ref[...] = dv.astype(BF16)
    for h in range(MLA_HEADS):
        c0 = h * (MLA_NOPE + MLA_V)
        kmla_ref[:, h * MLA_QK_PAD:h * MLA_QK_PAD + MLA_NOPE] = kv[:, c0:c0 + MLA_NOPE].astype(BF16)
        kmla_ref[:, h * MLA_QK_PAD + MLA_NOPE:(h + 1) * MLA_QK_PAD] = k_rope
        vmla_ref[:, h * MLA_V_PAD:h * MLA_V_PAD + MLA_V] = kv[:, c0 + MLA_NOPE:c0 + MLA_NOPE + MLA_V].astype(BF16)
        vmla_ref[:, h * MLA_V_PAD + MLA_V:(h + 1) * MLA_V_PAD] = jnp.ones((kv.shape[0], MLA_V_PAD - MLA_V), BF16)
    if kv_only:
        return

    for j in range(DIFF_HEADS):
        dq_ref[:, j * 128:(j + 1) * 128] = (rotate(dq[:, j * 128:(j + 1) * 128]) * dq_scale).astype(BF16)
    u_ref[...] = u.astype(BF16)
    for h in range(MLA_HEADS):
        c0 = h * MLA_QK_PAD
        qmla_ref[:, c0:c0 + 128] = (q[:, c0:c0 + 128] * q_scale).astype(BF16)
        qmla_ref[:, c0 + 128:c0 + 256] = (rotate(q[:, c0 + 128:c0 + 256]) * q_scale).astype(BF16)


def _proj(x, mod, l, seg0, g, w_in, kv_norm, w_ukv, q_norm, w_uq, rope_tabs=None, *, kv_only=False, tr=512):
    n_rows = x.shape[0]
    row = lambda w: pl.BlockSpec((tr, w), lambda i: (i, 0))
    vec = lambda w: _resident((None, 1, w), lambda i: (l, 0, 0))
    mat = lambda r, c: _resident((None, r, c), lambda i: (l, 0, 0))
    in_specs = [row(D_MODEL), _mod_spec(l, seg0, tr, 1, 1), _resident((1, D_MODEL), lambda i: (0, 0)),
                mat(D_MODEL, MLA_KV_RANK), mat(D_MODEL, ROPE_PAD), mat(D_MODEL, R_COLS),
                vec(MLA_KV_RANK), mat(MLA_KV_RANK, MLA_HEADS * (MLA_NOPE + MLA_V))]
    args = [x, mod, g.reshape(1, D_MODEL), *w_in, kv_norm.reshape(DEPTH, 1, -1), w_ukv]
    widths = [MLA_HEADS * MLA_QK_PAD, MLA_HEADS * MLA_V_PAD, DIFF_WIDTH, DIFF_WIDTH]
    if not kv_only:
        in_specs += [vec(MLA_Q_RANK), mat(MLA_Q_RANK, MLA_HEADS * MLA_QK_PAD)]
        args += [q_norm.reshape(DEPTH, 1, -1), w_uq]
        widths += [MLA_HEADS * MLA_QK_PAD, DIFF_WIDTH, FOURIER_WIDTH]
    if rope_tabs is not None:
        per_seg = SEG // tr
        in_specs += [pl.BlockSpec((tr, 128), lambda i: (i % per_seg, 0))] * 2
        args += list(rope_tabs)
    return pl.pallas_call(
        functools.partial(_proj_kernel, rope=rope_tabs is not None, kv_only=kv_only),
        out_shape=[jax.ShapeDtypeStruct((n_rows, w), BF16) for w in widths],
        grid=(n_rows // tr,),
        in_specs=in_specs,
        out_specs=[row(w) for w in widths],
        scratch_shapes=[pltpu.VMEM((tr, D_MODEL), BF16)],
        compiler_params=_params("parallel"),
        name="in_proj",
    )(*args)


SUB = 256
CHUNK_VREGS = 36


def _half_rows(n_keys):
    half = 8
    while 2 * half * n_keys <= CHUNK_VREGS * 1024 and 4 * half <= SUB:
        half *= 2
    return half


def _attn_refs(refs, lat):
    if not lat:
        q_ref, kc_ref, vc_ref, o_ref = refs
        return q_ref, kc_ref, vc_ref, None, None, o_ref
    n_cast = (len(refs) - 6) // 2
    for src, dst in zip(refs[5:5 + n_cast], refs[6 + n_cast:]):
        dst[...] = src[...].astype(BF16)
    return (*refs[:5], refs[5 + n_cast])


def _cast_rows(n_rows, n_steps):
    return min(r for r in range(16, n_rows + 1, 16) if n_rows % r == 0 and n_rows // r <= n_steps)


MLA_AHEAD = 1
DIFF_AHEAD = 2


def _pipelined(items, scores, softmax, values, n_slots):
    ahead = n_slots - 1
    for k in range(min(ahead, len(items))):
        scores(k % n_slots, *items[k])
    for i, item in enumerate(items):
        if i + ahead < len(items):
            scores((i + ahead) % n_slots, *items[i + ahead])
        softmax(i % n_slots, i % 2, *item)
        if i > 0:
            values((i - 1) % 2, *items[i - 1])
    values((len(items) - 1) % 2, *items[-1])


def _store_scores(s_scr, slot, q, kc, kl):
    s_scr[slot, :, :CTX_LEN] = _dot_nt(q, kc)
    if kl is not None:
        s_scr[slot, :, CTX_LEN:] = _dot_nt(q, kl)


def _weighted_values(p_scr, slot, vc, vl):
    o = _dot(p_scr[slot, :, :CTX_LEN], vc)
    if vl is not None:
        o = o + _dot(p_scr[slot, :, CTX_LEN:], vl)
    return o


def _chunk_numerators(s):
    return jnp.exp2(s - jnp.max(s, axis=-1, keepdims=True))


def _mla_kernel(*refs, heads, n_sub, lat):
    *io, s_scr, p_scr = refs
    q_ref, kc_ref, vc_ref, kl_ref, vl_ref, o_ref = _attn_refs(io, lat)

    def scores(slot, hh, j):
        qk = slice(hh * MLA_QK_PAD, (hh + 1) * MLA_QK_PAD)
        _store_scores(s_scr, slot, q_ref[j * SUB:(j + 1) * SUB, qk], kc_ref[:, qk], kl_ref[:, qk] if lat else None)

    def softmax(slot, pslot, hh, j):
        half = _half_rows(s_scr.shape[-1])
        for r in range(0, SUB, 2 * half):
            e = [_chunk_numerators(s_scr[slot, r0:r0 + half, :]) for r0 in (r, r + half)]
            p_scr[pslot, r:r + 2 * half, :] = jnp.concatenate(e, axis=0).astype(BF16)

    def values(pslot, hh, j):
        vv = slice(hh * MLA_V_PAD, (hh + 1) * MLA_V_PAD)
        o = _weighted_values(p_scr, pslot, vc_ref[:, vv], vl_ref[:, vv] if lat else None)
        o_ref[j * SUB:(j + 1) * SUB, hh * MLA_V:(hh + 1) * MLA_V] = (o[:, :MLA_V] / o[:, MLA_V:]).astype(BF16)

    _pipelined([(hh, j) for hh in range(heads) for j in range(n_sub)], scores, softmax, values, s_scr.shape[0])


def _mla_scratch(n_keys):
    return [pltpu.VMEM((MLA_AHEAD + 1, SUB, n_keys), F32), pltpu.VMEM((2, SUB, n_keys), BF16)]


def _diff_kernel(lam_ref, sub_ref, *refs, heads, n_sub, lat, lam_init):
    *io, s1_scr, s2_scr, p_scr, d_scr = refs
    q_ref, kc_ref, vc_ref, kl_ref, vl_ref, o_ref = _attn_refs(io, lat)
    lf = lam_ref[...]
    lam = (jnp.exp(jnp.sum(lf[0:1] * lf[1:2], axis=-1, keepdims=True))
           - jnp.exp(jnp.sum(lf[2:3] * lf[3:4], axis=-1, keepdims=True)) + lam_init)
    first = lax.broadcasted_iota(jnp.int32, (1, 2 * DIFF_QK), 1) < DIFF_QK

    def scores(slot, hh, j):
        cols = slice(hh * DIFF_V, (hh + 1) * DIFF_V)
        q = q_ref[j * SUB:(j + 1) * SUB, cols]
        kc = kc_ref[:, cols]
        kl = kl_ref[:, cols] if lat else None
        _store_scores(s1_scr, slot, jnp.where(first, q, jnp.zeros_like(q)), kc, kl)
        _store_scores(s2_scr, slot, jnp.where(first, jnp.zeros_like(q), q), kc, kl)

    def softmax(slot, pslot, hh, j):
        half = _half_rows(s1_scr.shape[-1])
        for r in range(0, SUB, 2 * half):
            p = []
            for r0 in (r, r + half):
                e1 = _chunk_numerators(s1_scr[slot, r0:r0 + half, :])
                e2 = _chunk_numerators(s2_scr[slot, r0:r0 + half, :])
                d1 = jnp.sum(e1, axis=-1, keepdims=True)
                d2 = jnp.sum(e2, axis=-1, keepdims=True)
                p.append(e1 - e2 * (lam * d1 / d2))
                d_scr[pslot, r0:r0 + half, :] = jnp.broadcast_to(d1, (half, DIFF_V))
            p_scr[pslot, r:r + 2 * half, :] = jnp.concatenate(p, axis=0).astype(BF16)

    def values(pslot, hh, j):
        cols = slice(hh * DIFF_V, (hh + 1) * DIFF_V)
        o = _weighted_values(p_scr, pslot, vc_ref[:, cols], vl_ref[:, cols] if lat else None) / d_scr[pslot]
        o_ref[j * SUB:(j + 1) * SUB, cols] = (_rms(o) * sub_ref[...] * (1.0 - lam_init)).astype(BF16)

    _pipelined([(hh, j) for hh in range(heads) for j in range(n_sub)], scores, softmax, values, s1_scr.shape[0])


def _diff_scratch(n_keys):
    return [pltpu.VMEM((DIFF_AHEAD + 1, SUB, n_keys), F32), pltpu.VMEM((DIFF_AHEAD + 1, SUB, n_keys), F32),
            pltpu.VMEM((2, SUB, n_keys), BF16), pltpu.VMEM((2, SUB, DIFF_V), F32)]


def _attention(kernel_fn, scratch_fn, name, q, k_ctx, v_ctx, k_lat=None, v_lat=None, *, n_heads, w_qk, w_v, w_o,
               tq, extra_args=(), extra_specs=(), cast=()):
    if k_lat is not None:
        per_seg = SEQ // tq
        n_steps = BATCH * n_heads * per_seg
        in_specs = list(extra_specs) + [
            pl.BlockSpec((tq, w_qk), lambda b, h, t: (b * per_seg + t, h)),
            pl.BlockSpec((CTX_LEN, w_qk), lambda b, h, t: (b, h)),
            pl.BlockSpec((CTX_LEN, w_v), lambda b, h, t: (b, h)),
            pl.BlockSpec((SEQ, w_qk), lambda b, h, t: (b, h)),
            pl.BlockSpec((SEQ, w_v), lambda b, h, t: (b, h)),
        ]
        out_shape = [jax.ShapeDtypeStruct((LAT_ROWS, n_heads * w_o), BF16)]
        out_specs = [pl.BlockSpec((tq, w_o), lambda b, h, t: (b * per_seg + t, h))]
        for w, l, k in cast:
            n_r, n_c = w.shape[2:]
            rows = _cast_rows(n_r, n_steps)
            blk = lambda b, h, t, last=n_r // rows - 1: jnp.minimum((b * n_heads + h) * per_seg + t, last)
            in_specs.append(pl.BlockSpec((None, None, rows, n_c), lambda b, h, t, l=l, k=k, blk=blk: (l, k, blk(b, h, t), 0)))
            out_shape.append(jax.ShapeDtypeStruct((n_r, n_c), BF16))
            out_specs.append(pl.BlockSpec((rows, n_c), lambda b, h, t, blk=blk: (blk(b, h, t), 0)))
        out = pl.pallas_call(
            functools.partial(kernel_fn, heads=1, n_sub=tq // SUB, lat=True),
            out_shape=out_shape,
            grid=(BATCH, n_heads, per_seg),
            in_specs=in_specs,
            out_specs=out_specs,
            scratch_shapes=scratch_fn(CTX_LEN + SEQ),
            compiler_params=_params(*(("arbitrary",) * 3 if cast else ("parallel", "parallel", "arbitrary"))),
            name=name,
        )(*extra_args, q, k_ctx, v_ctx, k_lat, v_lat, *[w for w, _, _ in cast])
        return out if cast else out[0]
    in_specs = list(extra_specs) + [
        pl.BlockSpec((CTX_LEN, n_heads * w_qk), lambda b: (b, 0)),
        pl.BlockSpec((CTX_LEN, n_heads * w_qk), lambda b: (b, 0)),
        pl.BlockSpec((CTX_LEN, n_heads * w_v), lambda b: (b, 0)),
    ]
    return pl.pallas_call(
        functools.partial(kernel_fn, heads=n_heads, n_sub=CTX_LEN // SUB, lat=False),
        out_shape=jax.ShapeDtypeStruct((CTX_ROWS, n_heads * w_o), BF16),
        grid=(BATCH,),
        in_specs=in_specs,
        out_specs=pl.BlockSpec((CTX_LEN, n_heads * w_o), lambda b: (b, 0)),
        scratch_shapes=scratch_fn(CTX_LEN),
        compiler_params=_params("parallel"),
        name=name + "_ctx",
    )(*extra_args, q, k_ctx, v_ctx)


def _mla_attention(q, k_ctx, v_ctx, k_lat=None, v_lat=None, *, tq=2048, cast=()):
    return _attention(_mla_kernel, _mla_scratch, "mla_attention", q, k_ctx, v_ctx, k_lat, v_lat,
                      n_heads=MLA_HEADS, w_qk=MLA_QK_PAD, w_v=MLA_V_PAD, w_o=MLA_V, tq=tq, cast=cast)


def _diff_attention(diff_lambda, subln, l, lam_init, q, k_ctx, v_ctx, k_lat=None, v_lat=None, *, tq=2048, cast=()):
    layer = lambda *_: (l, 0, 0)
    extra_specs = (pl.BlockSpec((None, 4, DIFF_QK), layer), pl.BlockSpec((None, 1, DIFF_V), layer))
    return _attention(functools.partial(_diff_kernel, lam_init=lam_init), _diff_scratch, "diff_attention",
                      q, k_ctx, v_ctx, k_lat, v_lat,
                      n_heads=DIFF_HEADS, w_qk=2 * DIFF_QK, w_v=DIFF_V, w_o=DIFF_V, tq=tq,
                      extra_args=(diff_lambda, subln.reshape(DEPTH, 1, DIFF_V)), extra_specs=extra_specs, cast=cast)


def _dft_tables(n_pos):
    def table(n_k, positions):
        k = np.arange(n_k, dtype=np.int64)[:, None]
        ang = 2.0 * np.pi * ((k * positions[None, :]) % n_pos) / n_pos
        return jnp.asarray(np.concatenate([np.cos(ang), -np.sin(ang)], axis=1) / math.sqrt(n_pos), dtype=BF16)

    n = np.arange(n_pos, dtype=np.int64)
    tabs = (table(n_pos // 4, n[0::4]), table(n_pos // 4, n[2::4]), table(n_pos // 2, n[1::2]))
    c = np.arange(FOURIER_CH, dtype=np.int64)
    angc = 2.0 * np.pi * ((c[:, None] * c[None, :]) % FOURIER_CH) / FOURIER_CH
    ch = np.concatenate([np.cos(angc), np.sin(angc)], axis=1) / math.sqrt(FOURIER_CH)
    return tabs, jnp.asarray(ch, dtype=BF16)


def _fourier_kernel(u_ref, ch_ref, t0_ref, t2_ref, to_ref, o_ref, a_scr, rhs0, rhs2, rhs_o, *, n_pos):
    for g in range(FOURIER_GROUPS):
        cols = slice(g * FOURIER_CH, (g + 1) * FOURIER_CH)
        a = _dot(u_ref[:, cols], ch_ref[...])
        a_scr[2 * g] = a[:, :FOURIER_CH]
        a_scr[2 * g + 1] = a[:, FOURIER_CH:]
    for g in range(FOURIER_GROUPS):
        cols = slice(g * FOURIER_CH, (g + 1) * FOURIER_CH)
        for first, stride, rhs in ((0, 4, rhs0), (2, 4, rhs2), (1, 2, rhs_o)):
            n_rows = n_pos // stride
            picked = pl.ds(first, n_rows, stride=stride)
            rhs[0:n_rows, cols] = a_scr[2 * g, picked, :].astype(BF16)
            rhs[n_rows:2 * n_rows, cols] = a_scr[2 * g + 1, picked, :].astype(BF16)
    pp = _dot(t0_ref[...], rhs0[...])
    pq = _dot(t2_ref[...], rhs2[...])
    q = _dot(to_ref[...], rhs_o[...])
    p = jnp.concatenate([pp + pq, pp - pq], axis=0)
    half = n_pos // 2
    o_ref[0:half, :] = (p + q).astype(BF16)
    o_ref[half:n_pos, :] = (p - q).astype(BF16)


def _fourier(u, n_pos):
    (t0, t2, t_odd), ch_t = _dft_tables(n_pos)
    return pl.pallas_call(
        functools.partial(_fourier_kernel, n_pos=n_pos),
        out_shape=jax.ShapeDtypeStruct((BATCH * n_pos, FOURIER_WIDTH), BF16),
        grid=(BATCH,),
        in_specs=[
            pl.BlockSpec((n_pos, FOURIER_WIDTH), lambda b: (b, 0)),
            _resident((FOURIER_CH, 2 * FOURIER_CH), lambda b: (0, 0)),
            _resident(t0.shape, lambda b: (0, 0)),
            _resident(t2.shape, lambda b: (0, 0)),
            _resident(t_odd.shape, lambda b: (0, 0)),
        ],
        out_specs=pl.BlockSpec((n_pos, FOURIER_WIDTH), lambda b: (b, 0)),
        scratch_shapes=[pltpu.VMEM((2 * FOURIER_GROUPS, n_pos, FOURIER_CH), F32),
                        pltpu.VMEM((n_pos // 2, FOURIER_WIDTH), BF16), pltpu.VMEM((n_pos // 2, FOURIER_WIDTH), BF16),
                        pltpu.VMEM((n_pos, FOURIER_WIDTH), BF16)],
        compiler_params=_params("parallel"),
        name="fourier_mix",
    )(u, ch_t, t0, t2, t_odd)


def _out_kernel(x_ref, mod_ref, om_ref, od_ref, of_ref, w_ref, o_ref):
    for c in range(0, D_MODEL, DOWN_COLS):
        cols = slice(c, c + DOWN_COLS)
        y = (_dot(om_ref[...], w_ref[0:MLA_WIDTH, cols])
             + _dot(od_ref[...], w_ref[MLA_WIDTH:MLA_WIDTH + DIFF_WIDTH, cols])
             + _dot(of_ref[...], w_ref[MLA_WIDTH + DIFF_WIDTH:, cols]))
        o_ref[:, cols] = x_ref[:, cols] + mod_ref[2:3, cols] * y


def _out_proj(x, mod, l, seg0, o_mla, o_diff, o_four, w_out, *, tr=512):
    n_rows = x.shape[0]
    row = lambda w: pl.BlockSpec((tr, w), lambda i: (i, 0))
    return pl.pallas_call(
        _out_kernel,
        out_shape=jax.ShapeDtypeStruct((n_rows, D_MODEL), F32),
        grid=(n_rows // tr,),
        in_specs=[
            row(D_MODEL), _mod_spec(l, seg0, tr, 1, 1),
            row(MLA_WIDTH), row(DIFF_WIDTH), row(FOURIER_WIDTH),
            _resident((None, D_MODEL, D_MODEL), lambda i: (l, 0, 0)),
        ],
        out_specs=row(D_MODEL),
        compiler_params=_params("parallel"),
        name="out_proj",
    )(x, mod, o_mla, o_diff, o_four, w_out)


def _rope_tables():
    rows = SEQ // GRID_W
    pos_r = jnp.repeat(jnp.arange(rows), GRID_W)
    pos_c = jnp.tile(jnp.arange(GRID_W), rows)
    d = MLA_ROPE // 2
    half = d // 2
    inv = ROPE_THETA ** (-2.0 * jnp.arange(half, dtype=F32) / d)

    def tabs(pos):
        ang = pos.astype(F32)[:, None] * inv[None, :]
        return jnp.cos(ang), jnp.sin(ang)

    cr, sr = tabs(pos_r)
    cc, sc = tabs(pos_c)
    cos64 = jnp.concatenate([cr, cr, cc, cc], axis=-1)
    sin64 = jnp.concatenate([-sr, sr, -sc, sc], axis=-1)
    return jnp.tile(cos64, (1, 2)), jnp.tile(sin64, (1, 2))


def kernel(x, c, ctx, c_ctx, ada_w, ada_b, norm_g, ffn_wg, ffn_wu, ffn_wd, w_in, mla_q_norm, mla_kv_norm,
           mla_w_uq, mla_w_ukv, diff_lambda, diff_subln, w_out, final_norm):
    assert DEPTH == 2
    s_in = jnp.concatenate([c_ctx[None, :], c, jnp.zeros((MOD_ROWS - 1 - BATCH, D_MODEL), F32)], axis=0)
    w_in_b = w_in.astype(BF16)
    w_in_p = (w_in_b[..., :IN_KROPE],
              jnp.pad(w_in_b[..., IN_KROPE:IN_REST], ((0, 0), (0, 0), (0, ROPE_PAD - MLA_ROPE))),
              w_in_b[..., IN_REST:])
    w_uq_p = jnp.pad(mla_w_uq.astype(BF16).reshape(DEPTH, MLA_Q_RANK, MLA_HEADS, MLA_NOPE + MLA_ROPE),
                     ((0, 0), (0, 0), (0, 0), (0, MLA_QK_PAD - MLA_NOPE - MLA_ROPE))
                     ).reshape(DEPTH, MLA_Q_RANK, MLA_HEADS * MLA_QK_PAD)
    w_ukv = mla_w_ukv.astype(BF16)
    w_o = w_out.astype(BF16)
    rope_tabs = _rope_tables()
    ffn_w32 = (ffn_wg, ffn_wu, ffn_wd)
    ffn_w = {(0, 0): tuple(w[0, 0].astype(BF16) for w in ffn_w32)}

    mod = _ada_table(s_in, ada_w, ada_b).reshape(DEPTH, MOD_ROWS, 3, 3, D_MODEL)

    zl = x.reshape(LAT_ROWS, D_MODEL)
    zc = ctx.reshape(CTX_ROWS, D_MODEL)
    for l in range(DEPTH):
        last = l == DEPTH - 1
        lam_init = 0.8 - 0.6 * math.exp(-0.3 * l)
        proj = functools.partial(_proj, mod=mod, l=l, g=norm_g[l, 1], w_in=w_in_p, kv_norm=mla_kv_norm,
                                 w_ukv=w_ukv, q_norm=mla_q_norm, w_uq=w_uq_p)
        diff_attn = functools.partial(_diff_attention, diff_lambda, diff_subln, l, lam_init)

        def ffn(z, k, seg0, **kw):
            wg, wu, wd = ffn_w[l, k]
            return _ffn(z, mod, l, k, seg0, norm_g[l, 2 * k], wg, wu, wd, **kw)

        zl = ffn(zl, 0, 1)
        zc = ffn(zc, 0, 0)
        kl, vl, dkl, dvl, ql, dql, ul = proj(zl, seg0=1, rope_tabs=rope_tabs)
        if last:
            kc, vc, dkc, dvc = proj(zc, seg0=0, kv_only=True)
            o_mla = _mla_attention(ql, kc, vc, kl, vl)
            o_diff = diff_attn(dql, dkc, dvc, dkl, dvl)
        else:
            kc, vc, dkc, dvc, qc, dqc, uc = proj(zc, seg0=0)
            o_mla, *w_a = _mla_attention(ql, kc, vc, kl, vl,
                                         cast=[(w, 0, 1) for w in ffn_w32] + [(w, 1, 0) for w in ffn_w32])
            o_diff, *w_b = diff_attn(dql, dkc, dvc, dkl, dvl, cast=[(w, 1, 1) for w in ffn_w32])
            ffn_w.update({(0, 1): tuple(w_a[:3]), (1, 0): tuple(w_a[3:]), (1, 1): tuple(w_b)})
        zl = _out_proj(zl, mod, l, 1, o_mla, o_diff, _fourier(ul, SEQ), w_o)
        if last:
            zl = ffn(zl, 1, 1, final_g=final_norm)
        else:
            zc = _out_proj(zc, mod, l, 0, _mla_attention(qc, kc, vc), diff_attn(dqc, dkc, dvc),
                           _fourier(uc, CTX_LEN), w_o)
            zl = ffn(zl, 1, 1)
            zc = ffn(zc, 1, 0)
    return zl.reshape(BATCH, SEQ, D_MODEL)
```
